```python
import math
import jax
import jax.numpy as jnp
from jax import lax
import numpy as np

D_MODEL = 1024
BATCH = 4
SEQ = 4096
DEPTH = 4
DEC_BATCH = 128
DEC_SEQ = 1
PAST_LEN = 2048
PAGE_SIZE = 128

D_PLE = 256
NORM_EPS = 1e-6
RET_HEADS = 4
RET_DK = 128
RET_DV = 256
RET_CHUNK = 128
ROPE_BASE = 10000.0
DIFF_HEADS = 8
DIFF_KV_HEADS = 4
DIFF_GROUP = DIFF_HEADS // DIFF_KV_HEADS
DIFF_DH = 64
DIFF_DV = 2 * DIFF_DH
Q_BLOCK = 128
REL_BUCKETS = 32
REL_MAX_DIST = 128
RNN_WIDTH = 1024
RNN_BLOCKS = 8
RNN_BLOCK_W = RNN_WIDTH // RNN_BLOCKS
CONV_WIDTH = 4
LRU_C = 8.0
RET_QK_W = RET_HEADS * RET_DK
RET_V_W = RET_HEADS * RET_DV
DIFF_Q_W = DIFF_HEADS * 2 * DIFF_DH
DIFF_K_W = DIFF_KV_HEADS * 2 * DIFF_DH
DIFF_V_W = DIFF_KV_HEADS * DIFF_DV
DIFF_OUT_W = DIFF_HEADS * DIFF_DV
IN_SPLITS = (RET_QK_W, RET_QK_W, RET_V_W, RET_V_W,
             DIFF_Q_W, DIFF_K_W, DIFF_V_W, DIFF_OUT_W,
             RNN_WIDTH, RNN_WIDTH, 3 * D_MODEL)
IN_WIDTH = sum(IN_SPLITS)

kernel_name = "hybrid_retention_diffattn_rglru_step"

F32 = jnp.float32


def rmsnorm(x, g):
    xf = x.astype(F32)
    y = xf * lax.rsqrt(jnp.mean(xf * xf, axis=-1, keepdims=True) + NORM_EPS)
    return (y * g.astype(F32)).astype(x.dtype)


def group_rms(x):
    xf = x.astype(F32)
    return xf * lax.rsqrt(jnp.mean(xf * xf, axis=-1, keepdims=True) + NORM_EPS)


def rotary(x, pos):
    half = x.shape[-1] // 2
    inv = ROPE_BASE ** (-jnp.arange(half, dtype=F32) / half)
    ang = pos.astype(F32)[:, None] * inv[None, :]
    cos = jnp.cos(ang)[None, :, None, :]
    sin = jnp.sin(ang)[None, :, None, :]
    xf = x.astype(F32)
    x1, x2 = xf[..., :half], xf[..., half:]
    return jnp.concatenate([x1 * cos - x2 * sin, x1 * sin + x2 * cos], axis=-1).astype(x.dtype)


def retention(q, k, v, s0):
    b, s, h, dk = q.shape
    dv = v.shape[-1]
    c = math.gcd(s, RET_CHUNK)
    n = s // c
    log_g = jnp.log1p(-(2.0 ** (-5.0 - jnp.arange(h, dtype=F32))))
    idx = jnp.arange(c, dtype=F32)
    dist = idx[:, None] - idx[None, :]
    dmask = jnp.where(dist[None] >= 0, jnp.exp(jnp.maximum(dist, 0.0)[None] * log_g[:, None, None]), 0.0)
    q_dec = jnp.exp((idx + 1.0)[:, None] * log_g[None, :])
    k_dec = jnp.exp((c - 1.0 - idx)[:, None] * log_g[None, :])
    chunk_dec = jnp.exp(c * log_g)
    qc = q.astype(F32).reshape(b, n, c, h, dk).swapaxes(0, 1)
    kc = k.astype(F32).reshape(b, n, c, h, dk).swapaxes(0, 1)
    vc = v.astype(F32).reshape(b, n, c, h, dv).swapaxes(0, 1)

    def step(state, inp):
        qi, ki, vi = inp
        scores = jnp.einsum('bihd,bjhd->bhij', qi, ki) * dmask[None]
        o = (jnp.einsum('bhij,bjhv->bihv', scores, vi)
             + jnp.einsum('bihd,bhdv->bihv', qi, state) * q_dec[None, :, :, None])
        state = (state * chunk_dec[None, :, None, None]
                 + jnp.einsum('bjhd,bjhv->bhdv', ki * k_dec[None, :, :, None], vi))
        return state, o

    s_fin, o = lax.scan(step, s0.astype(F32), (qc, kc, vc))
    o = o.swapaxes(0, 1).reshape(b, s, h, dv)
    return o, s_fin


def t5_bucket(rel):
    n = jnp.maximum(rel, 0)
    max_exact = REL_BUCKETS // 2
    nf = jnp.maximum(n.astype(F32), 1.0)
    large = max_exact + (jnp.log(nf / max_exact) / math.log(REL_MAX_DIST / max_exact)
                         * (REL_BUCKETS - max_exact)).astype(jnp.int32)
    large = jnp.minimum(large, REL_BUCKETS - 1)
    return jnp.where(n < max_exact, n, large)


def diff_attention(q, k, v, q_pos, k_pos, lam, rel_bias):
    b, sq = q.shape[0], q.shape[1]
    sk = k.shape[1]
    blk = math.gcd(sq, Q_BLOCK)
    nb = sq // blk
    qb = q.reshape((b, nb, blk) + q.shape[2:]).swapaxes(0, 1)
    pb = q_pos.reshape(nb, blk)
    scale = DIFF_DH ** -0.5

    def one(args):
        qi, pi = args
        s = jnp.einsum('bqkgmd,bskmd->bkgmqs', qi, k).astype(F32) * scale
        rel = pi[:, None] - k_pos[None, :]
        bias = rel_bias.astype(F32)[t5_bucket(rel)]
        bias = jnp.moveaxis(bias, -1, 0).reshape(DIFF_KV_HEADS, DIFF_GROUP, blk, sk)
        s = s + bias[None, :, :, None]
        s = jnp.where((rel >= 0)[None, None, None, None], s, -jnp.inf)
        pr = jax.nn.softmax(s, axis=-1)
        a = pr[:, :, :, 0] - lam * pr[:, :, :, 1]
        return jnp.einsum('bkgqs,bskd->bqkgd', a.astype(v.dtype), v)

    o = lax.map(one, (qb, pb))
    return o.swapaxes(0, 1).reshape(b, sq, DIFF_HEADS, DIFF_DV)


def rglru(xb, conv_prev, h0, conv_w, conv_b, wa, ba, wx, bx, lru_lambda):
    b, s, r = xb.shape
    xp = jnp.concatenate([conv_prev.astype(xb.dtype), xb], axis=1)
    xc = conv_b
    for j in range(CONV_WIDTH):
        xc = xc + xp[:, j:j + s] * conv_w[j]
    new_conv = xp[:, s:]
    xbk = xc.reshape(b, s, RNN_BLOCKS, RNN_BLOCK_W)
    rg = jax.nn.sigmoid(jnp.einsum('bsnc,ncd->bsnd', xbk, wa).reshape(b, s, r) + ba)
    ig = jax.nn.sigmoid(jnp.einsum('bsnc,ncd->bsnd', xbk, wx).reshape(b, s, r) + bx)
    log_a = -LRU_C * rg.astype(F32) * jax.nn.softplus(-lru_lambda.astype(F32))
    a = jnp.exp(log_a)
    bt = jnp.sqrt(-jnp.expm1(2.0 * log_a)) * (ig * xc).astype(F32)
    bt = bt.at[:, 0].add(a[:, 0] * h0.astype(F32))

    def comb(left, right):
        a1, b1 = left
        a2, b2 = right
        return a1 * a2, a2 * b1 + b2

    _, h = lax.associative_scan(comb, (a, bt), axis=1)
    return h, h[:, -1], new_conv


def lambda_init(layer):
    return 0.8 - 0.6 * math.exp(-0.3 * layer)


def trunk_layer(x, p, pos, past_k, past_v, s0, h0, conv0, layer, w):
    dt = x.dtype
    b, s, _ = x.shape
    hn = rmsnorm(x, w['norm_g'])
    z = hn @ w['w_in']
    split_idx = [int(i) for i in np.cumsum(IN_SPLITS)[:-1]]
    rq, rk, rv, rg, dq, dk, dv, dg, lx, lg, mg = jnp.split(z, split_idx, axis=-1)

    rq = rotary(rq.reshape(b, s, RET_HEADS, RET_DK), pos)
    rk = rotary(rk.reshape(b, s, RET_HEADS, RET_DK), pos) * (RET_DK ** -0.5)
    ro, s_new = retention(rq, rk, rv.reshape(b, s, RET_HEADS, RET_DV), s0)
    ro = group_rms(ro).reshape(b, s, RET_V_W).astype(dt) * jax.nn.silu(rg)

    dq = dq.reshape(b, s, DIFF_KV_HEADS, DIFF_GROUP, 2, DIFF_DH)
    k_rows = dk.reshape(b, s, DIFF_KV_HEADS, 2 * DIFF_DH)
    v_rows = dv.reshape(b, s, DIFF_KV_HEADS, DIFF_DV)
    if past_k is None:
        k_all, v_all, k_pos = k_rows, v_rows, pos
    else:
        n_past = past_k.shape[1]
        k_all = jnp.concatenate([past_k.astype(dt), k_rows], axis=1)
        v_all = jnp.concatenate([past_v.astype(dt), v_rows], axis=1)
        k_pos = jnp.concatenate([jnp.arange(n_past, dtype=jnp.int32), pos])
    k_all = k_all.reshape(b, k_all.shape[1], DIFF_KV_HEADS, 2, DIFF_DH)
    lam0 = lambda_init(layer)
    lam = (jnp.exp(jnp.sum(w['lq1'].astype(F32) * w['lk1'].astype(F32)))
           - jnp.exp(jnp.sum(w['lq2'].astype(F32) * w['lk2'].astype(F32))) + lam0)
    do = diff_attention(dq, k_all, v_all, pos, k_pos, lam, w['rel_bias'])
    do = rmsnorm(do, w['subln_g']) * (1.0 - lam0)
    do = do.reshape(b, s, DIFF_OUT_W) * jax.nn.silu(dg)

    lh, h_last, conv_new = rglru(lx, conv0, h0, w['conv_w'], w['conv_b'], w['gate_a_w'],
                                 w['gate_a_b'], w['gate_x_w'], w['gate_x_b'], w['lru_lambda'])
    lo = lh.astype(dt) * jax.nn.silu(lg)

    g_r, g_d, g_l = jnp.split(jax.nn.sigmoid(mg), 3, axis=-1)
    merged = (g_r * (ro @ w['w_ret_out']) + g_d * (do @ w['w_diff_out'])
              + g_l * (lo @ w['w_lru_out']))
    x = x + merged @ w['w_o']

    x = x + jax.nn.sigmoid(x @ w['w_ple_gate']) * (p @ w['w_ple'])
    return x, k_rows, v_rows, s_new.astype(dt), h_last.astype(dt), conv_new


def setup_inputs(seed: int = 0) -> dict:
    key = jax.random.key(seed)
    ks = jax.random.split(key, 40)
    n_pages = PAST_LEN // PAGE_SIZE
    n_used = DEC_BATCH * n_pages
    n_pool = n_used + (n_used + 3) // 4
    nrm = jax.random.normal
    a0 = jax.random.uniform(ks[20], (DEPTH, RNN_WIDTH), F32, 0.9, 0.999)
    return {
        "x_prompt": nrm(ks[0], (BATCH, SEQ, D_MODEL), F32),
        "x_sample": nrm(ks[1], (DEC_BATCH, DEC_SEQ, D_MODEL), F32),
        "cache_k": nrm(ks[2], (n_pool, DEPTH, PAGE_SIZE, DIFF_KV_HEADS, 2 * DIFF_DH), F32),
        "cache_v": nrm(ks[3], (n_pool, DEPTH, PAGE_SIZE, DIFF_KV_HEADS, DIFF_DV), F32),
        "state_ret": 0.3 * nrm(ks[4], (DEC_BATCH, DEPTH, RET_HEADS, RET_DK, RET_DV), F32),
        "state_lru": 0.5 * nrm(ks[5], (DEC_BATCH, DEPTH, RNN_WIDTH), F32),
        "state_conv": nrm(ks[6], (DEC_BATCH, DEPTH, CONV_WIDTH - 1, RNN_WIDTH), F32),
        "page_table": jax.random.permutation(ks[7], n_pool)[:n_used].reshape(DEC_BATCH, n_pages).astype(jnp.int32),
        "p_prompt": nrm(ks[8], (DEPTH, BATCH, SEQ, D_PLE), F32),
        "p_sample": nrm(ks[9], (DEPTH, DEC_BATCH, DEC_SEQ, D_PLE), F32),
        "rel_bias": 0.1 * nrm(ks[10], (REL_BUCKETS, DIFF_HEADS), F32),
        "norm_g": 1.0 + 0.05 * nrm(ks[11], (DEPTH, D_MODEL), F32),
        "w_in": nrm(ks[12], (DEPTH, D_MODEL, IN_WIDTH), F32) * D_MODEL ** -0.5,
        "lambda_q1": 0.1 * nrm(ks[13], (DEPTH, DIFF_DH), F32),
        "lambda_k1": 0.1 * nrm(ks[14], (DEPTH, DIFF_DH), F32),
        "lambda_q2": 0.1 * nrm(ks[15], (DEPTH, DIFF_DH), F32),
        "lambda_k2": 0.1 * nrm(ks[16], (DEPTH, DIFF_DH), F32),
        "subln_g": 1.0 + 0.05 * nrm(ks[17], (DEPTH, DIFF_DV), F32),
        "conv_w": nrm(ks[18], (DEPTH, CONV_WIDTH, RNN_WIDTH), F32) * CONV_WIDTH ** -0.5,
        "conv_b": 0.01 * nrm(ks[19], (DEPTH, RNN_WIDTH), F32),
        "gate_a_w": nrm(ks[21], (DEPTH, RNN_BLOCKS, RNN_BLOCK_W, RNN_BLOCK_W), F32) * RNN_BLOCK_W ** -0.5,
        "gate_a_b": 0.01 * nrm(ks[22], (DEPTH, RNN_WIDTH), F32),
        "gate_x_w": nrm(ks[23], (DEPTH, RNN_BLOCKS, RNN_BLOCK_W, RNN_BLOCK_W), F32) * RNN_BLOCK_W ** -0.5,
        "gate_x_b": 0.01 * nrm(ks[24], (DEPTH, RNN_WIDTH), F32),
        "lru_lambda": jnp.log(a0) - jnp.log1p(-a0),
        "w_ret_out": nrm(ks[25], (DEPTH, RET_V_W, D_MODEL), F32) * RET_V_W ** -0.5,
        "w_diff_out": nrm(ks[26], (DEPTH, DIFF_OUT_W, D_MODEL), F32) * DIFF_OUT_W ** -0.5,
        "w_lru_out": nrm(ks[27], (DEPTH, RNN_WIDTH, D_MODEL), F32) * RNN_WIDTH ** -0.5,
        "w_o": nrm(ks[28], (DEPTH, D_MODEL, D_MODEL), F32) * D_MODEL ** -0.5,
        "w_ple": nrm(ks[29], (DEPTH, D_PLE, D_MODEL), F32) * D_PLE ** -0.5,
        "w_ple_gate": nrm(ks[30], (DEPTH, D_MODEL, D_MODEL), F32) * D_MODEL ** -0.5,
        "final_norm_g": 1.0 + 0.05 * nrm(ks[31], (D_MODEL,), F32),
    }


def reference(x_prompt, x_sample, cache_k, cache_v, state_ret, state_lru, state_conv, page_table,
              p_prompt, p_sample, rel_bias, norm_g, w_in, lambda_q1, lambda_k1, lambda_q2, lambda_k2,
              subln_g, conv_w, conv_b, gate_a_w, gate_a_b, gate_x_w, gate_x_b, lru_lambda,
              w_ret_out, w_diff_out, w_lru_out, w_o, w_ple, w_ple_gate, final_norm_g):
    dt = x_prompt.dtype
    bp, sp = x_prompt.shape[0], x_prompt.shape[1]
    bs, ss = x_sample.shape[0], x_sample.shape[1]
    n_pages = page_table.shape[1]
    page = cache_k.shape[2]
    past = n_pages * page
    pos_p = jnp.arange(sp, dtype=jnp.int32)
    pos_s = past + jnp.arange(ss, dtype=jnp.int32)
    s0_p = jnp.zeros((bp, RET_HEADS, RET_DK, RET_DV), F32)
    h0_p = jnp.zeros((bp, RNN_WIDTH), F32)
    c0_p = jnp.zeros((bp, CONV_WIDTH - 1, RNN_WIDTH), dt)

    hp, hs = x_prompt, x_sample
    kp, vp, ks_, vs_, rp, rs, lp, ls, cp, cs = [], [], [], [], [], [], [], [], [], []
    for l in range(DEPTH):
        w = dict(norm_g=norm_g[l], w_in=w_in[l], lq1=lambda_q1[l], lk1=lambda_k1[l],
                 lq2=lambda_q2[l], lk2=lambda_k2[l], subln_g=subln_g[l], rel_bias=rel_bias,
                 conv_w=conv_w[l], conv_b=conv_b[l], gate_a_w=gate_a_w[l], gate_a_b=gate_a_b[l],
                 gate_x_w=gate_x_w[l], gate_x_b=gate_x_b[l], lru_lambda=lru_lambda[l],
                 w_ret_out=w_ret_out[l], w_diff_out=w_diff_out[l], w_lru_out=w_lru_out[l],
                 w_o=w_o[l], w_ple=w_ple[l], w_ple_gate=w_ple_gate[l])
        hp, k1, v1, r1, l1, c1 = trunk_layer(hp, p_prompt[l], pos_p, None, None,
                                             s0_p, h0_p, c0_p, l, w)
        past_k = cache_k[page_table, l].reshape(bs, past, DIFF_KV_HEADS, 2 * DIFF_DH)
        past_v = cache_v[page_table, l].reshape(bs, past, DIFF_KV_HEADS, DIFF_DV)
        hs, k2, v2, r2, l2, c2 = trunk_layer(hs, p_sample[l], pos_s, past_k, past_v,
                                             state_ret[:, l], state_lru[:, l], state_conv[:, l], l, w)
        kp.append(k1); vp.append(v1); rp.append(r1); lp.append(l1); cp.append(c1)
        ks_.append(k2); vs_.append(v2); rs.append(r2); ls.append(l2); cs.append(c2)

    y_prompt = rmsnorm(hp, final_norm_g)
    y_sample = rmsnorm(hs, final_norm_g)
    k_prompt = jnp.stack(kp, axis=1)
    v_prompt = jnp.stack(vp, axis=1)
    k_sample = jnp.stack(ks_, axis=1)
    v_sample = jnp.stack(vs_, axis=1)
    ret_prompt = jnp.stack(rp, axis=1)
    ret_sample = jnp.stack(rs, axis=1)
    lru_prompt = jnp.stack(lp, axis=1)
    lru_sample = jnp.stack(ls, axis=1)
    conv_prompt = jnp.stack(cp, axis=1)
    conv_sample = jnp.stack(cs, axis=1)
    return (y_prompt, y_sample, k_prompt, v_prompt, k_sample, v_sample,
            ret_prompt, ret_sample, lru_prompt, lru_sample, conv_prompt, conv_sample)
```

```python
import functools
import math

import numpy as np
import jax
import jax.numpy as jnp
from jax import lax
from jax.experimental import pallas as pl
from jax.experimental.pallas import tpu as pltpu

F32 = jnp.float32
BF16 = jnp.bfloat16

D_MODEL = 1024
D_PLE = 256
NORM_EPS = 1e-6
RET_HEADS = 4
RET_DK = 128
RET_DV = 256
RET_CHUNK = 128
ROPE_BASE = 10000.0
DIFF_HEADS = 8
DIFF_KV_HEADS = 4
DIFF_GROUP = DIFF_HEADS // DIFF_KV_HEADS
DIFF_DH = 64
DIFF_DV = 2 * DIFF_DH
REL_BUCKETS = 32
REL_MAX_DIST = 128
RNN_WIDTH = 1024
RNN_BLOCKS = 8
RNN_BLOCK_W = RNN_WIDTH // RNN_BLOCKS
CONV_WIDTH = 4
LRU_C = 8.0

RET_QK_W = RET_HEADS * RET_DK
RET_V_W = RET_HEADS * RET_DV
DIFF_Q_W = DIFF_HEADS * 2 * DIFF_DH
DIFF_K_W = DIFF_KV_HEADS * 2 * DIFF_DH
DIFF_V_W = DIFF_KV_HEADS * DIFF_DV
DIFF_OUT_W = DIFF_HEADS * DIFF_DV
IN_SPLITS = (RET_QK_W, RET_QK_W, RET_V_W, RET_V_W,
             DIFF_Q_W, DIFF_K_W, DIFF_V_W, DIFF_OUT_W,
             RNN_WIDTH, RNN_WIDTH, 3 * D_MODEL)
IN_OFFS = tuple(int(v) for v in np.cumsum((0,) + IN_SPLITS))

LANES = 128
SUBLANES = 8
MASK_VALUE = -1e30
MIB = 1024 * 1024


def _cparams(sem, vmem_mib):
    return pltpu.CompilerParams(dimension_semantics=sem, vmem_limit_bytes=int(vmem_mib * MIB))


def _rmsnorm_kernel(x_ref, g_ref, o_ref):
    x = x_ref[...]
    y = x * lax.rsqrt(jnp.mean(x * x, axis=-1, keepdims=True) + NORM_EPS)
    o_ref[...] = (y * g_ref[...]).astype(o_ref.dtype)


def _rmsnorm(x, g, out_dtype, tm):
    t, d = x.shape
    return pl.pallas_call(
        _rmsnorm_kernel,
        grid=(t // tm,),
        in_specs=[pl.BlockSpec((tm, d), lambda i: (i, 0)),
                  pl.BlockSpec((1, d), lambda i: (0, 0))],
        out_specs=pl.BlockSpec((tm, d), lambda i: (i, 0)),
        out_shape=jax.ShapeDtypeStruct((t, d), out_dtype),
        compiler_params=_cparams(("parallel",), 32),
        name="rmsnorm",
    )(x, g.reshape(1, d))


def _proj_kernel(x_ref, w_ref, *refs, kind):
    acc = jnp.dot(x_ref[...], w_ref[...], preferred_element_type=F32)
    if kind == "plain":
        (o_ref,) = refs
        o_ref[...] = acc.astype(o_ref.dtype)
    elif kind == "silu":
        (o_ref,) = refs
        o_ref[...] = (acc * jax.nn.sigmoid(acc)).astype(o_ref.dtype)
    elif kind == "sigmoid":
        (o_ref,) = refs
        o_ref[...] = jax.nn.sigmoid(acc).astype(o_ref.dtype)
    elif kind == "rotary":
        cos_ref, sin_ref, o_ref = refs
        cos = cos_ref[...]
        sin = sin_ref[...]
        for h in range(2 * RET_HEADS):
            z = acc[:, h * RET_DK:(h + 1) * RET_DK]
            r = z * cos + pltpu.roll(z, RET_DK // 2, 1) * sin
            if h >= RET_HEADS:
                r = r * (RET_DK ** -0.5)
            o_ref[:, h * RET_DK:(h + 1) * RET_DK] = r.astype(o_ref.dtype)
    elif kind == "kv":
        kf_ref, vf_ref, kb_ref, vb_ref = refs
        k = acc[:, :DIFF_K_W]
        v = acc[:, DIFF_K_W:]
        kf_ref[...] = k
        vf_ref[...] = v
        kb_ref[...] = k.astype(BF16)
        vb_ref[...] = v.astype(BF16)
    else:
        raise ValueError(kind)


def _proj(hn, w, kind, tm, out_dtype=BF16, tables=None):
    t, d = hn.shape
    n = w.shape[1]
    tn = 1024
    grid = (t // tm, n // tn)
    in_specs = [pl.BlockSpec((tm, d), lambda i, j: (i, 0)),
                pl.BlockSpec((d, tn), lambda i, j: (0, j))]
    args = [hn, w]
    if kind == "rotary":
        cos, sin = tables
        nb = cos.shape[0] // tm
        spec = pl.BlockSpec((tm, RET_DK), lambda i, j: (i % nb, 0))
        in_specs += [spec, spec]
        args += [cos, sin]
    if kind == "kv":
        half = pl.BlockSpec((tm, DIFF_K_W), lambda i, j: (i, 0))
        out_specs = [half, half, half, half]
        out_shape = [jax.ShapeDtypeStruct((t, DIFF_K_W), F32),
                     jax.ShapeDtypeStruct((t, DIFF_V_W), F32),
                     jax.ShapeDtypeStruct((t, DIFF_K_W), BF16),
                     jax.ShapeDtypeStruct((t, DIFF_V_W), BF16)]
    else:
        out_specs = pl.BlockSpec((tm, tn), lambda i, j: (i, j))
        out_shape = jax.ShapeDtypeStruct((t, n), out_dtype)
    return pl.pallas_call(
        functools.partial(_proj_kernel, kind=kind),
        grid=grid, in_specs=in_specs, out_specs=out_specs, out_shape=out_shape,
        compiler_params=_cparams(("parallel", "arbitrary"), 40),
        name="proj_" + kind,
    )(*args)


def _retention_kernel(qk_ref, v_ref, gate_ref, dmask_ref, qdec_ref, kdec_ref,
                      o_ref, sfin_ref, state_ref, *, chunk_dec):
    c = pl.program_id(1)

    @pl.when(c == 0)
    def _():
        state_ref[...] = jnp.zeros_like(state_ref)

    for h in range(RET_HEADS):
        q = qk_ref[:, h * RET_DK:(h + 1) * RET_DK]
        k = qk_ref[:, RET_QK_W + h * RET_DK:RET_QK_W + (h + 1) * RET_DK]
        v = v_ref[:, h * RET_DV:(h + 1) * RET_DV]
        st = state_ref[h]
        s = lax.dot_general(q, k, (((1,), (1,)), ((), ())), preferred_element_type=F32)
        s = s * dmask_ref[h]
        o = jnp.dot(s.astype(BF16), v, preferred_element_type=F32)
        o = o + jnp.dot(q, st.astype(BF16), preferred_element_type=F32) * qdec_ref[:, h:h + 1]
        kd = (k.astype(F32) * kdec_ref[:, h:h + 1]).astype(BF16)
        state_ref[h] = st * chunk_dec[h] + lax.dot_general(
            kd, v, (((0,), (0,)), ((), ())), preferred_element_type=F32)
        o = o * lax.rsqrt(jnp.mean(o * o, axis=-1, keepdims=True) + NORM_EPS)
        g = gate_ref[:, h * RET_DV:(h + 1) * RET_DV].astype(F32)
        o_ref[:, h * RET_DV:(h + 1) * RET_DV] = (o * g).astype(o_ref.dtype)

    @pl.when(c == pl.num_programs(1) - 1)
    def _():
        sfin_ref[...] = state_ref[...]


def _retention_prompt(qk, rvdq, gates, b, s):
    c = math.gcd(s, RET_CHUNK)
    nc = s // c
    log_g = np.log1p(-(2.0 ** (-5.0 - np.arange(RET_HEADS, dtype=np.float64))))
    idx = np.arange(c, dtype=np.float64)
    dist = idx[:, None] - idx[None, :]
    dmask = np.where(dist[None] >= 0, np.exp(np.maximum(dist, 0.0)[None] * log_g[:, None, None]), 0.0)
    qdec = np.exp((idx + 1.0)[:, None] * log_g[None, :])
    kdec = np.exp((c - 1.0 - idx)[:, None] * log_g[None, :])
    chunk_dec = tuple(float(x) for x in np.exp(c * log_g).astype(np.float32))
    t = b * s
    return pl.pallas_call(
        functools.partial(_retention_kernel, chunk_dec=chunk_dec),
        grid=(b, nc),
        in_specs=[pl.BlockSpec((c, 2 * RET_QK_W), lambda i, j: (i * nc + j, 0)),
                  pl.BlockSpec((c, RET_V_W), lambda i, j: (i * nc + j, 0)),
                  pl.BlockSpec((c, RET_V_W), lambda i, j: (i * nc + j, 0)),
                  pl.BlockSpec((RET_HEADS, c, c), lambda i, j: (0, 0, 0)),
                  pl.BlockSpec((c, RET_HEADS), lambda i, j: (0, 0)),
                  pl.BlockSpec((c, RET_HEADS), lambda i, j: (0, 0))],
        out_specs=[pl.BlockSpec((c, RET_V_W), lambda i, j: (i * nc + j, 0)),
                   pl.BlockSpec((None, RET_HEADS, RET_DK, RET_DV), lambda i, j: (i, 0, 0, 0))],
        out_shape=[jax.ShapeDtypeStruct((t, RET_V_W), BF16),
                   jax.ShapeDtypeStruct((b, RET_HEADS, RET_DK, RET_DV), F32)],
        scratch_shapes=[pltpu.VMEM((RET_HEADS, RET_DK, RET_DV), F32)],
        compiler_params=_cparams(("parallel", "arbitrary"), 32),
        name="retention_prompt",
    )(qk, rvdq, gates, jnp.asarray(dmask, F32), jnp.asarray(qdec, F32), jnp.asarray(kdec, F32))


def _retention_step_kernel(qkc_ref, v_ref, gate_ref, s_ref, o_ref, snew_ref, *, decay):
    for h in range(RET_HEADS):
        qc = qkc_ref[h]
        kc = qkc_ref[RET_HEADS + h]
        v = v_ref[:, h * RET_DV:(h + 1) * RET_DV]
        st = s_ref[h]
        qk = jnp.sum(qc * kc, axis=0, keepdims=True)
        o = qk * v + decay[h] * jnp.sum(qc * st, axis=0, keepdims=True)
        snew_ref[h] = st * decay[h] + kc * v
        o = o * lax.rsqrt(jnp.mean(o * o, axis=-1, keepdims=True) + NORM_EPS)
        o_ref[:, h * RET_DV:(h + 1) * RET_DV] = o * gate_ref[:, h * RET_DV:(h + 1) * RET_DV]


def _retention_step(qk, v, gate, state_ret, layer):
    bs = qk.shape[0]
    log_g = np.log1p(-(2.0 ** (-5.0 - np.arange(RET_HEADS, dtype=np.float64))))
    decay = tuple(float(x) for x in np.exp(log_g).astype(np.float32))
    qkc = qk.astype(F32).reshape(bs, 2 * RET_HEADS, RET_DK, 1)
    v3 = v.astype(F32).reshape(bs, 1, RET_V_W)
    g3 = gate.astype(F32).reshape(bs, 1, RET_V_W)
    o, snew = pl.pallas_call(
        functools.partial(_retention_step_kernel, decay=decay),
        grid=(bs,),
        in_specs=[pl.BlockSpec((None, 2 * RET_HEADS, RET_DK, 1), lambda i: (i, 0, 0, 0)),
                  pl.BlockSpec((None, 1, RET_V_W), lambda i: (i, 0, 0)),
                  pl.BlockSpec((None, 1, RET_V_W), lambda i: (i, 0, 0)),
                  pl.BlockSpec((None, None, RET_HEADS, RET_DK, RET_DV),
                               lambda i: (i, layer, 0, 0, 0))],
        out_specs=[pl.BlockSpec((None, 1, RET_V_W), lambda i: (i, 0, 0)),
                   pl.BlockSpec((None, RET_HEADS, RET_DK, RET_DV), lambda i: (i, 0, 0, 0))],
        out_shape=[jax.ShapeDtypeStruct((bs, 1, RET_V_W), F32),
                   jax.ShapeDtypeStruct((bs, RET_HEADS, RET_DK, RET_DV), F32)],
        compiler_params=_cparams(("parallel",), 32),
        name="retention_step",
    )(qkc, v3, g3, state_ret)
    return o.reshape(bs, RET_V_W).astype(BF16), snew


def _t5_bucket(rel):
    n = jnp.maximum(rel, 0)
    max_exact = REL_BUCKETS // 2
    nf = jnp.maximum(n.astype(F32), 1.0)
    large = max_exact + (jnp.log(nf / max_exact) / math.log(REL_MAX_DIST / max_exact)
                         * (REL_BUCKETS - max_exact)).astype(jnp.int32)
    large = jnp.minimum(large, REL_BUCKETS - 1)
    return jnp.where(n < max_exact, n, large)


def _bias_lookup(tab_ref, bucket, head):
    out = jnp.zeros(bucket.shape, F32)
    far = tab_ref[REL_BUCKETS - 1, head]
    for bkt in range(REL_BUCKETS - 1):
        out = jnp.where(bucket == bkt, tab_ref[bkt, head] - far, out)
    return out


def _bias_prompt_kernel(tab_ref, o_ref, *, blk):
    i = lax.broadcasted_iota(jnp.int32, (blk, blk), 0)
    j = lax.broadcasted_iota(jnp.int32, (blk, blk), 1)
    for which in range(2):
        rel = i - j + which * blk
        bucket = _t5_bucket(rel)
        for h in range(DIFF_HEADS):
            o_ref[which, h] = jnp.where(rel >= 0, _bias_lookup(tab_ref, bucket, h), MASK_VALUE)


def _bias_prompt(rel_bias, blk):
    return pl.pallas_call(
        functools.partial(_bias_prompt_kernel, blk=blk),
        in_specs=[pl.BlockSpec(memory_space=pltpu.SMEM)],
        out_specs=pl.BlockSpec(memory_space=pltpu.VMEM),
        out_shape=jax.ShapeDtypeStruct((2, DIFF_HEADS, blk, blk), F32),
        compiler_params=pltpu.CompilerParams(vmem_limit_bytes=32 * MIB),
        name="bias_prompt",
    )(rel_bias)


def _bias_sample_kernel(tab_ref, o_ref, onew_ref, *, past, width):
    rows = 2 * DIFF_HEADS
    r = lax.broadcasted_iota(jnp.int32, (rows, width), 0)
    lane = lax.broadcasted_iota(jnp.int32, (rows, width), 1)
    head = r % DIFF_HEADS
    kvh = lane % DIFF_KV_HEADS
    pos = lane // DIFF_KV_HEADS
    bucket = _t5_bucket(past - pos)
    out = jnp.zeros((rows, width), F32)
    for h in range(DIFF_HEADS):
        out = jnp.where(head == h, _bias_lookup(tab_ref, bucket, h), out)
    o_ref[...] = jnp.where(kvh == head // DIFF_GROUP, out, MASK_VALUE)
    rn = lax.broadcasted_iota(jnp.int32, (rows, LANES), 0) % DIFF_HEADS
    new = jnp.zeros((rows, LANES), F32)
    zero_bucket = _t5_bucket(jnp.zeros((rows, LANES), jnp.int32))
    for h in range(DIFF_HEADS):
        new = jnp.where(rn == h, _bias_lookup(tab_ref, zero_bucket, h), new)
    onew_ref[...] = new


def _bias_sample(rel_bias, past):
    width = past * DIFF_KV_HEADS
    return pl.pallas_call(
        functools.partial(_bias_sample_kernel, past=past, width=width),
        in_specs=[pl.BlockSpec(memory_space=pltpu.SMEM)],
        out_specs=[pl.BlockSpec(memory_space=pltpu.VMEM), pl.BlockSpec(memory_space=pltpu.VMEM)],
        out_shape=[jax.ShapeDtypeStruct((2 * DIFF_HEADS, width), F32),
                   jax.ShapeDtypeStruct((2 * DIFF_HEADS, LANES), F32)],
        compiler_params=pltpu.CompilerParams(vmem_limit_bytes=32 * MIB),
        name="bias_sample",
    )(rel_bias)


def _lambda_full(lam_ref, lam0):
    a = jnp.sum(lam_ref[0:1, :] * lam_ref[1:2, :], axis=-1, keepdims=True)
    b = jnp.sum(lam_ref[2:3, :] * lam_ref[3:4, :], axis=-1, keepdims=True)
    return jnp.exp(a) - jnp.exp(b) + lam0


def _attn_kernel(q_ref, k_ref, v_ref, bias_ref, gate_ref, sg_ref, lam_ref, o_ref,
                 qt_ref, m_ref, l_ref, acc_ref, *, blk, lam0):
    qi = pl.program_id(2)
    lane = lax.broadcasted_iota(jnp.int32, (blk, DIFF_DV), 1)
    for g in range(DIFF_GROUP):
        qg = q_ref[:, g * DIFF_DV:(g + 1) * DIFF_DV]
        for m in range(2):
            keep = (lane < DIFF_DH) if m == 0 else (lane >= DIFF_DH)
            r0 = (m * DIFF_GROUP + g) * blk
            qt_ref[r0:r0 + blk, :] = jnp.where(keep, qg, jnp.zeros_like(qg))
    m_ref[...] = jnp.full(m_ref.shape, MASK_VALUE, F32)
    l_ref[...] = jnp.zeros(l_ref.shape, F32)
    acc_ref[...] = jnp.zeros(acc_ref.shape, F32)

    def step(ki, bias_idx):
        start = pl.multiple_of(ki * blk, blk)
        kb = k_ref[pl.ds(start, blk), :]
        vb = v_ref[pl.ds(start, blk), :]
        s = lax.dot_general(qt_ref[...], kb, (((1,), (1,)), ((), ())),
                            preferred_element_type=F32)
        if bias_idx is not None:
            b0 = bias_ref[bias_idx, 0]
            b1 = bias_ref[bias_idx, 1]
            s = s + jnp.concatenate([b0, b1, b0, b1], axis=0)
        m_prev = m_ref[...]
        m_new = jnp.maximum(m_prev, jnp.max(s, axis=1, keepdims=True))
        p = jnp.exp(s - m_new[:, 0:1])
        alpha = jnp.exp(m_prev - m_new)
        l_ref[...] = alpha * l_ref[...] + jnp.sum(p, axis=1, keepdims=True)
        acc_ref[...] = acc_ref[...] * alpha + jnp.dot(p.astype(BF16), vb,
                                                      preferred_element_type=F32)
        m_ref[...] = m_new

    def far_body(ki, carry):
        step(ki, None)
        return carry

    lax.fori_loop(0, qi - 1, far_body, 0)

    @pl.when(qi >= 1)
    def _():
        step(qi - 1, 1)

    step(qi, 0)

    lam = _lambda_full(lam_ref, lam0)
    a = acc_ref[...] / l_ref[...]
    for g in range(DIFF_GROUP):
        o = a[g * blk:(g + 1) * blk] - lam * a[(DIFF_GROUP + g) * blk:(DIFF_GROUP + g + 1) * blk]
        o = o * lax.rsqrt(jnp.mean(o * o, axis=-1, keepdims=True) + NORM_EPS)
        o = o * sg_ref[...] * (1.0 - lam0)
        gate = gate_ref[:, g * DIFF_DV:(g + 1) * DIFF_DV].astype(F32)
        o_ref[:, g * DIFF_DV:(g + 1) * DIFF_DV] = (o * gate).astype(o_ref.dtype)


def _attention_prompt(rvdq, kb, vb, bias, gates, subln_g, lam_rows, lam0, b, s, blk):
    nq = s // blk
    t = b * s
    qcol0 = RET_V_W // (DIFF_GROUP * DIFF_DV)
    gcol0 = RET_V_W // (DIFF_GROUP * DIFF_DV)
    kvw = 2 * DIFF_DH
    return pl.pallas_call(
        functools.partial(_attn_kernel, blk=blk, lam0=lam0),
        grid=(b, DIFF_KV_HEADS, nq),
        in_specs=[pl.BlockSpec((blk, DIFF_GROUP * DIFF_DV), lambda i, h, q: (i * nq + q, qcol0 + h)),
                  pl.BlockSpec((s, kvw), lambda i, h, q: (i, h)),
                  pl.BlockSpec((s, DIFF_DV), lambda i, h, q: (i, h)),
                  pl.BlockSpec((2, DIFF_GROUP, blk, blk), lambda i, h, q: (0, h, 0, 0)),
                  pl.BlockSpec((blk, DIFF_GROUP * DIFF_DV), lambda i, h, q: (i * nq + q, gcol0 + h)),
                  pl.BlockSpec((1, DIFF_DV), lambda i, h, q: (0, 0)),
                  pl.BlockSpec((4, DIFF_DH), lambda i, h, q: (0, 0))],
        out_specs=pl.BlockSpec((blk, DIFF_GROUP * DIFF_DV), lambda i, h, q: (i * nq + q, h)),
        out_shape=jax.ShapeDtypeStruct((t, DIFF_OUT_W), BF16),
        scratch_shapes=[pltpu.VMEM((4 * blk, DIFF_DV), BF16),
                        pltpu.VMEM((4 * blk, LANES), F32),
                        pltpu.VMEM((4 * blk, LANES), F32),
                        pltpu.VMEM((4 * blk, DIFF_DV), F32)],
        compiler_params=_cparams(("parallel", "parallel", "arbitrary"), 48),
        name="attention_prompt",
    )(rvdq, kb, vb, bias, gates, subln_g.reshape(1, DIFF_DV), lam_rows)


def _attn_decode_kernel(pt_ref, *refs, n_pages, rows_per_page, lam0):
    del pt_ref
    k_refs = refs[:n_pages]
    v_refs = refs[n_pages:2 * n_pages]
    (qt_ref, kn_ref, vn_ref, bias_ref, bnew_ref, gate_ref, sg_ref, lam_ref,
     o_ref, s_ref) = refs[2 * n_pages:]
    qt = qt_ref[...]
    w = rows_per_page
    for p in range(n_pages):
        kp = k_refs[p][...].astype(BF16)
        sp = lax.dot_general(qt, kp, (((1,), (1,)), ((), ())), preferred_element_type=F32)
        s_ref[:, p * w:(p + 1) * w] = sp + bias_ref[:, p * w:(p + 1) * w]
    s = s_ref[...]
    s_new = jnp.sum(qt.astype(F32) * kn_ref[...].astype(F32), axis=1, keepdims=True)
    s_new = s_new + bnew_ref[:, 0:1]
    m = jnp.maximum(jnp.max(s, axis=1, keepdims=True), s_new)
    e_new = jnp.exp(s_new - m)
    pr = jnp.exp(s - m)
    l = jnp.sum(pr, axis=1, keepdims=True) + e_new
    prb = pr.astype(BF16)
    acc = e_new * vn_ref[...].astype(F32)
    for p in range(n_pages):
        vp = v_refs[p][...].astype(BF16)
        acc = acc + jnp.dot(prb[:, p * w:(p + 1) * w], vp, preferred_element_type=F32)
    a = acc / l
    lam = _lambda_full(lam_ref, lam0)
    o = a[:DIFF_HEADS] - lam * a[DIFF_HEADS:]
    o = o * lax.rsqrt(jnp.mean(o * o, axis=-1, keepdims=True) + NORM_EPS)
    o = o * sg_ref[...] * (1.0 - lam0)
    o_ref[...] = o * gate_ref[...]


def _attention_decode(dq, kb_new, vb_new, gate, cache_k, cache_v, page_table, bias, bias_new,
                      subln_g, lam_rows, lam0, layer):
    bs = dq.shape[0]
    n_pool, depth, page = cache_k.shape[0], cache_k.shape[1], cache_k.shape[2]
    n_pages = page_table.shape[1]
    w = page * DIFF_KV_HEADS
    ck = cache_k.reshape(n_pool, depth, w, 2 * DIFF_DH)
    cv = cache_v.reshape(n_pool, depth, w, DIFF_DV)
    q = dq.reshape(bs, DIFF_KV_HEADS, DIFF_GROUP, 2, DIFF_DH)
    q = jnp.transpose(q, (0, 3, 1, 2, 4)).reshape(bs, 2, DIFF_HEADS, DIFF_DH)
    z = jnp.zeros_like(q[:, 0])
    qt = jnp.concatenate([jnp.concatenate([q[:, 0], z], axis=-1),
                          jnp.concatenate([z, q[:, 1]], axis=-1)], axis=1)
    kn = jnp.tile(jnp.repeat(kb_new.reshape(bs, DIFF_KV_HEADS, 2 * DIFF_DH), DIFF_GROUP, axis=1), (1, 2, 1))
    vn = jnp.tile(jnp.repeat(vb_new.reshape(bs, DIFF_KV_HEADS, DIFF_DV), DIFF_GROUP, axis=1), (1, 2, 1))
    g3 = gate.astype(F32).reshape(bs, DIFF_HEADS, DIFF_DV)
    rows = 2 * DIFF_HEADS

    def page_spec(p):
        return pl.BlockSpec((None, None, w, 2 * DIFF_DH), lambda i, pt: (pt[i, p], layer, 0, 0))

    row_spec = pl.BlockSpec((None, rows, LANES), lambda i, pt: (i, 0, 0))
    in_specs = ([page_spec(p) for p in range(n_pages)] + [page_spec(p) for p in range(n_pages)]
                + [row_spec, row_spec, row_spec,
                   pl.BlockSpec((rows, n_pages * w), lambda i, pt: (0, 0)),
                   pl.BlockSpec((rows, LANES), lambda i, pt: (0, 0)),
                   pl.BlockSpec((None, DIFF_HEADS, DIFF_DV), lambda i, pt: (i, 0, 0)),
                   pl.BlockSpec((1, DIFF_DV), lambda i, pt: (0, 0)),
                   pl.BlockSpec((4, DIFF_DH), lambda i, pt: (0, 0))])
    out = pl.pallas_call(
        functools.partial(_attn_decode_kernel, n_pages=n_pages, rows_per_page=w, lam0=lam0),
        grid_spec=pltpu.PrefetchScalarGridSpec(
            num_scalar_prefetch=1, grid=(bs,), in_specs=in_specs,
            out_specs=pl.BlockSpec((None, DIFF_HEADS, DIFF_DV), lambda i, pt: (i, 0, 0)),
            scratch_shapes=[pltpu.VMEM((rows, n_pages * w), F32)]),
        out_shape=jax.ShapeDtypeStruct((bs, DIFF_HEADS, DIFF_DV), F32),
        compiler_params=_cparams(("arbitrary",), 48),
        name="attention_decode",
    )(page_table, *([ck] * n_pages), *([cv] * n_pages), qt, kn, vn, bias, bias_new, g3,
      subln_g.reshape(1, DIFF_DV), lam_rows)
    return out.reshape(bs, DIFF_OUT_W).astype(BF16)


def _softplus(x):
    return jnp.maximum(x, 0.0) + jnp.log1p(jnp.exp(-jnp.abs(x)))


def _lru_coeffs(xc, wg_ref, ba_ref, bx_ref, lam_ref, n):
    sl = slice(n * RNN_BLOCK_W, (n + 1) * RNN_BLOCK_W)
    xn = xc[:, sl]
    z = jnp.dot(xn.astype(BF16), wg_ref[n], preferred_element_type=F32)
    rg = jax.nn.sigmoid(z[:, :RNN_BLOCK_W] + ba_ref[:, sl])
    ig = jax.nn.sigmoid(z[:, RNN_BLOCK_W:] + bx_ref[:, sl])
    log_a = -LRU_C * rg * _softplus(-lam_ref[:, sl])
    a = jnp.exp(log_a)
    bt = jnp.sqrt(-jnp.tanh(log_a) * (1.0 + a * a)) * (ig * xn)
    return a, bt


def _lru_prompt_kernel(x_ref, gate_ref, cw_ref, cb_ref, wg_ref, ba_ref, bx_ref, lam_ref,
                       o_ref, hlast_ref, conv_ref, xbuf_ref, a_ref, b_ref, h_ref, *, tm):
    t = pl.program_id(1)
    pad = SUBLANES

    @pl.when(t == 0)
    def _():
        xbuf_ref[0:pad, :] = jnp.zeros((pad, RNN_WIDTH), F32)
        h_ref[...] = jnp.zeros(h_ref.shape, F32)

    xbuf_ref[pad:pad + tm, :] = x_ref[...]
    xc = cb_ref[...]
    for j in range(CONV_WIDTH):
        off = pad - (CONV_WIDTH - 1) + j
        xc = xc + xbuf_ref[off:off + tm, :] * cw_ref[j:j + 1, :]
    conv_ref[...] = xbuf_ref[pad + tm - (CONV_WIDTH - 1):pad + tm, :]
    xbuf_ref[0:pad, :] = xbuf_ref[tm:tm + pad, :]

    for n in range(RNN_BLOCKS):
        a, bt = _lru_coeffs(xc, wg_ref, ba_ref, bx_ref, lam_ref, n)
        sl = slice(n * RNN_BLOCK_W, (n + 1) * RNN_BLOCK_W)
        r8 = lax.broadcasted_iota(jnp.int32, a.shape, 0) % SUBLANES
        for d in (1, 2, 4):
            a_sh = pltpu.roll(a, d, 0)
            b_sh = pltpu.roll(bt, d, 0)
            ok = r8 >= d
            bt = jnp.where(ok, a * b_sh + bt, bt)
            a = jnp.where(ok, a * a_sh, a)
        a_ref[:, sl] = a
        b_ref[:, sl] = bt

    h = h_ref[...]
    for gi in range(tm // SUBLANES):
        rs = slice(gi * SUBLANES, (gi + 1) * SUBLANES)
        hg = a_ref[rs, :] * h + b_ref[rs, :]
        b_ref[rs, :] = hg
        h = hg[SUBLANES - 1:SUBLANES, :]
    h_ref[...] = h
    hlast_ref[...] = h
    o_ref[...] = (b_ref[...] * gate_ref[...].astype(F32)).astype(o_ref.dtype)


def _lru_prompt(lx, gates, conv_w, conv_b, wg, ba, bx, lam, b, s, tm):
    nt = s // tm
    t = b * s
    gcol = 2
    vec = pl.BlockSpec((1, RNN_WIDTH), lambda i, j: (0, 0))
    return pl.pallas_call(
        functools.partial(_lru_prompt_kernel, tm=tm),
        grid=(b, nt),
        in_specs=[pl.BlockSpec((tm, RNN_WIDTH), lambda i, j: (i * nt + j, 0)),
                  pl.BlockSpec((tm, RNN_WIDTH), lambda i, j: (i * nt + j, gcol)),
                  pl.BlockSpec((CONV_WIDTH, RNN_WIDTH), lambda i, j: (0, 0)),
                  vec,
                  pl.BlockSpec((RNN_BLOCKS, RNN_BLOCK_W, 2 * RNN_BLOCK_W), lambda i, j: (0, 0, 0)),
                  vec, vec, vec],
        out_specs=[pl.BlockSpec((tm, RNN_WIDTH), lambda i, j: (i * nt + j, 0)),
                   pl.BlockSpec((None, 1, RNN_WIDTH), lambda i, j: (i, 0, 0)),
                   pl.BlockSpec((None, CONV_WIDTH - 1, RNN_WIDTH), lambda i, j: (i, 0, 0))],
        out_shape=[jax.ShapeDtypeStruct((t, RNN_WIDTH), BF16),
                   jax.ShapeDtypeStruct((b, 1, RNN_WIDTH), F32),
                   jax.ShapeDtypeStruct((b, CONV_WIDTH - 1, RNN_WIDTH), F32)],
        scratch_shapes=[pltpu.VMEM((tm + SUBLANES, RNN_WIDTH), F32),
                        pltpu.VMEM((tm, RNN_WIDTH), F32),
                        pltpu.VMEM((tm, RNN_WIDTH), F32),
                        pltpu.VMEM((1, RNN_WIDTH), F32)],
        compiler_params=_cparams(("parallel", "arbitrary"), 32),
        name="lru_prompt",
    )(lx, gates, conv_w, conv_b.reshape(1, -1), wg, ba.reshape(1, -1), bx.reshape(1, -1),
      lam.reshape(1, -1))


def _lru_step_kernel(x_ref, c0_ref, c1_ref, c2_ref, h0_ref, gate_ref, cw_ref, cb_ref, wg_ref,
                     ba_ref, bx_ref, lam_ref, o_ref, h_ref):
    xc = (cb_ref[...] + c0_ref[...] * cw_ref[0:1, :] + c1_ref[...] * cw_ref[1:2, :]
          + c2_ref[...] * cw_ref[2:3, :] + x_ref[...] * cw_ref[3:4, :])
    for n in range(RNN_BLOCKS):
        a, bt = _lru_coeffs(xc, wg_ref, ba_ref, bx_ref, lam_ref, n)
        sl = slice(n * RNN_BLOCK_W, (n + 1) * RNN_BLOCK_W)
        h = a * h0_ref[:, sl] + bt
        h_ref[:, sl] = h
        o_ref[:, sl] = (h * gate_ref[:, sl].astype(F32)).astype(o_ref.dtype)


def _lru_step(lx, conv_prev, h0, gate, conv_w, conv_b, wg, ba, bx, lam):
    bs = lx.shape[0]
    vm = pl.BlockSpec(memory_space=pltpu.VMEM)
    return pl.pallas_call(
        _lru_step_kernel,
        in_specs=[vm] * 12,
        out_specs=[vm, vm],
        out_shape=[jax.ShapeDtypeStruct((bs, RNN_WIDTH), BF16),
                   jax.ShapeDtypeStruct((bs, RNN_WIDTH), F32)],
        compiler_params=pltpu.CompilerParams(vmem_limit_bytes=32 * MIB),
        name="lru_step",
    )(lx, conv_prev[:, 0], conv_prev[:, 1], conv_prev[:, 2], h0, gate, conv_w,
      conv_b.reshape(1, -1), wg, ba.reshape(1, -1), bx.reshape(1, -1), lam.reshape(1, -1))


def _merge_kernel(ro_ref, do_ref, lo_ref, mg_ref, x_ref, p_ref, wr_ref, wd_ref, wl_ref, wo_ref,
                  wpg_ref, wp_ref, gn_ref, x_out_ref, hn_ref):
    d = D_MODEL
    merged = (mg_ref[:, 0:d].astype(F32) * jnp.dot(ro_ref[...], wr_ref[...], preferred_element_type=F32)
              + mg_ref[:, d:2 * d].astype(F32) * jnp.dot(do_ref[...], wd_ref[...], preferred_element_type=F32)
              + mg_ref[:, 2 * d:3 * d].astype(F32) * jnp.dot(lo_ref[...], wl_ref[...], preferred_element_type=F32))
    x1 = x_ref[...] + jnp.dot(merged.astype(BF16), wo_ref[...], preferred_element_type=F32)
    gate = jax.nn.sigmoid(jnp.dot(x1.astype(BF16), wpg_ref[...], preferred_element_type=F32))
    x2 = x1 + gate * jnp.dot(p_ref[...].astype(BF16), wp_ref[...], preferred_element_type=F32)
    x_out_ref[...] = x2
    y = x2 * lax.rsqrt(jnp.mean(x2 * x2, axis=-1, keepdims=True) + NORM_EPS)
    hn_ref[...] = (y * gn_ref[...]).astype(hn_ref.dtype)


def _merge(ro, do, lo, mg, x, p, wr, wd, wl, wo, wpg, wp, g_next, hn_dtype, tm):
    t = x.shape[0]
    d = D_MODEL
    row = lambda wdt: pl.BlockSpec((tm, wdt), lambda i: (i, 0))
    full = lambda a: pl.BlockSpec(a.shape, lambda i: (0, 0))
    return pl.pallas_call(
        _merge_kernel,
        grid=(t // tm,),
        in_specs=[row(RET_V_W), row(DIFF_OUT_W), row(RNN_WIDTH), row(3 * d), row(d), row(D_PLE),
                  full(wr), full(wd), full(wl), full(wo), full(wpg), full(wp),
                  pl.BlockSpec((1, d), lambda i: (0, 0))],
        out_specs=[row(d), row(d)],
        out_shape=[jax.ShapeDtypeStruct((t, d), F32), jax.ShapeDtypeStruct((t, d), hn_dtype)],
        compiler_params=_cparams(("parallel",), 56),
        name="merge",
    )(ro, do, lo, mg, x, p, wr, wd, wl, wo, wpg, wp, g_next.reshape(1, d))


def _lambda_init(layer):
    return 0.8 - 0.6 * math.exp(-0.3 * layer)


def _rope_tables(pos):
    half = RET_DK // 2
    inv = ROPE_BASE ** (-jnp.arange(half, dtype=F32) / half)
    ang = pos.astype(F32)[:, None] * inv[None, :]
    cos = jnp.cos(ang)
    sin = jnp.sin(ang)
    return jnp.concatenate([cos, cos], axis=-1), jnp.concatenate([-sin, sin], axis=-1)


def kernel(x_prompt, x_sample, cache_k, cache_v, state_ret, state_lru, state_conv, page_table, p_prompt, p_sample, rel_bias, norm_g, w_in, lambda_q1, lambda_k1, lambda_q2, lambda_k2, subln_g, conv_w, conv_b, gate_a_w, gate_a_b, gate_x_w, gate_x_b, lru_lambda, w_ret_out, w_diff_out, w_lru_out, w_o, w_ple, w_ple_gate, final_norm_g):
    bp, sp, d = x_prompt.shape
    bs, ss, _ = x_sample.shape
    assert d == D_MODEL and ss == 1
    depth = w_in.shape[0]
    n_pages, page = page_table.shape[1], cache_k.shape[2]
    past = n_pages * page
    tp = bp * sp
    blk = min(256, sp)
    tm_p = min(512, sp)
    tm_lru = min(256, sp)
    tm_merge = min(256, sp)

    cos_p, sin_p = _rope_tables(jnp.arange(sp, dtype=jnp.int32))
    cos_s, sin_s = _rope_tables(jnp.full((bs,), past, dtype=jnp.int32))
    bias_p = _bias_prompt(rel_bias, blk)
    bias_s, bias_s_new = _bias_sample(rel_bias, past)

    o = IN_OFFS
    xp = x_prompt.reshape(tp, d)
    xs = x_sample.reshape(bs, d)
    hn_p = _rmsnorm(xp, norm_g[0], BF16, tm_p)
    hn_s = _rmsnorm(xs, norm_g[0], BF16, bs)

    outs = {k: [] for k in ("kp", "vp", "ks", "vs", "rp", "rs", "lp", "ls", "cp", "cs")}
    for l in range(depth):
        w = w_in[l]
        w_rot = w[:, o[0]:o[2]].astype(BF16)
        w_plain = jnp.concatenate([w[:, o[2]:o[3]], w[:, o[4]:o[5]] * (DIFF_DH ** -0.5)], axis=1).astype(BF16)
        w_silu = jnp.concatenate([w[:, o[3]:o[4]], w[:, o[7]:o[8]], w[:, o[9]:o[10]]], axis=1).astype(BF16)
        w_kv = w[:, o[5]:o[7]].astype(BF16)
        w_lx = w[:, o[8]:o[9]].astype(BF16)
        w_mg = w[:, o[10]:o[11]].astype(BF16)
        wg = jnp.concatenate([gate_a_w[l], gate_x_w[l]], axis=-1).astype(BF16)
        wr, wd, wl = w_ret_out[l].astype(BF16), w_diff_out[l].astype(BF16), w_lru_out[l].astype(BF16)
        wo, wpg, wp = w_o[l].astype(BF16), w_ple_gate[l].astype(BF16), w_ple[l].astype(BF16)
        lam0 = _lambda_init(l)
        lam_rows = jnp.stack([lambda_q1[l], lambda_k1[l], lambda_q2[l], lambda_k2[l]], axis=0)
        last = l == depth - 1
        g_next = final_norm_g if last else norm_g[l + 1]
        hn_dtype = F32 if last else BF16

        qk = _proj(hn_p, w_rot, "rotary", tm_p, tables=(cos_p, sin_p))
        rvdq = _proj(hn_p, w_plain, "plain", tm_p)
        gates = _proj(hn_p, w_silu, "silu", tm_p)
        kf, vf, kb, vb = _proj(hn_p, w_kv, "kv", tm_p)
        lx = _proj(hn_p, w_lx, "plain", tm_p, out_dtype=F32)
        mg = _proj(hn_p, w_mg, "sigmoid", tm_p)
        ro, s_fin = _retention_prompt(qk, rvdq, gates, bp, sp)
        do = _attention_prompt(rvdq, kb, vb, bias_p, gates, subln_g[l], lam_rows, lam0, bp, sp, blk)
        lo, h_last, conv_new = _lru_prompt(lx, gates, conv_w[l], conv_b[l], wg, gate_a_b[l],
                                           gate_x_b[l], lru_lambda[l], bp, sp, tm_lru)
        xp, hn_p = _merge(ro, do, lo, mg, xp, p_prompt[l].reshape(tp, D_PLE), wr, wd, wl, wo, wpg, wp,
                          g_next, hn_dtype, tm_merge)
        outs["kp"].append(kf.reshape(bp, sp, DIFF_KV_HEADS, 2 * DIFF_DH))
        outs["vp"].append(vf.reshape(bp, sp, DIFF_KV_HEADS, DIFF_DV))
        outs["rp"].append(s_fin)
        outs["lp"].append(h_last.reshape(bp, RNN_WIDTH))
        outs["cp"].append(conv_new)

        qk = _proj(hn_s, w_rot, "rotary", bs, tables=(cos_s, sin_s))
        rvdq = _proj(hn_s, w_plain, "plain", bs)
        gates = _proj(hn_s, w_silu, "silu", bs)
        kf, vf, kb, vb = _proj(hn_s, w_kv, "kv", bs)
        lx = _proj(hn_s, w_lx, "plain", bs, out_dtype=F32)
        mg = _proj(hn_s, w_mg, "sigmoid", bs)
        ro, s_new = _retention_step(qk, rvdq[:, :RET_V_W], gates[:, :RET_V_W], state_ret, l)
        do = _attention_decode(rvdq[:, RET_V_W:], kb, vb, gates[:, RET_V_W:RET_V_W + DIFF_OUT_W],
                               cache_k, cache_v, page_table, bias_s, bias_s_new, subln_g[l],
                               lam_rows, lam0, l)
        conv_prev = state_conv[:, l]
        lo, h_new = _lru_step(lx, conv_prev, state_lru[:, l], gates[:, 2 * RET_V_W:], conv_w[l],
                              conv_b[l], wg, gate_a_b[l], gate_x_b[l], lru_lambda[l])
        xs, hn_s = _merge(ro, do, lo, mg, xs, p_sample[l].reshape(bs, D_PLE), wr, wd, wl, wo, wpg, wp,
                          g_next, hn_dtype, bs)
        outs["ks"].append(kf.reshape(bs, 1, DIFF_KV_HEADS, 2 * DIFF_DH))
        outs["vs"].append(vf.reshape(bs, 1, DIFF_KV_HEADS, DIFF_DV))
        outs["rs"].append(s_new)
        outs["ls"].append(h_new)
        outs["cs"].append(jnp.concatenate([conv_prev[:, 1:], lx[:, None, :]], axis=1))

    st = lambda key: jnp.stack(outs[key], axis=1)
    return (hn_p.reshape(bp, sp, d), hn_s.reshape(bs, 1, d), st("kp"), st("vp"), st("ks"), st("vs"),
            st("rp"), st("rs"), st("lp"), st("ls"), st("cp"), st("cs"))
```

```python
import functools
import math

import numpy as np
import jax
import jax.numpy as jnp
from jax import lax
from jax.experimental import pallas as pl
from jax.experimental.pallas import tpu as pltpu

F32 = jnp.float32
BF16 = jnp.bfloat16

D_MODEL = 1024
D_PLE = 256
NORM_EPS = 1e-6
RET_HEADS = 4
RET_DK = 128
RET_DV = 256
RET_CHUNK = 128
ROPE_BASE = 10000.0
DIFF_HEADS = 8
DIFF_KV_HEADS = 4
DIFF_GROUP = DIFF_HEADS // DIFF_KV_HEADS
DIFF_DH = 64
DIFF_DV = 2 * DIFF_DH
REL_BUCKETS = 32
REL_MAX_DIST = 128
RNN_WIDTH = 1024
RNN_BLOCKS = 8
RNN_BLOCK_W = RNN_WIDTH // RNN_BLOCKS
CONV_WIDTH = 4
LRU_C = 8.0

RET_QK_W = RET_HEADS * RET_DK
RET_V_W = RET_HEADS * RET_DV
DIFF_Q_W = DIFF_HEADS * 2 * DIFF_DH
DIFF_K_W = DIFF_KV_HEADS * 2 * DIFF_DH
DIFF_V_W = DIFF_KV_HEADS * DIFF_DV
DIFF_OUT_W = DIFF_HEADS * DIFF_DV
IN_SPLITS = (RET_QK_W, RET_QK_W, RET_V_W, RET_V_W,
             DIFF_Q_W, DIFF_K_W, DIFF_V_W, DIFF_OUT_W,
             RNN_WIDTH, RNN_WIDTH, 3 * D_MODEL)
IN_OFFS = tuple(int(v) for v in np.cumsum((0,) + IN_SPLITS))

LANES = 128
SUBLANES = 8
MASK_VALUE = -1e30
LOG2E = math.log2(math.e)
VT_ONES = 2 * SUBLANES
VT_ROWS = DIFF_DV + VT_ONES
FAR_GROUP = 4
MIB = 1024 * 1024


def _cparams(sem, vmem_mib):
    return pltpu.CompilerParams(dimension_semantics=sem, vmem_limit_bytes=int(vmem_mib * MIB))


def _rmsnorm_kernel(x_ref, g_ref, o_ref):
    x = x_ref[...]
    y = x * lax.rsqrt(jnp.mean(x * x, axis=-1, keepdims=True) + NORM_EPS)
    o_ref[...] = (y * g_ref[...]).astype(o_ref.dtype)


def _rmsnorm(x, g, out_dtype, tm):
    t, d = x.shape
    return pl.pallas_call(
        _rmsnorm_kernel,
        grid=(t // tm,),
        in_specs=[pl.BlockSpec((tm, d), lambda i: (i, 0)),
                  pl.BlockSpec((1, d), lambda i: (0, 0))],
        out_specs=pl.BlockSpec((tm, d), lambda i: (i, 0)),
        out_shape=jax.ShapeDtypeStruct((t, d), out_dtype),
        compiler_params=_cparams(("parallel",), 32),
        name="rmsnorm",
    )(x, g.reshape(1, d))


def _proj_qkv_t_kernel(x_ref, wq_ref, wkv_ref, kf_ref, vf_ref, kb_ref, qt_ref, vt_ref):
    x = x_ref[...]
    q = jnp.dot(x, wq_ref[...], preferred_element_type=F32)
    kv = jnp.dot(x, wkv_ref[...], preferred_element_type=F32)
    blk = qt_ref.shape[-1]
    k = kv[:, :DIFF_K_W]
    v = kv[:, DIFF_K_W:]
    kf_ref[...] = k
    vf_ref[...] = v
    kb_ref[...] = k.astype(BF16)
    q = q * (DIFF_DH ** -0.5 * LOG2E)
    for j in range(qt_ref.shape[0]):
        rows = slice(j * blk, (j + 1) * blk)
        qt_ref[j] = q[rows, :].T.astype(BF16)
        vt = v[rows, :].T.astype(BF16)
        for h in range(DIFF_KV_HEADS):
            vt_ref[j, h * VT_ROWS:h * VT_ROWS + DIFF_DV, :] = vt[h * DIFF_DV:(h + 1) * DIFF_DV]
            vt_ref[j, h * VT_ROWS + DIFF_DV:(h + 1) * VT_ROWS, :] = jnp.ones((VT_ONES, blk), BF16)


def _proj_kernel(x_ref, w_ref, *refs, kind):
    acc = jnp.dot(x_ref[...], w_ref[...], preferred_element_type=F32)
    if kind == "plain":
        (o_ref,) = refs
        o_ref[...] = acc.astype(o_ref.dtype)
    elif kind == "silu":
        (o_ref,) = refs
        o_ref[...] = (acc * jax.nn.sigmoid(acc)).astype(o_ref.dtype)
    elif kind == "sigmoid":
        (o_ref,) = refs
        o_ref[...] = jax.nn.sigmoid(acc).astype(o_ref.dtype)
    elif kind == "rotary":
        cos_ref, sin_ref, o_ref = refs
        cos = cos_ref[...]
        sin = sin_ref[...]
        for h in range(2 * RET_HEADS):
            z = acc[:, h * RET_DK:(h + 1) * RET_DK]
            r = z * cos + pltpu.roll(z, RET_DK // 2, 1) * sin
            if h >= RET_HEADS:
                r = r * (RET_DK ** -0.5)
            o_ref[:, h * RET_DK:(h + 1) * RET_DK] = r.astype(o_ref.dtype)
    elif kind == "kv":
        kf_ref, vf_ref, kb_ref, vb_ref = refs
        k = acc[:, :DIFF_K_W]
        v = acc[:, DIFF_K_W:]
        kf_ref[...] = k
        vf_ref[...] = v
        kb_ref[...] = k.astype(BF16)
        vb_ref[...] = v.astype(BF16)
    else:
        raise ValueError(kind)


PROJ_TN = 1024
COL_ROT = (0,)
COL_RV = (1,)
COL_RV_DQ = (1, 3)
COL_SILU = (2, 5, 7)
COL_DQ = 3
COL_KV = 4
COL_LX = (6,)
COL_MG = (8, 9, 10)
assert IN_OFFS[2] == PROJ_TN and IN_OFFS[5] == COL_KV * PROJ_TN and IN_OFFS[10] == COL_MG[0] * PROJ_TN


def _col_index(cols, j):
    if len(cols) == 1:
        return cols[0]
    step = cols[1] - cols[0]
    idx = cols[0] + step * j
    if cols[-1] != cols[0] + step * (len(cols) - 1):
        idx = idx - j // (len(cols) - 1)
    return idx


def _proj_qkv_t(hn, w_all, layer, tm, blk):
    t, d = hn.shape
    nj = tm // blk
    half = pl.BlockSpec((tm, DIFF_K_W), lambda i: (i, 0))
    return pl.pallas_call(
        _proj_qkv_t_kernel,
        grid=(t // tm,),
        in_specs=[pl.BlockSpec((tm, d), lambda i: (i, 0)),
                  pl.BlockSpec((None, d, PROJ_TN), lambda i: (layer, 0, COL_DQ)),
                  pl.BlockSpec((None, d, PROJ_TN), lambda i: (layer, 0, COL_KV))],
        out_specs=[half, half, half,
                   pl.BlockSpec((nj, DIFF_Q_W, blk), lambda i: (i, 0, 0)),
                   pl.BlockSpec((nj, DIFF_KV_HEADS * VT_ROWS, blk), lambda i: (i, 0, 0))],
        out_shape=[jax.ShapeDtypeStruct((t, DIFF_K_W), F32),
                   jax.ShapeDtypeStruct((t, DIFF_V_W), F32),
                   jax.ShapeDtypeStruct((t, DIFF_K_W), BF16),
                   jax.ShapeDtypeStruct((t // blk, DIFF_Q_W, blk), BF16),
                   jax.ShapeDtypeStruct((t // blk, DIFF_KV_HEADS * VT_ROWS, blk), BF16)],
        compiler_params=_cparams(("parallel",), 48),
        name="proj_qkv_t",
    )(hn, w_all, w_all)


def _proj(hn, w_all, layer, cols, kind, tm, out_dtype=BF16, tables=None):
    t, d = hn.shape
    tn = PROJ_TN
    n = tn * len(cols)
    grid = (t // tm, len(cols))
    in_specs = [pl.BlockSpec((tm, d), lambda i, j: (i, 0)),
                pl.BlockSpec((None, d, tn), lambda i, j: (layer, 0, _col_index(cols, j)))]
    args = [hn, w_all]
    if kind == "rotary":
        cos, sin = tables
        nb = cos.shape[0] // tm
        spec = pl.BlockSpec((tm, RET_DK), lambda i, j: (i % nb, 0))
        in_specs += [spec, spec]
        args += [cos, sin]
    if kind == "kv":
        half = pl.BlockSpec((tm, DIFF_K_W), lambda i, j: (i, 0))
        out_specs = [half, half, half, half]
        out_shape = [jax.ShapeDtypeStruct((t, DIFF_K_W), F32),
                     jax.ShapeDtypeStruct((t, DIFF_V_W), F32),
                     jax.ShapeDtypeStruct((t, DIFF_K_W), BF16),
                     jax.ShapeDtypeStruct((t, DIFF_V_W), BF16)]
    else:
        out_specs = pl.BlockSpec((tm, tn), lambda i, j: (i, j))
        out_shape = jax.ShapeDtypeStruct((t, n), out_dtype)
    return pl.pallas_call(
        functools.partial(_proj_kernel, kind=kind),
        grid=grid, in_specs=in_specs, out_specs=out_specs, out_shape=out_shape,
        compiler_params=_cparams(("parallel", "arbitrary"), 40),
        name="proj_" + kind,
    )(*args)


def _retention_kernel(qk_ref, v_ref, gate_ref, dmask_ref, qdec_ref, kdec_ref,
                      o_ref, sfin_ref, state_ref, *, chunk_dec):
    c = pl.program_id(1)

    @pl.when(c == 0)
    def _():
        state_ref[...] = jnp.zeros_like(state_ref)

    for h in range(RET_HEADS):
        q = qk_ref[:, h * RET_DK:(h + 1) * RET_DK]
        k = qk_ref[:, RET_QK_W + h * RET_DK:RET_QK_W + (h + 1) * RET_DK]
        v = v_ref[:, h * RET_DV:(h + 1) * RET_DV]
        st = state_ref[h]
        s = lax.dot_general(q, k, (((1,), (1,)), ((), ())), preferred_element_type=F32)
        s = s * dmask_ref[h]
        o = jnp.dot(s.astype(BF16), v, preferred_element_type=F32)
        o = o + jnp.dot(q, st.astype(BF16), preferred_element_type=F32) * qdec_ref[:, h:h + 1]
        kd = (k.astype(F32) * kdec_ref[:, h:h + 1]).astype(BF16)
        state_ref[h] = st * chunk_dec[h] + lax.dot_general(
            kd, v, (((0,), (0,)), ((), ())), preferred_element_type=F32)
        o = o * lax.rsqrt(jnp.mean(o * o, axis=-1, keepdims=True) + NORM_EPS)
        g = gate_ref[:, h * RET_DV:(h + 1) * RET_DV].astype(F32)
        o_ref[:, h * RET_DV:(h + 1) * RET_DV] = (o * g).astype(o_ref.dtype)

    @pl.when(c == pl.num_programs(1) - 1)
    def _():
        sfin_ref[...] = state_ref[...]


def _retention_prompt(qk, rvdq, gates, b, s):
    c = math.gcd(s, RET_CHUNK)
    nc = s // c
    log_g = np.log1p(-(2.0 ** (-5.0 - np.arange(RET_HEADS, dtype=np.float64))))
    idx = np.arange(c, dtype=np.float64)
    dist = idx[:, None] - idx[None, :]
    dmask = np.where(dist[None] >= 0, np.exp(np.maximum(dist, 0.0)[None] * log_g[:, None, None]), 0.0)
    qdec = np.exp((idx + 1.0)[:, None] * log_g[None, :])
    kdec = np.exp((c - 1.0 - idx)[:, None] * log_g[None, :])
    chunk_dec = tuple(float(x) for x in np.exp(c * log_g).astype(np.float32))
    t = b * s
    return pl.pallas_call(
        functools.partial(_retention_kernel, chunk_dec=chunk_dec),
        grid=(b, nc),
        in_specs=[pl.BlockSpec((c, 2 * RET_QK_W), lambda i, j: (i * nc + j, 0)),
                  pl.BlockSpec((c, RET_V_W), lambda i, j: (i * nc + j, 0)),
                  pl.BlockSpec((c, RET_V_W), lambda i, j: (i * nc + j, 0)),
                  pl.BlockSpec((RET_HEADS, c, c), lambda i, j: (0, 0, 0)),
                  pl.BlockSpec((c, RET_HEADS), lambda i, j: (0, 0)),
                  pl.BlockSpec((c, RET_HEADS), lambda i, j: (0, 0))],
        out_specs=[pl.BlockSpec((c, RET_V_W), lambda i, j: (i * nc + j, 0)),
                   pl.BlockSpec((None, RET_HEADS, RET_DK, RET_DV), lambda i, j: (i, 0, 0, 0))],
        out_shape=[jax.ShapeDtypeStruct((t, RET_V_W), BF16),
                   jax.ShapeDtypeStruct((b, RET_HEADS, RET_DK, RET_DV), F32)],
        scratch_shapes=[pltpu.VMEM((RET_HEADS, RET_DK, RET_DV), F32)],
        compiler_params=_cparams(("parallel", "arbitrary"), 32),
        name="retention_prompt",
    )(qk, rvdq, gates, jnp.asarray(dmask, F32), jnp.asarray(qdec, F32), jnp.asarray(kdec, F32))


def _retention_step_kernel(qkc_ref, v_ref, gate_ref, s_ref, o_ref, snew_ref, *, decay):
    for h in range(RET_HEADS):
        qc = qkc_ref[h]
        kc = qkc_ref[RET_HEADS + h]
        v = v_ref[:, h * RET_DV:(h + 1) * RET_DV]
        st = s_ref[h]
        qk = jnp.sum(qc * kc, axis=0, keepdims=True)
        o = qk * v + decay[h] * jnp.sum(qc * st, axis=0, keepdims=True)
        snew_ref[h] = st * decay[h] + kc * v
        o = o * lax.rsqrt(jnp.mean(o * o, axis=-1, keepdims=True) + NORM_EPS)
        o_ref[:, h * RET_DV:(h + 1) * RET_DV] = o * gate_ref[:, h * RET_DV:(h + 1) * RET_DV]


def _retention_step(qk, v, gate, state_ret, layer):
    bs = qk.shape[0]
    log_g = np.log1p(-(2.0 ** (-5.0 - np.arange(RET_HEADS, dtype=np.float64))))
    decay = tuple(float(x) for x in np.exp(log_g).astype(np.float32))
    qkc = qk.astype(F32).reshape(bs, 2 * RET_HEADS, RET_DK, 1)
    v3 = v.astype(F32).reshape(bs, 1, RET_V_W)
    g3 = gate.astype(F32).reshape(bs, 1, RET_V_W)
    o, snew = pl.pallas_call(
        functools.partial(_retention_step_kernel, decay=decay),
        grid=(bs,),
        in_specs=[pl.BlockSpec((None, 2 * RET_HEADS, RET_DK, 1), lambda i: (i, 0, 0, 0)),
                  pl.BlockSpec((None, 1, RET_V_W), lambda i: (i, 0, 0)),
                  pl.BlockSpec((None, 1, RET_V_W), lambda i: (i, 0, 0)),
                  pl.BlockSpec((None, None, RET_HEADS, RET_DK, RET_DV),
                               lambda i: (i, layer, 0, 0, 0))],
        out_specs=[pl.BlockSpec((None, 1, RET_V_W), lambda i: (i, 0, 0)),
                   pl.BlockSpec((None, RET_HEADS, RET_DK, RET_DV), lambda i: (i, 0, 0, 0))],
        out_shape=[jax.ShapeDtypeStruct((bs, 1, RET_V_W), F32),
                   jax.ShapeDtypeStruct((bs, RET_HEADS, RET_DK, RET_DV), F32)],
        compiler_params=_cparams(("parallel",), 32),
        name="retention_step",
    )(qkc, v3, g3, state_ret)
    return o.reshape(bs, RET_V_W).astype(BF16), snew


def _t5_bucket(rel):
    n = jnp.maximum(rel, 0)
    max_exact = REL_BUCKETS // 2
    nf = jnp.maximum(n.astype(F32), 1.0)
    large = max_exact + (jnp.log(nf / max_exact) / math.log(REL_MAX_DIST / max_exact)
                         * (REL_BUCKETS - max_exact)).astype(jnp.int32)
    large = jnp.minimum(large, REL_BUCKETS - 1)
    return jnp.where(n < max_exact, n, large)


def _bias_lookup(tab_ref, bucket, head):
    out = jnp.zeros(bucket.shape, F32)
    far = tab_ref[REL_BUCKETS - 1, head]
    for bkt in range(REL_BUCKETS - 1):
        out = jnp.where(bucket == bkt, tab_ref[bkt, head] - far, out)
    return out


def _bias_prompt_kernel(tab_ref, o_ref, *, blk):
    i = lax.broadcasted_iota(jnp.int32, (blk, blk), 1)
    j = lax.broadcasted_iota(jnp.int32, (blk, blk), 0)
    for which in range(2):
        rel = i - j + which * blk
        bucket = _t5_bucket(rel)
        for h in range(DIFF_HEADS):
            o_ref[which, h] = jnp.where(rel >= 0, _bias_lookup(tab_ref, bucket, h) * LOG2E, MASK_VALUE)


def _bias_prompt(rel_bias, blk):
    return pl.pallas_call(
        functools.partial(_bias_prompt_kernel, blk=blk),
        in_specs=[pl.BlockSpec(memory_space=pltpu.SMEM)],
        out_specs=pl.BlockSpec(memory_space=pltpu.VMEM),
        out_shape=jax.ShapeDtypeStruct((2, DIFF_HEADS, blk, blk), F32),
        compiler_params=pltpu.CompilerParams(vmem_limit_bytes=32 * MIB),
        name="bias_prompt",
    )(rel_bias)


def _bias_sample_kernel(tab_ref, o_ref, onew_ref, *, past, width):
    rows = 2 * DIFF_HEADS
    r = lax.broadcasted_iota(jnp.int32, (rows, width), 0)
    lane = lax.broadcasted_iota(jnp.int32, (rows, width), 1)
    head = r % DIFF_HEADS
    kvh = lane % DIFF_KV_HEADS
    pos = lane // DIFF_KV_HEADS
    bucket = _t5_bucket(past - pos)
    out = jnp.zeros((rows, width), F32)
    for h in range(DIFF_HEADS):
        out = jnp.where(head == h, _bias_lookup(tab_ref, bucket, h), out)
    o_ref[...] = jnp.where(kvh == head // DIFF_GROUP, out, MASK_VALUE)
    rn = lax.broadcasted_iota(jnp.int32, (rows, LANES), 0) % DIFF_HEADS
    new = jnp.zeros((rows, LANES), F32)
    zero_bucket = _t5_bucket(jnp.zeros((rows, LANES), jnp.int32))
    for h in range(DIFF_HEADS):
        new = jnp.where(rn == h, _bias_lookup(tab_ref, zero_bucket, h), new)
    onew_ref[...] = new


def _bias_sample(rel_bias, past):
    width = past * DIFF_KV_HEADS
    return pl.pallas_call(
        functools.partial(_bias_sample_kernel, past=past, width=width),
        in_specs=[pl.BlockSpec(memory_space=pltpu.SMEM)],
        out_specs=[pl.BlockSpec(memory_space=pltpu.VMEM), pl.BlockSpec(memory_space=pltpu.VMEM)],
        out_shape=[jax.ShapeDtypeStruct((2 * DIFF_HEADS, width), F32),
                   jax.ShapeDtypeStruct((2 * DIFF_HEADS, LANES), F32)],
        compiler_params=pltpu.CompilerParams(vmem_limit_bytes=32 * MIB),
        name="bias_sample",
    )(rel_bias)


def _lambda_full(lam_ref, lam0):
    a = jnp.sum(lam_ref[0:1, :] * lam_ref[1:2, :], axis=-1, keepdims=True)
    b = jnp.sum(lam_ref[2:3, :] * lam_ref[3:4, :], axis=-1, keepdims=True)
    return jnp.exp(a) - jnp.exp(b) + lam0


def _attn_kernel(qt_in_ref, k_ref, vt_ref, bias_ref, gate_ref, sg_ref, lam_ref, o_ref,
                 qt_ref, m_ref, acc_ref, *, blk, lam0):
    qi = pl.program_id(2)
    n_chunks = 2 * DIFF_GROUP
    row = lax.broadcasted_iota(jnp.int32, (DIFF_DV, blk), 0)
    for g in range(DIFF_GROUP):
        qg = qt_in_ref[g * DIFF_DV:(g + 1) * DIFF_DV, :]
        for m in range(2):
            keep = (row < DIFF_DH) if m == 0 else (row >= DIFF_DH)
            c = m * DIFF_GROUP + g
            qt_ref[:, c * blk:(c + 1) * blk] = jnp.where(keep, qg, jnp.zeros_like(qg))
    m_ref[...] = jnp.full(m_ref.shape, MASK_VALUE, F32)
    acc_ref[...] = jnp.zeros(acc_ref.shape, F32)

    def run_blocks(blocks):
        s_lag = 2
        kbs = [k_ref[pl.ds(pl.multiple_of(ki * blk, blk), blk), :] for ki, _ in blocks]
        vts = [vt_ref[ki] for ki, _ in blocks]
        units = [(bi, c) for bi in range(len(blocks)) for c in range(n_chunks)]
        n_units = len(units)

        def scores(u):
            bi, c = units[u]
            s = jnp.dot(kbs[bi], qt_ref[:, c * blk:(c + 1) * blk], preferred_element_type=F32)
            if blocks[bi][1] is not None:
                s = s + bias_ref[blocks[bi][1], c % DIFF_GROUP]
            return s

        def softmax(u, s):
            c = units[u][1]
            cs = slice(c * blk, (c + 1) * blk)
            m_prev = m_ref[:, cs]
            m_new = jnp.maximum(m_prev, jnp.max(s, axis=0, keepdims=True))
            p = jnp.exp2(s - m_new)
            alpha = jnp.exp2(m_prev - m_new)
            m_ref[:, cs] = m_new
            return p.astype(BF16), alpha

        def accumulate(u, p, alpha):
            bi, c = units[u]
            cs = slice(c * blk, (c + 1) * blk)
            acc_ref[:, cs] = acc_ref[:, cs] * alpha + jnp.dot(vts[bi], p, preferred_element_type=F32)

        s_vals, p_vals = {}, {}
        for t in range(n_units + s_lag + 1):
            if t < n_units:
                s_vals[t] = scores(t)
            if 0 <= t - s_lag < n_units:
                p_vals[t - s_lag] = softmax(t - s_lag, s_vals.pop(t - s_lag))
            if 0 <= t - s_lag - 1 < n_units:
                accumulate(t - s_lag - 1, *p_vals.pop(t - s_lag - 1))

    n_far = jnp.maximum(qi - 1, 0)
    n_main = n_far // FAR_GROUP
    n_left = n_far - n_main * FAR_GROUP

    def far_body(i, carry):
        run_blocks([(FAR_GROUP * i + j, None) for j in range(FAR_GROUP)])
        return carry

    lax.fori_loop(0, n_main, far_body, 0)

    for left in range(FAR_GROUP):
        @pl.when(jnp.logical_and(qi >= 1, n_left == left))
        def _():
            base = n_main * FAR_GROUP
            run_blocks([(base + j, None) for j in range(left)] + [(qi - 1, 1), (qi, 0)])

    @pl.when(qi == 0)
    def _():
        run_blocks([(qi, 0)])

    lam = _lambda_full(lam_ref, lam0)
    a = acc_ref[0:DIFF_DV, :] / acc_ref[DIFF_DV:DIFF_DV + 1, :]
    for g in range(DIFF_GROUP):
        ot = a[:, g * blk:(g + 1) * blk] - lam * a[:, (DIFF_GROUP + g) * blk:(DIFF_GROUP + g + 1) * blk]
        ot = ot * lax.rsqrt(jnp.mean(ot * ot, axis=0, keepdims=True) + NORM_EPS)
        ot = ot * sg_ref[...] * (1.0 - lam0)
        gate = gate_ref[:, g * DIFF_DV:(g + 1) * DIFF_DV].astype(F32)
        o_ref[:, g * DIFF_DV:(g + 1) * DIFF_DV] = (ot.T * gate).astype(o_ref.dtype)


def _attention_prompt(qt, kb, vt, bias, gates, subln_g, lam_rows, lam0, b, s, blk):
    nq = s // blk
    t = b * s
    gcol0 = RET_V_W // (DIFF_GROUP * DIFF_DV)
    kvw = 2 * DIFF_DH
    return pl.pallas_call(
        functools.partial(_attn_kernel, blk=blk, lam0=lam0),
        grid=(b, DIFF_KV_HEADS, nq),
        in_specs=[pl.BlockSpec((None, DIFF_GROUP * DIFF_DV, blk), lambda i, h, q: (i * nq + q, h, 0)),
                  pl.BlockSpec((s, kvw), lambda i, h, q: (i, h)),
                  pl.BlockSpec((nq, VT_ROWS, blk), lambda i, h, q: (i, h, 0)),
                  pl.BlockSpec((2, DIFF_GROUP, blk, blk), lambda i, h, q: (0, h, 0, 0)),
                  pl.BlockSpec((blk, DIFF_GROUP * DIFF_DV), lambda i, h, q: (i * nq + q, gcol0 + h)),
                  pl.BlockSpec((DIFF_DV, 1), lambda i, h, q: (0, 0)),
                  pl.BlockSpec((4, DIFF_DH), lambda i, h, q: (0, 0))],
        out_specs=pl.BlockSpec((blk, DIFF_GROUP * DIFF_DV), lambda i, h, q: (i * nq + q, h)),
        out_shape=jax.ShapeDtypeStruct((t, DIFF_OUT_W), BF16),
        scratch_shapes=[pltpu.VMEM((DIFF_DV, 4 * blk), BF16),
                        pltpu.VMEM((1, 4 * blk), F32),
                        pltpu.VMEM((VT_ROWS, 4 * blk), F32)],
        compiler_params=_cparams(("parallel", "parallel", "arbitrary"), 48),
        name="attention_prompt",
    )(qt, kb, vt, bias, gates, subln_g.reshape(DIFF_DV, 1), lam_rows)


def _attn_decode_kernel(pt_ref, *refs, n_pages, rows_per_page, lam0):
    del pt_ref
    k_refs = refs[:n_pages]
    v_refs = refs[n_pages:2 * n_pages]
    (qt_ref, kn_ref, vn_ref, bias_ref, bnew_ref, gate_ref, sg_ref, lam_ref,
     o_ref, s_ref) = refs[2 * n_pages:]
    qt = qt_ref[...]
    w = rows_per_page
    for p in range(n_pages):
        kp = k_refs[p][...].astype(BF16)
        sp = lax.dot_general(qt, kp, (((1,), (1,)), ((), ())), preferred_element_type=F32)
        s_ref[:, p * w:(p + 1) * w] = sp + bias_ref[:, p * w:(p + 1) * w]
    s = s_ref[...]
    s_new = jnp.sum(qt.astype(F32) * kn_ref[...].astype(F32), axis=1, keepdims=True)
    s_new = s_new + bnew_ref[:, 0:1]
    m = jnp.maximum(jnp.max(s, axis=1, keepdims=True), s_new)
    e_new = jnp.exp(s_new - m)
    pr = jnp.exp(s - m)
    l = jnp.sum(pr, axis=1, keepdims=True) + e_new
    prb = pr.astype(BF16)
    acc = e_new * vn_ref[...].astype(F32)
    for p in range(n_pages):
        vp = v_refs[p][...].astype(BF16)
        acc = acc + jnp.dot(prb[:, p * w:(p + 1) * w], vp, preferred_element_type=F32)
    a = acc / l
    lam = _lambda_full(lam_ref, lam0)
    o = a[:DIFF_HEADS] - lam * a[DIFF_HEADS:]
    o = o * lax.rsqrt(jnp.mean(o * o, axis=-1, keepdims=True) + NORM_EPS)
    o = o * sg_ref[...] * (1.0 - lam0)
    o_ref[...] = o * gate_ref[...]


def _attention_decode(dq, kb_new, vb_new, gate, cache_k, cache_v, page_table, bias, bias_new,
                      subln_g, lam_rows, lam0, layer):
    bs = dq.shape[0]
    n_pool, depth, page = cache_k.shape[0], cache_k.shape[1], cache_k.shape[2]
    n_pages = page_table.shape[1]
    w = page * DIFF_KV_HEADS
    ck = cache_k.reshape(n_pool, depth, w, 2 * DIFF_DH)
    cv = cache_v.reshape(n_pool, depth, w, DIFF_DV)
    q = (dq * (DIFF_DH ** -0.5)).reshape(bs, DIFF_KV_HEADS, DIFF_GROUP, 2, DIFF_DH)
    q = jnp.transpose(q, (0, 3, 1, 2, 4)).reshape(bs, 2, DIFF_HEADS, DIFF_DH)
    z = jnp.zeros_like(q[:, 0])
    qt = jnp.concatenate([jnp.concatenate([q[:, 0], z], axis=-1),
                          jnp.concatenate([z, q[:, 1]], axis=-1)], axis=1)
    kn = jnp.tile(jnp.repeat(kb_new.reshape(bs, DIFF_KV_HEADS, 2 * DIFF_DH), DIFF_GROUP, axis=1), (1, 2, 1))
    vn = jnp.tile(jnp.repeat(vb_new.reshape(bs, DIFF_KV_HEADS, DIFF_DV), DIFF_GROUP, axis=1), (1, 2, 1))
    g3 = gate.astype(F32).reshape(bs, DIFF_HEADS, DIFF_DV)
    rows = 2 * DIFF_HEADS

    def page_spec(p):
        return pl.BlockSpec((None, None, w, 2 * DIFF_DH), lambda i, pt: (pt[i, p], layer, 0, 0))

    row_spec = pl.BlockSpec((None, rows, LANES), lambda i, pt: (i, 0, 0))
    in_specs = ([page_spec(p) for p in range(n_pages)] + [page_spec(p) for p in range(n_pages)]
                + [row_spec, row_spec, row_spec,
                   pl.BlockSpec((rows, n_pages * w), lambda i, pt: (0, 0)),
                   pl.BlockSpec((rows, LANES), lambda i, pt: (0, 0)),
                   pl.BlockSpec((None, DIFF_HEADS, DIFF_DV), lambda i, pt: (i, 0, 0)),
                   pl.BlockSpec((1, DIFF_DV), lambda i, pt: (0, 0)),
                   pl.BlockSpec((4, DIFF_DH), lambda i, pt: (0, 0))])
    out = pl.pallas_call(
        functools.partial(_attn_decode_kernel, n_pages=n_pages, rows_per_page=w, lam0=lam0),
        grid_spec=pltpu.PrefetchScalarGridSpec(
            num_scalar_prefetch=1, grid=(bs,), in_specs=in_specs,
            out_specs=pl.BlockSpec((None, DIFF_HEADS, DIFF_DV), lambda i, pt: (i, 0, 0)),
            scratch_shapes=[pltpu.VMEM((rows, n_pages * w), F32)]),
        out_shape=jax.ShapeDtypeStruct((bs, DIFF_HEADS, DIFF_DV), F32),
        compiler_params=_cparams(("arbitrary",), 48),
        name="attention_decode",
    )(page_table, *([ck] * n_pages), *([cv] * n_pages), qt, kn, vn, bias, bias_new, g3,
      subln_g.reshape(1, DIFF_DV), lam_rows)
    return out.reshape(bs, DIFF_OUT_W).astype(BF16)


def _softplus(x):
    return jnp.maximum(x, 0.0) + jnp.log1p(jnp.exp(-jnp.abs(x)))


def _lru_coeffs(xc, wg_ref, ba_ref, bx_ref, lam_ref, n):
    sl = slice(n * RNN_BLOCK_W, (n + 1) * RNN_BLOCK_W)
    xn = xc[:, sl]
    z = jnp.dot(xn.astype(BF16), wg_ref[n], preferred_element_type=F32)
    rg = jax.nn.sigmoid(z[:, :RNN_BLOCK_W] + ba_ref[:, sl])
    ig = jax.nn.sigmoid(z[:, RNN_BLOCK_W:] + bx_ref[:, sl])
    log_a = -LRU_C * rg * _softplus(-lam_ref[:, sl])
    a = jnp.exp(log_a)
    bt = jnp.sqrt(-jnp.tanh(log_a) * (1.0 + a * a)) * (ig * xn)
    return a, bt


def _lru_prompt_kernel(x_ref, gate_ref, cw_ref, cb_ref, wg_ref, ba_ref, bx_ref, lam_ref,
                       o_ref, hlast_ref, conv_ref, xbuf_ref, a_ref, b_ref, h_ref, *, tm):
    t = pl.program_id(1)
    pad = SUBLANES

    @pl.when(t == 0)
    def _():
        xbuf_ref[0:pad, :] = jnp.zeros((pad, RNN_WIDTH), F32)
        h_ref[...] = jnp.zeros(h_ref.shape, F32)

    xbuf_ref[pad:pad + tm, :] = x_ref[...]
    xc = cb_ref[...]
    for j in range(CONV_WIDTH):
        off = pad - (CONV_WIDTH - 1) + j
        xc = xc + xbuf_ref[off:off + tm, :] * cw_ref[j:j + 1, :]
    conv_ref[...] = xbuf_ref[pad + tm - (CONV_WIDTH - 1):pad + tm, :]
    xbuf_ref[0:pad, :] = xbuf_ref[tm:tm + pad, :]

    for n in range(RNN_BLOCKS):
        a, bt = _lru_coeffs(xc, wg_ref, ba_ref, bx_ref, lam_ref, n)
        sl = slice(n * RNN_BLOCK_W, (n + 1) * RNN_BLOCK_W)
        r8 = lax.broadcasted_iota(jnp.int32, a.shape, 0) % SUBLANES
        for d in (1, 2, 4):
            a_sh = pltpu.roll(a, d, 0)
            b_sh = pltpu.roll(bt, d, 0)
            ok = r8 >= d
            bt = jnp.where(ok, a * b_sh + bt, bt)
            a = jnp.where(ok, a * a_sh, a)
        a_ref[:, sl] = a
        b_ref[:, sl] = bt

    h = h_ref[...]
    for gi in range(tm // SUBLANES):
        rs = slice(gi * SUBLANES, (gi + 1) * SUBLANES)
        hg = a_ref[rs, :] * h + b_ref[rs, :]
        b_ref[rs, :] = hg
        h = hg[SUBLANES - 1:SUBLANES, :]
    h_ref[...] = h
    hlast_ref[...] = h
    o_ref[...] = (b_ref[...] * gate_ref[...].astype(F32)).astype(o_ref.dtype)


def _lru_prompt(lx, gates, conv_w, conv_b, wg, ba, bx, lam, b, s, tm):
    nt = s // tm
    t = b * s
    gcol = 2
    vec = pl.BlockSpec((1, RNN_WIDTH), lambda i, j: (0, 0))
    return pl.pallas_call(
        functools.partial(_lru_prompt_kernel, tm=tm),
        grid=(b, nt),
        in_specs=[pl.BlockSpec((tm, RNN_WIDTH), lambda i, j: (i * nt + j, 0)),
                  pl.BlockSpec((tm, RNN_WIDTH), lambda i, j: (i * nt + j, gcol)),
                  pl.BlockSpec((CONV_WIDTH, RNN_WIDTH), lambda i, j: (0, 0)),
                  vec,
                  pl.BlockSpec((RNN_BLOCKS, RNN_BLOCK_W, 2 * RNN_BLOCK_W), lambda i, j: (0, 0, 0)),
                  vec, vec, vec],
        out_specs=[pl.BlockSpec((tm, RNN_WIDTH), lambda i, j: (i * nt + j, 0)),
                   pl.BlockSpec((None, 1, RNN_WIDTH), lambda i, j: (i, 0, 0)),
                   pl.BlockSpec((None, CONV_WIDTH - 1, RNN_WIDTH), lambda i, j: (i, 0, 0))],
        out_shape=[jax.ShapeDtypeStruct((t, RNN_WIDTH), BF16),
                   jax.ShapeDtypeStruct((b, 1, RNN_WIDTH), F32),
                   jax.ShapeDtypeStruct((b, CONV_WIDTH - 1, RNN_WIDTH), F32)],
        scratch_shapes=[pltpu.VMEM((tm + SUBLANES, RNN_WIDTH), F32),
                        pltpu.VMEM((tm, RNN_WIDTH), F32),
                        pltpu.VMEM((tm, RNN_WIDTH), F32),
                        pltpu.VMEM((1, RNN_WIDTH), F32)],
        compiler_params=_cparams(("parallel", "arbitrary"), 32),
        name="lru_prompt",
    )(lx, gates, conv_w, conv_b.reshape(1, -1), wg, ba.reshape(1, -1), bx.reshape(1, -1),
      lam.reshape(1, -1))


def _lru_step_kernel(x_ref, c0_ref, c1_ref, c2_ref, h0_ref, gate_ref, cw_ref, cb_ref, wg_ref,
                     ba_ref, bx_ref, lam_ref, o_ref, h_ref):
    xc = (cb_ref[...] + c0_ref[...] * cw_ref[0:1, :] + c1_ref[...] * cw_ref[1:2, :]
          + c2_ref[...] * cw_ref[2:3, :] + x_ref[...] * cw_ref[3:4, :])
    for n in range(RNN_BLOCKS):
        a, bt = _lru_coeffs(xc, wg_ref, ba_ref, bx_ref, lam_ref, n)
        sl = slice(n * RNN_BLOCK_W, (n + 1) * RNN_BLOCK_W)
        h = a * h0_ref[:, sl] + bt
        h_ref[:, sl] = h
        o_ref[:, sl] = (h * gate_ref[:, sl].astype(F32)).astype(o_ref.dtype)


def _lru_step(lx, conv_prev, h0, gate, conv_w, conv_b, wg, ba, bx, lam):
    bs = lx.shape[0]
    vm = pl.BlockSpec(memory_space=pltpu.VMEM)
    return pl.pallas_call(
        _lru_step_kernel,
        in_specs=[vm] * 12,
        out_specs=[vm, vm],
        out_shape=[jax.ShapeDtypeStruct((bs, RNN_WIDTH), BF16),
                   jax.ShapeDtypeStruct((bs, RNN_WIDTH), F32)],
        compiler_params=pltpu.CompilerParams(vmem_limit_bytes=32 * MIB),
        name="lru_step",
    )(lx, conv_prev[:, 0], conv_prev[:, 1], conv_prev[:, 2], h0, gate, conv_w,
      conv_b.reshape(1, -1), wg, ba.reshape(1, -1), bx.reshape(1, -1), lam.reshape(1, -1))


def _merge_kernel(ro_ref, do_ref, lo_ref, mg_ref, x_ref, p_ref, wr_ref, wd_ref, wl_ref, wo_ref,
                  wpg_ref, wp_ref, gn_ref, x_out_ref, hn_ref):
    d = D_MODEL
    merged = (mg_ref[:, 0:d].astype(F32) * jnp.dot(ro_ref[...], wr_ref[...], preferred_element_type=F32)
              + mg_ref[:, d:2 * d].astype(F32) * jnp.dot(do_ref[...], wd_ref[...], preferred_element_type=F32)
              + mg_ref[:, 2 * d:3 * d].astype(F32) * jnp.dot(lo_ref[...], wl_ref[...], preferred_element_type=F32))
    x1 = x_ref[...] + jnp.dot(merged.astype(BF16), wo_ref[...], preferred_element_type=F32)
    gate = jax.nn.sigmoid(jnp.dot(x1.astype(BF16), wpg_ref[...], preferred_element_type=F32))
    x2 = x1 + gate * jnp.dot(p_ref[...].astype(BF16), wp_ref[...], preferred_element_type=F32)
    x_out_ref[...] = x2
    y = x2 * lax.rsqrt(jnp.mean(x2 * x2, axis=-1, keepdims=True) + NORM_EPS)
    hn_ref[...] = (y * gn_ref[...]).astype(hn_ref.dtype)


def _merge(ro, do, lo, mg, x, p, wr, wd, wl, wo, wpg, wp, g_next, hn_dtype, tm):
    t = x.shape[0]
    d = D_MODEL
    row = lambda wdt: pl.BlockSpec((tm, wdt), lambda i: (i, 0))
    full = lambda a: pl.BlockSpec(a.shape, lambda i: (0, 0))
    return pl.pallas_call(
        _merge_kernel,
        grid=(t // tm,),
        in_specs=[row(RET_V_W), row(DIFF_OUT_W), row(RNN_WIDTH), row(3 * d), row(d), row(D_PLE),
                  full(wr), full(wd), full(wl), full(wo), full(wpg), full(wp),
                  pl.BlockSpec((1, d), lambda i: (0, 0))],
        out_specs=[row(d), row(d)],
        out_shape=[jax.ShapeDtypeStruct((t, d), F32), jax.ShapeDtypeStruct((t, d), hn_dtype)],
        compiler_params=_cparams(("parallel",), 56),
        name="merge",
    )(ro, do, lo, mg, x, p, wr, wd, wl, wo, wpg, wp, g_next.reshape(1, d))


def _lambda_init(layer):
    return 0.8 - 0.6 * math.exp(-0.3 * layer)


def _rope_tables(pos):
    half = RET_DK // 2
    inv = ROPE_BASE ** (-jnp.arange(half, dtype=F32) / half)
    ang = pos.astype(F32)[:, None] * inv[None, :]
    cos = jnp.cos(ang)
    sin = jnp.sin(ang)
    return jnp.concatenate([cos, cos], axis=-1), jnp.concatenate([-sin, sin], axis=-1)


def kernel(x_prompt, x_sample, cache_k, cache_v, state_ret, state_lru, state_conv, page_table, p_prompt, p_sample, rel_bias, norm_g, w_in, lambda_q1, lambda_k1, lambda_q2, lambda_k2, subln_g, conv_w, conv_b, gate_a_w, gate_a_b, gate_x_w, gate_x_b, lru_lambda, w_ret_out, w_diff_out, w_lru_out, w_o, w_ple, w_ple_gate, final_norm_g):
    bp, sp, d = x_prompt.shape
    bs, ss, _ = x_sample.shape
    assert d == D_MODEL and ss == 1
    depth = w_in.shape[0]
    n_pages, page = page_table.shape[1], cache_k.shape[2]
    past = n_pages * page
    tp = bp * sp
    blk = min(256, sp)
    tm_p = min(512, sp)
    tm_lru = min(256, sp)
    tm_merge = min(256, sp)

    cos_p, sin_p = _rope_tables(jnp.arange(sp, dtype=jnp.int32))
    cos_s, sin_s = _rope_tables(jnp.full((bs,), past, dtype=jnp.int32))
    bias_p = _bias_prompt(rel_bias, blk)
    bias_s, bias_s_new = _bias_sample(rel_bias, past)

    w_in_b = w_in.astype(BF16)
    wg_all = jnp.concatenate([gate_a_w, gate_x_w], axis=-1).astype(BF16)
    xp = x_prompt.reshape(tp, d)
    xs = x_sample.reshape(bs, d)
    hn_p = _rmsnorm(xp, norm_g[0], BF16, tm_p)
    hn_s = _rmsnorm(xs, norm_g[0], BF16, bs)

    outs = {k: [] for k in ("kp", "vp", "ks", "vs", "rp", "rs", "lp", "ls", "cp", "cs")}
    for l in range(depth):
        wg = wg_all[l]
        wr, wd, wl = w_ret_out[l].astype(BF16), w_diff_out[l].astype(BF16), w_lru_out[l].astype(BF16)
        wo, wpg, wp = w_o[l].astype(BF16), w_ple_gate[l].astype(BF16), w_ple[l].astype(BF16)
        lam0 = _lambda_init(l)
        lam_rows = jnp.stack([lambda_q1[l], lambda_k1[l], lambda_q2[l], lambda_k2[l]], axis=0)
        last = l == depth - 1
        g_next = final_norm_g if last else norm_g[l + 1]
        hn_dtype = F32 if last else BF16

        qk = _proj(hn_p, w_in_b, l, COL_ROT, "rotary", tm_p, tables=(cos_p, sin_p))
        rv = _proj(hn_p, w_in_b, l, COL_RV, "plain", tm_p)
        gates = _proj(hn_p, w_in_b, l, COL_SILU, "silu", tm_p)
        kf, vf, kb, qt, vt = _proj_qkv_t(hn_p, w_in_b, l, tm_p, blk)
        lx = _proj(hn_p, w_in_b, l, COL_LX, "plain", tm_p, out_dtype=F32)
        mg = _proj(hn_p, w_in_b, l, COL_MG, "sigmoid", tm_p)
        ro, s_fin = _retention_prompt(qk, rv, gates, bp, sp)
        do = _attention_prompt(qt, kb, vt, bias_p, gates, subln_g[l], lam_rows, lam0, bp, sp, blk)
        lo, h_last, conv_new = _lru_prompt(lx, gates, conv_w[l], conv_b[l], wg, gate_a_b[l],
                                           gate_x_b[l], lru_lambda[l], bp, sp, tm_lru)
        xp, hn_p = _merge(ro, do, lo, mg, xp, p_prompt[l].reshape(tp, D_PLE), wr, wd, wl, wo, wpg, wp,
                          g_next, hn_dtype, tm_merge)
        outs["kp"].append(kf.reshape(bp, sp, DIFF_KV_HEADS, 2 * DIFF_DH))
        outs["vp"].append(vf.reshape(bp, sp, DIFF_KV_HEADS, DIFF_DV))
        outs["rp"].append(s_fin)
        outs["lp"].append(h_last.reshape(bp, RNN_WIDTH))
        outs["cp"].append(conv_new)

        qk = _proj(hn_s, w_in_b, l, COL_ROT, "rotary", bs, tables=(cos_s, sin_s))
        rvdq = _proj(hn_s, w_in_b, l, COL_RV_DQ, "plain", bs)
        gates = _proj(hn_s, w_in_b, l, COL_SILU, "silu", bs)
        kf, vf, kb, vb = _proj(hn_s, w_in_b, l, (COL_KV,), "kv", bs)
        lx = _proj(hn_s, w_in_b, l, COL_LX, "plain", bs, out_dtype=F32)
        mg = _proj(hn_s, w_in_b, l, COL_MG, "sigmoid", bs)
        ro, s_new = _retention_step(qk, rvdq[:, :RET_V_W], gates[:, :RET_V_W], state_ret, l)
        do = _attention_decode(rvdq[:, RET_V_W:], kb, vb, gates[:, RET_V_W:RET_V_W + DIFF_OUT_W],
                               cache_k, cache_v, page_table, bias_s, bias_s_new, subln_g[l],
                               lam_rows, lam0, l)
        conv_prev = state_conv[:, l]
        lo, h_new = _lru_step(lx, conv_prev, state_lru[:, l], gates[:, 2 * RET_V_W:], conv_w[l],
                              conv_b[l], wg, gate_a_b[l], gate_x_b[l], lru_lambda[l])
        xs, hn_s = _merge(ro, do, lo, mg, xs, p_sample[l].reshape(bs, D_PLE), wr, wd, wl, wo, wpg, wp,
                          g_next, hn_dtype, bs)
        outs["ks"].append(kf.reshape(bs, 1, DIFF_KV_HEADS, 2 * DIFF_DH))
        outs["vs"].append(vf.reshape(bs, 1, DIFF_KV_HEADS, DIFF_DV))
        outs["rs"].append(s_new)
        outs["ls"].append(h_new)
        outs["cs"].append(jnp.concatenate([conv_prev[:, 1:], lx[:, None, :]], axis=1))

    st = lambda key: jnp.stack(outs[key], axis=1)
    return (hn_p.reshape(bp, sp, d), hn_s.reshape(bs, 1, d), st("kp"), st("vp"), st("ks"), st("vs"),
            st("rp"), st("rs"), st("lp"), st("ls"), st("cp"), st("cs"))
```

```python
import functools
import math

import numpy as np
import jax
import jax.numpy as jnp
from jax import lax
from jax.experimental import pallas as pl
from jax.experimental.pallas import tpu as pltpu

F32 = jnp.float32
BF16 = jnp.bfloat16

D_MODEL = 1024
D_PLE = 256
NORM_EPS = 1e-6
RET_HEADS = 4
RET_DK = 128
RET_DV = 256
RET_CHUNK = 128
ROPE_BASE = 10000.0
DIFF_HEADS = 8
DIFF_KV_HEADS = 4
DIFF_GROUP = DIFF_HEADS // DIFF_KV_HEADS
DIFF_DH = 64
DIFF_DV = 2 * DIFF_DH
REL_BUCKETS = 32
REL_MAX_DIST = 128
RNN_WIDTH = 1024
RNN_BLOCKS = 8
RNN_BLOCK_W = RNN_WIDTH // RNN_BLOCKS
CONV_WIDTH = 4
LRU_C = 8.0

RET_QK_W = RET_HEADS * RET_DK
RET_V_W = RET_HEADS * RET_DV
DIFF_Q_W = DIFF_HEADS * 2 * DIFF_DH
DIFF_K_W = DIFF_KV_HEADS * 2 * DIFF_DH
DIFF_V_W = DIFF_KV_HEADS * DIFF_DV
DIFF_OUT_W = DIFF_HEADS * DIFF_DV
IN_SPLITS = (RET_QK_W, RET_QK_W, RET_V_W, RET_V_W,
             DIFF_Q_W, DIFF_K_W, DIFF_V_W, DIFF_OUT_W,
             RNN_WIDTH, RNN_WIDTH, 3 * D_MODEL)
IN_OFFS = tuple(int(v) for v in np.cumsum((0,) + IN_SPLITS))

LANES = 128
SUBLANES = 8
MASK_VALUE = -1e30
LOG2E = math.log2(math.e)
VT_ONES = 2 * SUBLANES
VT_ROWS = DIFF_DV + VT_ONES
FAR_GROUP = 4
MIB = 1024 * 1024


def _cparams(sem, vmem_mib):
    return pltpu.CompilerParams(dimension_semantics=sem, vmem_limit_bytes=int(vmem_mib * MIB))


def _rmsnorm_kernel(x_ref, g_ref, o_ref):
    x = x_ref[...]
    y = x * lax.rsqrt(jnp.mean(x * x, axis=-1, keepdims=True) + NORM_EPS)
    o_ref[...] = (y * g_ref[...]).astype(o_ref.dtype)


def _rmsnorm(x, g, out_dtype, tm):
    t, d = x.shape
    return pl.pallas_call(
        _rmsnorm_kernel,
        grid=(t // tm,),
        in_specs=[pl.BlockSpec((tm, d), lambda i: (i, 0)),
                  pl.BlockSpec((1, d), lambda i: (0, 0))],
        out_specs=pl.BlockSpec((tm, d), lambda i: (i, 0)),
        out_shape=jax.ShapeDtypeStruct((t, d), out_dtype),
        compiler_params=_cparams(("parallel",), 32),
        name="rmsnorm",
    )(x, g.reshape(1, d))


def _proj_qkv_t_kernel(x_ref, wq_ref, wkv_ref, kf_ref, vf_ref, kb_ref, qt_ref, vt_ref):
    blk = qt_ref.shape[-1]

    def matmuls(j):
        x = x_ref[j * blk:(j + 1) * blk, :]
        return (jnp.dot(x, wq_ref[...], preferred_element_type=F32),
                jnp.dot(x, wkv_ref[...], preferred_element_type=F32))

    def epilogue(j, q, kv):
        rows = slice(j * blk, (j + 1) * blk)
        k = kv[:, :DIFF_K_W]
        v = kv[:, DIFF_K_W:]
        for h in range(DIFF_KV_HEADS):
            dst = pl.ds(j * blk * DIFF_KV_HEADS + h, blk, stride=DIFF_KV_HEADS)
            kf_ref[dst, :] = k[:, h * 2 * DIFF_DH:(h + 1) * 2 * DIFF_DH]
            vf_ref[dst, :] = v[:, h * DIFF_DV:(h + 1) * DIFF_DV]
        kb_ref[rows, :] = k.astype(BF16)
        qt_ref[j] = (q * (DIFF_DH ** -0.5 * LOG2E)).T.astype(BF16)
        vt = v.T.astype(BF16)
        for h in range(DIFF_KV_HEADS):
            vt_ref[j, h * VT_ROWS:h * VT_ROWS + DIFF_DV, :] = vt[h * DIFF_DV:(h + 1) * DIFF_DV]
            vt_ref[j, h * VT_ROWS + DIFF_DV:(h + 1) * VT_ROWS, :] = jnp.ones((VT_ONES, blk), BF16)

    _pipeline_rows(qt_ref.shape[0], matmuls, epilogue)


def _pipeline_rows(n_chunks, matmuls, epilogue):
    pending = {}
    for c in range(n_chunks + 1):
        if c < n_chunks:
            pending[c] = matmuls(c)
        if c >= 1:
            epilogue(c - 1, *pending.pop(c - 1))


def _sigmoid(x):
    return 0.5 * jnp.tanh(0.5 * x) + 0.5


def _proj_kernel(x_ref, w_ref, *refs, kind, chunk):
    tm = x_ref.shape[0]

    def matmuls(c):
        return (jnp.dot(x_ref[c * chunk:(c + 1) * chunk, :], w_ref[...], preferred_element_type=F32),)

    def epilogue(c, acc):
        rows = slice(c * chunk, (c + 1) * chunk)
        if kind == "plain":
            (o_ref,) = refs
            o_ref[rows, :] = acc.astype(o_ref.dtype)
        elif kind == "silu":
            (o_ref,) = refs
            o_ref[rows, :] = (acc * _sigmoid(acc)).astype(o_ref.dtype)
        elif kind == "sigmoid":
            (o_ref,) = refs
            o_ref[rows, :] = _sigmoid(acc).astype(o_ref.dtype)
        elif kind == "rotary":
            cos_ref, sin_ref, o_ref = refs
            cos = cos_ref[rows, :]
            sin = sin_ref[rows, :]
            for h in range(2 * RET_HEADS):
                z = acc[:, h * RET_DK:(h + 1) * RET_DK]
                r = z * cos + pltpu.roll(z, RET_DK // 2, 1) * sin
                if h >= RET_HEADS:
                    r = r * (RET_DK ** -0.5)
                o_ref[rows, h * RET_DK:(h + 1) * RET_DK] = r.astype(o_ref.dtype)
        elif kind == "kv":
            kf_ref, vf_ref, kb_ref, vb_ref = refs
            k = acc[:, :DIFF_K_W]
            v = acc[:, DIFF_K_W:]
            kf_ref[rows, :] = k
            vf_ref[rows, :] = v
            kb_ref[rows, :] = k.astype(BF16)
            vb_ref[rows, :] = v.astype(BF16)
        else:
            raise ValueError(kind)

    _pipeline_rows(tm // chunk, matmuls, epilogue)


PROJ_TN = 1024
PROJ_CHUNK = 256
COL_ROT = (0,)
COL_RV = (1,)
COL_RV_DQ = (1, 3)
COL_SILU = (2, 5, 7)
COL_DQ = 3
COL_KV = 4
COL_LX = (6,)
COL_MG = (8, 9, 10)
assert IN_OFFS[2] == PROJ_TN and IN_OFFS[5] == COL_KV * PROJ_TN and IN_OFFS[10] == COL_MG[0] * PROJ_TN


def _col_index(cols, j):
    if len(cols) == 1:
        return cols[0]
    step = cols[1] - cols[0]
    idx = cols[0] + step * j
    if cols[-1] != cols[0] + step * (len(cols) - 1):
        idx = idx - j // (len(cols) - 1)
    return idx


def _proj_qkv_t(hn, w_all, layer, tm, blk):
    t, d = hn.shape
    nj = tm // blk
    half = pl.BlockSpec((tm, DIFF_K_W), lambda i: (i, 0))
    per_head = pl.BlockSpec((tm * DIFF_KV_HEADS, DIFF_DV), lambda i: (i, 0))
    return pl.pallas_call(
        _proj_qkv_t_kernel,
        grid=(t // tm,),
        in_specs=[pl.BlockSpec((tm, d), lambda i: (i, 0)),
                  pl.BlockSpec((None, d, PROJ_TN), lambda i: (layer, 0, COL_DQ)),
                  pl.BlockSpec((None, d, PROJ_TN), lambda i: (layer, 0, COL_KV))],
        out_specs=[per_head, per_head, half,
                   pl.BlockSpec((nj, DIFF_Q_W, blk), lambda i: (i, 0, 0)),
                   pl.BlockSpec((nj, DIFF_KV_HEADS * VT_ROWS, blk), lambda i: (i, 0, 0))],
        out_shape=[jax.ShapeDtypeStruct((t * DIFF_KV_HEADS, 2 * DIFF_DH), F32),
                   jax.ShapeDtypeStruct((t * DIFF_KV_HEADS, DIFF_DV), F32),
                   jax.ShapeDtypeStruct((t, DIFF_K_W), BF16),
                   jax.ShapeDtypeStruct((t // blk, DIFF_Q_W, blk), BF16),
                   jax.ShapeDtypeStruct((t // blk, DIFF_KV_HEADS * VT_ROWS, blk), BF16)],
        compiler_params=_cparams(("parallel",), 48),
        name="proj_qkv_t",
    )(hn, w_all, w_all)


def _proj(hn, w_all, layer, cols, kind, tm, out_dtype=BF16, tables=None):
    t, d = hn.shape
    tn = PROJ_TN
    n = tn * len(cols)
    grid = (len(cols), t // tm)
    in_specs = [pl.BlockSpec((tm, d), lambda j, i: (i, 0)),
                pl.BlockSpec((None, d, tn), lambda j, i: (layer, 0, _col_index(cols, j)))]
    args = [hn, w_all]
    if kind == "rotary":
        cos, sin = tables
        nb = cos.shape[0] // tm
        spec = pl.BlockSpec((tm, RET_DK), lambda j, i: (i % nb, 0))
        in_specs += [spec, spec]
        args += [cos, sin]
    if kind == "kv":
        half = pl.BlockSpec((tm, DIFF_K_W), lambda j, i: (i, 0))
        out_specs = [half, half, half, half]
        out_shape = [jax.ShapeDtypeStruct((t, DIFF_K_W), F32),
                     jax.ShapeDtypeStruct((t, DIFF_V_W), F32),
                     jax.ShapeDtypeStruct((t, DIFF_K_W), BF16),
                     jax.ShapeDtypeStruct((t, DIFF_V_W), BF16)]
    else:
        out_specs = pl.BlockSpec((tm, tn), lambda j, i: (i, j))
        out_shape = jax.ShapeDtypeStruct((t, n), out_dtype)
    return pl.pallas_call(
        functools.partial(_proj_kernel, kind=kind, chunk=min(PROJ_CHUNK, tm)),
        grid=grid, in_specs=in_specs, out_specs=out_specs, out_shape=out_shape,
        compiler_params=_cparams(("parallel", "parallel"), 40),
        name="proj_" + kind,
    )(*args)


def _retention_kernel(qk_ref, v_ref, gate_ref, dmask_ref, qdec_ref, kdec_ref,
                      o_ref, sfin_ref, state_ref, *, chunk_dec):
    c = pl.program_id(1)

    @pl.when(c == 0)
    def _():
        state_ref[...] = jnp.zeros_like(state_ref)

    for h in range(RET_HEADS):
        q = qk_ref[:, h * RET_DK:(h + 1) * RET_DK]
        k = qk_ref[:, RET_QK_W + h * RET_DK:RET_QK_W + (h + 1) * RET_DK]
        v = v_ref[:, h * RET_DV:(h + 1) * RET_DV]
        st = state_ref[h]
        s = lax.dot_general(q, k, (((1,), (1,)), ((), ())), preferred_element_type=F32)
        s = s * dmask_ref[h]
        o = jnp.dot(s.astype(BF16), v, preferred_element_type=F32)
        o = o + jnp.dot(q, st.astype(BF16), preferred_element_type=F32) * qdec_ref[:, h:h + 1]
        kd = (k.astype(F32) * kdec_ref[:, h:h + 1]).astype(BF16)
        state_ref[h] = st * chunk_dec[h] + lax.dot_general(
            kd, v, (((0,), (0,)), ((), ())), preferred_element_type=F32)
        o = o * lax.rsqrt(jnp.mean(o * o, axis=-1, keepdims=True) + NORM_EPS)
        g = gate_ref[:, h * RET_DV:(h + 1) * RET_DV].astype(F32)
        o_ref[:, h * RET_DV:(h + 1) * RET_DV] = (o * g).astype(o_ref.dtype)

    @pl.when(c == pl.num_programs(1) - 1)
    def _():
        sfin_ref[...] = state_ref[...]


def _retention_prompt(qk, rvdq, gates, b, s):
    c = math.gcd(s, RET_CHUNK)
    nc = s // c
    log_g = np.log1p(-(2.0 ** (-5.0 - np.arange(RET_HEADS, dtype=np.float64))))
    idx = np.arange(c, dtype=np.float64)
    dist = idx[:, None] - idx[None, :]
    dmask = np.where(dist[None] >= 0, np.exp(np.maximum(dist, 0.0)[None] * log_g[:, None, None]), 0.0)
    qdec = np.exp((idx + 1.0)[:, None] * log_g[None, :])
    kdec = np.exp((c - 1.0 - idx)[:, None] * log_g[None, :])
    chunk_dec = tuple(float(x) for x in np.exp(c * log_g).astype(np.float32))
    t = b * s
    return pl.pallas_call(
        functools.partial(_retention_kernel, chunk_dec=chunk_dec),
        grid=(b, nc),
        in_specs=[pl.BlockSpec((c, 2 * RET_QK_W), lambda i, j: (i * nc + j, 0)),
                  pl.BlockSpec((c, RET_V_W), lambda i, j: (i * nc + j, 0)),
                  pl.BlockSpec((c, RET_V_W), lambda i, j: (i * nc + j, 0)),
                  pl.BlockSpec((RET_HEADS, c, c), lambda i, j: (0, 0, 0)),
                  pl.BlockSpec((c, RET_HEADS), lambda i, j: (0, 0)),
                  pl.BlockSpec((c, RET_HEADS), lambda i, j: (0, 0))],
        out_specs=[pl.BlockSpec((c, RET_V_W), lambda i, j: (i * nc + j, 0)),
                   pl.BlockSpec((None, RET_HEADS, RET_DK, RET_DV), lambda i, j: (i, 0, 0, 0))],
        out_shape=[jax.ShapeDtypeStruct((t, RET_V_W), BF16),
                   jax.ShapeDtypeStruct((b, RET_HEADS, RET_DK, RET_DV), F32)],
        scratch_shapes=[pltpu.VMEM((RET_HEADS, RET_DK, RET_DV), F32)],
        compiler_params=_cparams(("parallel", "arbitrary"), 32),
        name="retention_prompt",
    )(qk, rvdq, gates, jnp.asarray(dmask, F32), jnp.asarray(qdec, F32), jnp.asarray(kdec, F32))


def _retention_step_kernel(qk_ref, v_ref, gate_ref, s_ref, o_ref, snew_ref, *, decay):
    sb = qk_ref.shape[0]
    rowid = lax.broadcasted_iota(jnp.int32, (sb, 1), 0)

    def per_sequence(b, outs):
        new_outs = []
        for h in range(RET_HEADS):
            q = qk_ref[:, h * RET_DK:(h + 1) * RET_DK]
            k = qk_ref[:, RET_QK_W + h * RET_DK:RET_QK_W + (h + 1) * RET_DK]
            v = v_ref[:, h * RET_DV:(h + 1) * RET_DV]
            st = s_ref[b, h]
            qs = jnp.dot(q, st.astype(BF16), preferred_element_type=F32)
            new_outs.append(outs[h] + jnp.where(rowid == b, decay[h] * qs, 0.0))
            kb = jnp.where(rowid == b, k, jnp.zeros_like(k))
            snew_ref[b, h] = st * decay[h] + lax.dot_general(
                kb, v, (((0,), (0,)), ((), ())), preferred_element_type=F32)
        return tuple(new_outs)

    outs = []
    for h in range(RET_HEADS):
        q = qk_ref[:, h * RET_DK:(h + 1) * RET_DK].astype(F32)
        k = qk_ref[:, RET_QK_W + h * RET_DK:RET_QK_W + (h + 1) * RET_DK].astype(F32)
        v = v_ref[:, h * RET_DV:(h + 1) * RET_DV].astype(F32)
        outs.append(jnp.sum(q * k, axis=1, keepdims=True) * v)
    outs = lax.fori_loop(0, sb, per_sequence, tuple(outs))
    for h in range(RET_HEADS):
        o = outs[h]
        o = o * lax.rsqrt(jnp.mean(o * o, axis=-1, keepdims=True) + NORM_EPS)
        g = gate_ref[:, h * RET_DV:(h + 1) * RET_DV].astype(F32)
        o_ref[:, h * RET_DV:(h + 1) * RET_DV] = (o * g).astype(o_ref.dtype)


def _retention_step(qk, rvdq, gates, state_ret, layer):
    bs = qk.shape[0]
    sb = min(2 * SUBLANES, bs)
    log_g = np.log1p(-(2.0 ** (-5.0 - np.arange(RET_HEADS, dtype=np.float64))))
    decay = tuple(float(x) for x in np.exp(log_g).astype(np.float32))
    return pl.pallas_call(
        functools.partial(_retention_step_kernel, decay=decay),
        grid=(bs // sb,),
        in_specs=[pl.BlockSpec((sb, 2 * RET_QK_W), lambda i: (i, 0)),
                  pl.BlockSpec((sb, RET_V_W), lambda i: (i, 0)),
                  pl.BlockSpec((sb, RET_V_W), lambda i: (i, 0)),
                  pl.BlockSpec((sb, None, RET_HEADS, RET_DK, RET_DV),
                               lambda i: (i, layer, 0, 0, 0))],
        out_specs=[pl.BlockSpec((sb, RET_V_W), lambda i: (i, 0)),
                   pl.BlockSpec((sb, RET_HEADS, RET_DK, RET_DV), lambda i: (i, 0, 0, 0))],
        out_shape=[jax.ShapeDtypeStruct((bs, RET_V_W), BF16),
                   jax.ShapeDtypeStruct((bs, RET_HEADS, RET_DK, RET_DV), F32)],
        compiler_params=_cparams(("parallel",), 48),
        name="retention_step",
    )(qk, rvdq, gates, state_ret)


def _t5_bucket(rel):
    n = jnp.maximum(rel, 0)
    max_exact = REL_BUCKETS // 2
    nf = jnp.maximum(n.astype(F32), 1.0)
    large = max_exact + (jnp.log(nf / max_exact) / math.log(REL_MAX_DIST / max_exact)
                         * (REL_BUCKETS - max_exact)).astype(jnp.int32)
    large = jnp.minimum(large, REL_BUCKETS - 1)
    return jnp.where(n < max_exact, n, large)


def _bias_lookup(tab_ref, bucket, head):
    out = jnp.zeros(bucket.shape, F32)
    far = tab_ref[REL_BUCKETS - 1, head]
    for bkt in range(REL_BUCKETS - 1):
        out = jnp.where(bucket == bkt, tab_ref[bkt, head] - far, out)
    return out


def _bias_prompt_kernel(tab_ref, o_ref, *, blk):
    i = lax.broadcasted_iota(jnp.int32, (blk, blk), 1)
    j = lax.broadcasted_iota(jnp.int32, (blk, blk), 0)
    for which in range(2):
        rel = i - j + which * blk
        bucket = _t5_bucket(rel)
        for h in range(DIFF_HEADS):
            o_ref[which, h] = jnp.where(rel >= 0, _bias_lookup(tab_ref, bucket, h) * LOG2E, MASK_VALUE)


def _bias_prompt(rel_bias, blk):
    return pl.pallas_call(
        functools.partial(_bias_prompt_kernel, blk=blk),
        in_specs=[pl.BlockSpec(memory_space=pltpu.SMEM)],
        out_specs=pl.BlockSpec(memory_space=pltpu.VMEM),
        out_shape=jax.ShapeDtypeStruct((2, DIFF_HEADS, blk, blk), F32),
        compiler_params=pltpu.CompilerParams(vmem_limit_bytes=32 * MIB),
        name="bias_prompt",
    )(rel_bias)


def _bias_sample_kernel(tab_ref, o_ref, onew_ref, *, past, width):
    rows = 2 * DIFF_HEADS
    r = lax.broadcasted_iota(jnp.int32, (rows, width), 0)
    lane = lax.broadcasted_iota(jnp.int32, (rows, width), 1)
    head = r % DIFF_HEADS
    kvh = lane % DIFF_KV_HEADS
    pos = lane // DIFF_KV_HEADS
    bucket = _t5_bucket(past - pos)
    out = jnp.zeros((rows, width), F32)
    for h in range(DIFF_HEADS):
        out = jnp.where(head == h, _bias_lookup(tab_ref, bucket, h), out)
    o_ref[...] = jnp.where(kvh == head // DIFF_GROUP, out, MASK_VALUE)
    rn = lax.broadcasted_iota(jnp.int32, (rows, LANES), 0) % DIFF_HEADS
    new = jnp.zeros((rows, LANES), F32)
    zero_bucket = _t5_bucket(jnp.zeros((rows, LANES), jnp.int32))
    for h in range(DIFF_HEADS):
        new = jnp.where(rn == h, _bias_lookup(tab_ref, zero_bucket, h), new)
    onew_ref[...] = new


def _bias_sample(rel_bias, past):
    width = past * DIFF_KV_HEADS
    return pl.pallas_call(
        functools.partial(_bias_sample_kernel, past=past, width=width),
        in_specs=[pl.BlockSpec(memory_space=pltpu.SMEM)],
        out_specs=[pl.BlockSpec(memory_space=pltpu.VMEM), pl.BlockSpec(memory_space=pltpu.VMEM)],
        out_shape=[jax.ShapeDtypeStruct((2 * DIFF_HEADS, width), F32),
                   jax.ShapeDtypeStruct((2 * DIFF_HEADS, LANES), F32)],
        compiler_params=pltpu.CompilerParams(vmem_limit_bytes=32 * MIB),
        name="bias_sample",
    )(rel_bias)


def _lambda_full(lam_ref, lam0):
    a = jnp.sum(lam_ref[0:1, :] * lam_ref[1:2, :], axis=-1, keepdims=True)
    b = jnp.sum(lam_ref[2:3, :] * lam_ref[3:4, :], axis=-1, keepdims=True)
    return jnp.exp(a) - jnp.exp(b) + lam0


def _attn_kernel(qt_in_ref, k_ref, vt_ref, bias_ref, gate_ref, sg_ref, lam_ref, o_ref,
                 qt_ref, m_ref, acc_ref, *, blk, lam0):
    qi = pl.program_id(2)
    n_chunks = 2 * DIFF_GROUP
    row = lax.broadcasted_iota(jnp.int32, (DIFF_DV, blk), 0)
    for g in range(DIFF_GROUP):
        qg = qt_in_ref[g * DIFF_DV:(g + 1) * DIFF_DV, :]
        for m in range(2):
            keep = (row < DIFF_DH) if m == 0 else (row >= DIFF_DH)
            c = m * DIFF_GROUP + g
            qt_ref[:, c * blk:(c + 1) * blk] = jnp.where(keep, qg, jnp.zeros_like(qg))
    m_ref[...] = jnp.full(m_ref.shape, MASK_VALUE, F32)
    acc_ref[...] = jnp.zeros(acc_ref.shape, F32)

    def run_blocks(blocks):
        s_lag = 2
        kbs = [k_ref[pl.ds(pl.multiple_of(ki * blk, blk), blk), :] for ki, _ in blocks]
        vts = [vt_ref[ki] for ki, _ in blocks]
        units = [(bi, c) for bi in range(len(blocks)) for c in range(n_chunks)]
        n_units = len(units)

        def scores(u):
            bi, c = units[u]
            s = jnp.dot(kbs[bi], qt_ref[:, c * blk:(c + 1) * blk], preferred_element_type=F32)
            if blocks[bi][1] is not None:
                s = s + bias_ref[blocks[bi][1], c % DIFF_GROUP]
            return s

        def softmax(u, s):
            c = units[u][1]
            cs = slice(c * blk, (c + 1) * blk)
            m_prev = m_ref[:, cs]
            m_new = jnp.maximum(m_prev, jnp.max(s, axis=0, keepdims=True))
            p = jnp.exp2(s - m_new)
            alpha = jnp.exp2(m_prev - m_new)
            m_ref[:, cs] = m_new
            return p.astype(BF16), alpha

        def accumulate(u, p, alpha):
            bi, c = units[u]
            cs = slice(c * blk, (c + 1) * blk)
            acc_ref[:, cs] = acc_ref[:, cs] * alpha + jnp.dot(vts[bi], p, preferred_element_type=F32)

        s_vals, p_vals = {}, {}
        for t in range(n_units + s_lag + 1):
            if t < n_units:
                s_vals[t] = scores(t)
            if 0 <= t - s_lag < n_units:
                p_vals[t - s_lag] = softmax(t - s_lag, s_vals.pop(t - s_lag))
            if 0 <= t - s_lag - 1 < n_units:
                accumulate(t - s_lag - 1, *p_vals.pop(t - s_lag - 1))

    n_far = jnp.maximum(qi - 1, 0)
    n_main = n_far // FAR_GROUP
    n_left = n_far - n_main * FAR_GROUP

    def far_body(i, carry):
        run_blocks([(FAR_GROUP * i + j, None) for j in range(FAR_GROUP)])
        return carry

    lax.fori_loop(0, n_main, far_body, 0)

    for left in range(FAR_GROUP):
        @pl.when(jnp.logical_and(qi >= 1, n_left == left))
        def _():
            base = n_main * FAR_GROUP
            run_blocks([(base + j, None) for j in range(left)] + [(qi - 1, 1), (qi, 0)])

    @pl.when(qi == 0)
    def _():
        run_blocks([(qi, 0)])

    lam = _lambda_full(lam_ref, lam0)
    a = acc_ref[0:DIFF_DV, :] / acc_ref[DIFF_DV:DIFF_DV + 1, :]
    for g in range(DIFF_GROUP):
        ot = a[:, g * blk:(g + 1) * blk] - lam * a[:, (DIFF_GROUP + g) * blk:(DIFF_GROUP + g + 1) * blk]
        ot = ot * lax.rsqrt(jnp.mean(ot * ot, axis=0, keepdims=True) + NORM_EPS)
        ot = ot * sg_ref[...] * (1.0 - lam0)
        gate = gate_ref[:, g * DIFF_DV:(g + 1) * DIFF_DV].astype(F32)
        o_ref[:, g * DIFF_DV:(g + 1) * DIFF_DV] = (ot.T * gate).astype(o_ref.dtype)


def _attention_prompt(qt, kb, vt, bias, gates, subln_g, lam_rows, lam0, b, s, blk):
    nq = s // blk
    t = b * s
    gcol0 = RET_V_W // (DIFF_GROUP * DIFF_DV)
    kvw = 2 * DIFF_DH
    return pl.pallas_call(
        functools.partial(_attn_kernel, blk=blk, lam0=lam0),
        grid=(b, DIFF_KV_HEADS, nq),
        in_specs=[pl.BlockSpec((None, DIFF_GROUP * DIFF_DV, blk), lambda i, h, q: (i * nq + q, h, 0)),
                  pl.BlockSpec((s, kvw), lambda i, h, q: (i, h)),
                  pl.BlockSpec((nq, VT_ROWS, blk), lambda i, h, q: (i, h, 0)),
                  pl.BlockSpec((2, DIFF_GROUP, blk, blk), lambda i, h, q: (0, h, 0, 0)),
                  pl.BlockSpec((blk, DIFF_GROUP * DIFF_DV), lambda i, h, q: (i * nq + q, gcol0 + h)),
                  pl.BlockSpec((DIFF_DV, 1), lambda i, h, q: (0, 0)),
                  pl.BlockSpec((4, DIFF_DH), lambda i, h, q: (0, 0))],
        out_specs=pl.BlockSpec((blk, DIFF_GROUP * DIFF_DV), lambda i, h, q: (i * nq + q, h)),
        out_shape=jax.ShapeDtypeStruct((t, DIFF_OUT_W), BF16),
        scratch_shapes=[pltpu.VMEM((DIFF_DV, 4 * blk), BF16),
                        pltpu.VMEM((1, 4 * blk), F32),
                        pltpu.VMEM((VT_ROWS, 4 * blk), F32)],
        compiler_params=_cparams(("parallel", "parallel", "arbitrary"), 48),
        name="attention_prompt",
    )(qt, kb, vt, bias, gates, subln_g.reshape(DIFF_DV, 1), lam_rows)


def _attn_decode_kernel(pt_ref, *refs, n_pages, rows_per_page, lam0):
    del pt_ref
    k_refs = refs[:n_pages]
    v_refs = refs[n_pages:2 * n_pages]
    (qt_ref, kn_ref, vn_ref, bias_ref, bnew_ref, gate_ref, sg_ref, lam_ref,
     o_ref, s_ref) = refs[2 * n_pages:]
    qt = qt_ref[...]
    w = rows_per_page
    for p in range(n_pages):
        kp = k_refs[p][...].astype(BF16)
        sp = lax.dot_general(qt, kp, (((1,), (1,)), ((), ())), preferred_element_type=F32)
        s_ref[:, p * w:(p + 1) * w] = sp + bias_ref[:, p * w:(p + 1) * w]
    s = s_ref[...]
    s_new = jnp.sum(qt.astype(F32) * kn_ref[...].astype(F32), axis=1, keepdims=True)
    s_new = s_new + bnew_ref[:, 0:1]
    m = jnp.maximum(jnp.max(s, axis=1, keepdims=True), s_new)
    e_new = jnp.exp(s_new - m)
    pr = jnp.exp(s - m)
    l = jnp.sum(pr, axis=1, keepdims=True) + e_new
    prb = pr.astype(BF16)
    acc = e_new * vn_ref[...].astype(F32)
    for p in range(n_pages):
        vp = v_refs[p][...].astype(BF16)
        acc = acc + jnp.dot(prb[:, p * w:(p + 1) * w], vp, preferred_element_type=F32)
    a = acc / l
    lam = _lambda_full(lam_ref, lam0)
    o = a[:DIFF_HEADS] - lam * a[DIFF_HEADS:]
    o = o * lax.rsqrt(jnp.mean(o * o, axis=-1, keepdims=True) + NORM_EPS)
    o = o * sg_ref[...] * (1.0 - lam0)
    o_ref[...] = o * gate_ref[...]


def _attention_decode(dq, kb_new, vb_new, gate, cache_k, cache_v, page_table, bias, bias_new,
                      subln_g, lam_rows, lam0, layer):
    bs = dq.shape[0]
    n_pool, depth, page = cache_k.shape[0], cache_k.shape[1], cache_k.shape[2]
    n_pages = page_table.shape[1]
    w = page * DIFF_KV_HEADS
    ck = cache_k.reshape(n_pool, depth, w, 2 * DIFF_DH)
    cv = cache_v.reshape(n_pool, depth, w, DIFF_DV)
    q = (dq * (DIFF_DH ** -0.5)).reshape(bs, DIFF_KV_HEADS, DIFF_GROUP, 2, DIFF_DH)
    q = jnp.transpose(q, (0, 3, 1, 2, 4)).reshape(bs, 2, DIFF_HEADS, DIFF_DH)
    z = jnp.zeros_like(q[:, 0])
    qt = jnp.concatenate([jnp.concatenate([q[:, 0], z], axis=-1),
                          jnp.concatenate([z, q[:, 1]], axis=-1)], axis=1)
    kn = jnp.tile(jnp.repeat(kb_new.reshape(bs, DIFF_KV_HEADS, 2 * DIFF_DH), DIFF_GROUP, axis=1), (1, 2, 1))
    vn = jnp.tile(jnp.repeat(vb_new.reshape(bs, DIFF_KV_HEADS, DIFF_DV), DIFF_GROUP, axis=1), (1, 2, 1))
    g3 = gate.astype(F32).reshape(bs, DIFF_HEADS, DIFF_DV)
    rows = 2 * DIFF_HEADS

    def page_spec(p):
        return pl.BlockSpec((None, None, w, 2 * DIFF_DH), lambda i, pt: (pt[i, p], layer, 0, 0))

    row_spec = pl.BlockSpec((None, rows, LANES), lambda i, pt: (i, 0, 0))
    in_specs = ([page_spec(p) for p in range(n_pages)] + [page_spec(p) for p in range(n_pages)]
                + [row_spec, row_spec, row_spec,
                   pl.BlockSpec((rows, n_pages * w), lambda i, pt: (0, 0)),
                   pl.BlockSpec((rows, LANES), lambda i, pt: (0, 0)),
                   pl.BlockSpec((None, DIFF_HEADS, DIFF_DV), lambda i, pt: (i, 0, 0)),
                   pl.BlockSpec((1, DIFF_DV), lambda i, pt: (0, 0)),
                   pl.BlockSpec((4, DIFF_DH), lambda i, pt: (0, 0))])
    out = pl.pallas_call(
        functools.partial(_attn_decode_kernel, n_pages=n_pages, rows_per_page=w, lam0=lam0),
        grid_spec=pltpu.PrefetchScalarGridSpec(
            num_scalar_prefetch=1, grid=(bs,), in_specs=in_specs,
            out_specs=pl.BlockSpec((None, DIFF_HEADS, DIFF_DV), lambda i, pt: (i, 0, 0)),
            scratch_shapes=[pltpu.VMEM((rows, n_pages * w), F32)]),
        out_shape=jax.ShapeDtypeStruct((bs, DIFF_HEADS, DIFF_DV), F32),
        compiler_params=_cparams(("arbitrary",), 48),
        name="attention_decode",
    )(page_table, *([ck] * n_pages), *([cv] * n_pages), qt, kn, vn, bias, bias_new, g3,
      subln_g.reshape(1, DIFF_DV), lam_rows)
    return out.reshape(bs, DIFF_OUT_W).astype(BF16)


def _softplus(x):
    return jnp.maximum(x, 0.0) + jnp.log1p(jnp.exp(-jnp.abs(x)))


def _lru_coeffs(xc, wg_ref, ba_ref, bx_ref, lam_ref, n):
    sl = slice(n * RNN_BLOCK_W, (n + 1) * RNN_BLOCK_W)
    xn = xc[:, sl]
    z = jnp.dot(xn.astype(BF16), wg_ref[n], preferred_element_type=F32)
    rg = _sigmoid(z[:, :RNN_BLOCK_W] + ba_ref[:, sl])
    ig = _sigmoid(z[:, RNN_BLOCK_W:] + bx_ref[:, sl])
    log_a = -LRU_C * rg * _softplus(-lam_ref[:, sl])
    a = jnp.exp(log_a)
    bt = jnp.sqrt(-jnp.tanh(log_a) * (1.0 + a * a)) * (ig * xn)
    return a, bt


def _lru_prompt_kernel(x_ref, gate_ref, cw_ref, cb_ref, wg_ref, ba_ref, bx_ref, lam_ref,
                       o_ref, hlast_ref, conv_ref, xbuf_ref, a_ref, b_ref, h_ref, *, tm):
    t = pl.program_id(1)
    pad = SUBLANES

    @pl.when(t == 0)
    def _():
        xbuf_ref[0:pad, :] = jnp.zeros((pad, RNN_WIDTH), F32)
        h_ref[...] = jnp.zeros(h_ref.shape, F32)

    xbuf_ref[pad:pad + tm, :] = x_ref[...]
    xc = cb_ref[...]
    for j in range(CONV_WIDTH):
        off = pad - (CONV_WIDTH - 1) + j
        xc = xc + xbuf_ref[off:off + tm, :] * cw_ref[j:j + 1, :]
    conv_ref[...] = xbuf_ref[pad + tm - (CONV_WIDTH - 1):pad + tm, :]
    xbuf_ref[0:pad, :] = xbuf_ref[tm:tm + pad, :]

    for n in range(RNN_BLOCKS):
        a, bt = _lru_coeffs(xc, wg_ref, ba_ref, bx_ref, lam_ref, n)
        sl = slice(n * RNN_BLOCK_W, (n + 1) * RNN_BLOCK_W)
        r8 = lax.broadcasted_iota(jnp.int32, a.shape, 0) % SUBLANES
        for d in (1, 2, 4):
            a_sh = pltpu.roll(a, d, 0)
            b_sh = pltpu.roll(bt, d, 0)
            ok = r8 >= d
            bt = jnp.where(ok, a * b_sh + bt, bt)
            a = jnp.where(ok, a * a_sh, a)
        a_ref[:, sl] = a
        b_ref[:, sl] = bt

    h = h_ref[...]
    for gi in range(tm // SUBLANES):
        rs = slice(gi * SUBLANES, (gi + 1) * SUBLANES)
        hg = a_ref[rs, :] * h + b_ref[rs, :]
        b_ref[rs, :] = hg
        h = hg[SUBLANES - 1:SUBLANES, :]
    h_ref[...] = h
    hlast_ref[...] = h
    o_ref[...] = (b_ref[...] * gate_ref[...].astype(F32)).astype(o_ref.dtype)


def _lru_prompt(lx, gates, conv_w, conv_b, wg, ba, bx, lam, b, s, tm):
    nt = s // tm
    t = b * s
    gcol = 2
    vec = pl.BlockSpec((1, RNN_WIDTH), lambda i, j: (0, 0))
    return pl.pallas_call(
        functools.partial(_lru_prompt_kernel, tm=tm),
        grid=(b, nt),
        in_specs=[pl.BlockSpec((tm, RNN_WIDTH), lambda i, j: (i * nt + j, 0)),
                  pl.BlockSpec((tm, RNN_WIDTH), lambda i, j: (i * nt + j, gcol)),
                  pl.BlockSpec((CONV_WIDTH, RNN_WIDTH), lambda i, j: (0, 0)),
                  vec,
                  pl.BlockSpec((RNN_BLOCKS, RNN_BLOCK_W, 2 * RNN_BLOCK_W), lambda i, j: (0, 0, 0)),
                  vec, vec, vec],
        out_specs=[pl.BlockSpec((tm, RNN_WIDTH), lambda i, j: (i * nt + j, 0)),
                   pl.BlockSpec((None, 1, RNN_WIDTH), lambda i, j: (i, 0, 0)),
                   pl.BlockSpec((None, CONV_WIDTH - 1, RNN_WIDTH), lambda i, j: (i, 0, 0))],
        out_shape=[jax.ShapeDtypeStruct((t, RNN_WIDTH), BF16),
                   jax.ShapeDtypeStruct((b, 1, RNN_WIDTH), F32),
                   jax.ShapeDtypeStruct((b, CONV_WIDTH - 1, RNN_WIDTH), F32)],
        scratch_shapes=[pltpu.VMEM((tm + SUBLANES, RNN_WIDTH), F32),
                        pltpu.VMEM((tm, RNN_WIDTH), F32),
                        pltpu.VMEM((tm, RNN_WIDTH), F32),
                        pltpu.VMEM((1, RNN_WIDTH), F32)],
        compiler_params=_cparams(("parallel", "arbitrary"), 32),
        name="lru_prompt",
    )(lx, gates, conv_w, conv_b.reshape(1, -1), wg, ba.reshape(1, -1), bx.reshape(1, -1),
      lam.reshape(1, -1))


def _lru_step_kernel(x_ref, c0_ref, c1_ref, c2_ref, h0_ref, gate_ref, cw_ref, cb_ref, wg_ref,
                     ba_ref, bx_ref, lam_ref, o_ref, h_ref):
    xc = (cb_ref[...] + c0_ref[...] * cw_ref[0:1, :] + c1_ref[...] * cw_ref[1:2, :]
          + c2_ref[...] * cw_ref[2:3, :] + x_ref[...] * cw_ref[3:4, :])
    for n in range(RNN_BLOCKS):
        a, bt = _lru_coeffs(xc, wg_ref, ba_ref, bx_ref, lam_ref, n)
        sl = slice(n * RNN_BLOCK_W, (n + 1) * RNN_BLOCK_W)
        h = a * h0_ref[:, sl] + bt
        h_ref[:, sl] = h
        o_ref[:, sl] = (h * gate_ref[:, sl].astype(F32)).astype(o_ref.dtype)


def _lru_step(lx, conv_prev, h0, gate, conv_w, conv_b, wg, ba, bx, lam):
    bs = lx.shape[0]
    vm = pl.BlockSpec(memory_space=pltpu.VMEM)
    return pl.pallas_call(
        _lru_step_kernel,
        in_specs=[vm] * 12,
        out_specs=[vm, vm],
        out_shape=[jax.ShapeDtypeStruct((bs, RNN_WIDTH), BF16),
                   jax.ShapeDtypeStruct((bs, RNN_WIDTH), F32)],
        compiler_params=pltpu.CompilerParams(vmem_limit_bytes=32 * MIB),
        name="lru_step",
    )(lx, conv_prev[:, 0], conv_prev[:, 1], conv_prev[:, 2], h0, gate, conv_w,
      conv_b.reshape(1, -1), wg, ba.reshape(1, -1), bx.reshape(1, -1), lam.reshape(1, -1))


def _merge_kernel(ro_ref, do_ref, lo_ref, mg_ref, x_ref, p_ref, wr_ref, wd_ref, wl_ref, wo_ref,
                  wpg_ref, wp_ref, gn_ref, x_out_ref, hn_ref):
    d = D_MODEL
    tm = x_ref.shape[0]
    chunk = min(MERGE_CHUNK, tm)
    chunks = [slice(c * chunk, (c + 1) * chunk) for c in range(tm // chunk)]
    dot = functools.partial(jnp.dot, preferred_element_type=F32)
    merged = [(mg_ref[rs, 0:d].astype(F32) * dot(ro_ref[rs, :], wr_ref[...])
               + mg_ref[rs, d:2 * d].astype(F32) * dot(do_ref[rs, :], wd_ref[...])
               + mg_ref[rs, 2 * d:3 * d].astype(F32) * dot(lo_ref[rs, :], wl_ref[...])).astype(BF16)
              for rs in chunks]
    ple = [dot(p_ref[rs, :].astype(BF16), wp_ref[...]) for rs in chunks]
    x1 = [x_ref[rs, :] + dot(m, wo_ref[...]) for rs, m in zip(chunks, merged)]
    gate = [dot(v.astype(BF16), wpg_ref[...]) for v in x1]
    for rs, v, g, e in zip(chunks, x1, gate, ple):
        x2 = v + _sigmoid(g) * e
        x_out_ref[rs, :] = x2
        y = x2 * lax.rsqrt(jnp.mean(x2 * x2, axis=-1, keepdims=True) + NORM_EPS)
        hn_ref[rs, :] = (y * gn_ref[...]).astype(hn_ref.dtype)


MERGE_CHUNK = 256


def _merge(ro, do, lo, mg, x, p_all, layer, wr, wd, wl, wo, wpg, wp, g_next, hn_dtype, tm):
    t = x.shape[0]
    d = D_MODEL
    row = lambda wdt: pl.BlockSpec((tm, wdt), lambda i: (i, 0))
    full = lambda a: pl.BlockSpec(a.shape, lambda i: (0, 0), pipeline_mode=pl.Buffered(1))
    return pl.pallas_call(
        _merge_kernel,
        grid=(t // tm,),
        in_specs=[row(RET_V_W), row(DIFF_OUT_W), row(RNN_WIDTH), row(3 * d), row(d),
                  pl.BlockSpec((None, tm, D_PLE), lambda i: (layer, i, 0)),
                  full(wr), full(wd), full(wl), full(wo), full(wpg), full(wp),
                  pl.BlockSpec((1, d), lambda i: (0, 0))],
        out_specs=[row(d), row(d)],
        out_shape=[jax.ShapeDtypeStruct((t, d), F32), jax.ShapeDtypeStruct((t, d), hn_dtype)],
        compiler_params=_cparams(("parallel",), 56),
        name="merge",
    )(ro, do, lo, mg, x, p_all, wr, wd, wl, wo, wpg, wp, g_next.reshape(1, d))


def _lambda_init(layer):
    return 0.8 - 0.6 * math.exp(-0.3 * layer)


def _rope_tables(pos):
    half = RET_DK // 2
    inv = ROPE_BASE ** (-jnp.arange(half, dtype=F32) / half)
    ang = pos.astype(F32)[:, None] * inv[None, :]
    cos = jnp.cos(ang)
    sin = jnp.sin(ang)
    return jnp.concatenate([cos, cos], axis=-1), jnp.concatenate([-sin, sin], axis=-1)


def kernel(x_prompt, x_sample, cache_k, cache_v, state_ret, state_lru, state_conv, page_table, p_prompt, p_sample, rel_bias, norm_g, w_in, lambda_q1, lambda_k1, lambda_q2, lambda_k2, subln_g, conv_w, conv_b, gate_a_w, gate_a_b, gate_x_w, gate_x_b, lru_lambda, w_ret_out, w_diff_out, w_lru_out, w_o, w_ple, w_ple_gate, final_norm_g):
    bp, sp, d = x_prompt.shape
    bs, ss, _ = x_sample.shape
    assert d == D_MODEL and ss == 1
    depth = w_in.shape[0]
    n_pages, page = page_table.shape[1], cache_k.shape[2]
    past = n_pages * page
    tp = bp * sp
    blk = min(256, sp)
    tm_p = min(1024, sp)
    tm_qkv = min(512, sp)
    tm_lru = min(256, sp)
    tm_merge = min(512, sp)
    p_prompt_all = p_prompt.reshape(depth, tp, D_PLE)
    p_sample_all = p_sample.reshape(depth, bs, D_PLE)

    cos_p, sin_p = _rope_tables(jnp.arange(sp, dtype=jnp.int32))
    cos_s, sin_s = _rope_tables(jnp.full((bs,), past, dtype=jnp.int32))
    bias_p = _bias_prompt(rel_bias, blk)
    bias_s, bias_s_new = _bias_sample(rel_bias, past)

    w_in_b = w_in.astype(BF16)
    wg_all = jnp.concatenate([gate_a_w, gate_x_w], axis=-1).astype(BF16)
    xp = x_prompt.reshape(tp, d)
    xs = x_sample.reshape(bs, d)
    hn_p = _rmsnorm(xp, norm_g[0], BF16, tm_p)
    hn_s = _rmsnorm(xs, norm_g[0], BF16, bs)

    outs = {k: [] for k in ("kp", "vp", "ks", "vs", "rp", "rs", "lp", "ls", "cp", "cs")}
    for l in range(depth):
        wg = wg_all[l]
        wr, wd, wl = w_ret_out[l].astype(BF16), w_diff_out[l].astype(BF16), w_lru_out[l].astype(BF16)
        wo, wpg, wp = w_o[l].astype(BF16), w_ple_gate[l].astype(BF16), w_ple[l].astype(BF16)
        lam0 = _lambda_init(l)
        lam_rows = jnp.stack([lambda_q1[l], lambda_k1[l], lambda_q2[l], lambda_k2[l]], axis=0)
        last = l == depth - 1
        g_next = final_norm_g if last else norm_g[l + 1]
        hn_dtype = F32 if last else BF16

        qk = _proj(hn_p, w_in_b, l, COL_ROT, "rotary", tm_p, tables=(cos_p, sin_p))
        rv = _proj(hn_p, w_in_b, l, COL_RV, "plain", tm_p)
        gates = _proj(hn_p, w_in_b, l, COL_SILU, "silu", tm_p)
        kf, vf, kb, qt, vt = _proj_qkv_t(hn_p, w_in_b, l, tm_qkv, blk)
        lx = _proj(hn_p, w_in_b, l, COL_LX, "plain", tm_p, out_dtype=F32)
        mg = _proj(hn_p, w_in_b, l, COL_MG, "sigmoid", tm_p)
        ro, s_fin = _retention_prompt(qk, rv, gates, bp, sp)
        do = _attention_prompt(qt, kb, vt, bias_p, gates, subln_g[l], lam_rows, lam0, bp, sp, blk)
        lo, h_last, conv_new = _lru_prompt(lx, gates, conv_w[l], conv_b[l], wg, gate_a_b[l],
                                           gate_x_b[l], lru_lambda[l], bp, sp, tm_lru)
        xp, hn_p = _merge(ro, do, lo, mg, xp, p_prompt_all, l, wr, wd, wl, wo, wpg, wp,
                          g_next, hn_dtype, tm_merge)
        outs["kp"].append(kf.reshape(bp, sp, DIFF_KV_HEADS, 2 * DIFF_DH))
        outs["vp"].append(vf.reshape(bp, sp, DIFF_KV_HEADS, DIFF_DV))
        outs["rp"].append(s_fin)
        outs["lp"].append(h_last.reshape(bp, RNN_WIDTH))
        outs["cp"].append(conv_new)

        qk = _proj(hn_s, w_in_b, l, COL_ROT, "rotary", bs, tables=(cos_s, sin_s))
        rvdq = _proj(hn_s, w_in_b, l, COL_RV_DQ, "plain", bs)
        gates = _proj(hn_s, w_in_b, l, COL_SILU, "silu", bs)
        kf, vf, kb, vb = _proj(hn_s, w_in_b, l, (COL_KV,), "kv", bs)
        lx = _proj(hn_s, w_in_b, l, COL_LX, "plain", bs, out_dtype=F32)
        mg = _proj(hn_s, w_in_b, l, COL_MG, "sigmoid", bs)
        ro, s_new = _retention_step(qk, rvdq, gates, state_ret, l)
        do = _attention_decode(rvdq[:, RET_V_W:], kb, vb, gates[:, RET_V_W:RET_V_W + DIFF_OUT_W],
                               cache_k, cache_v, page_table, bias_s, bias_s_new, subln_g[l],
                               lam_rows, lam0, l)
        conv_prev = state_conv[:, l]
        lo, h_new = _lru_step(lx, conv_prev, state_lru[:, l], gates[:, 2 * RET_V_W:], conv_w[l],
                              conv_b[l], wg, gate_a_b[l], gate_x_b[l], lru_lambda[l])
        xs, hn_s = _merge(ro, do, lo, mg, xs, p_sample_all, l, wr, wd, wl, wo, wpg, wp,
                          g_next, hn_dtype, bs)
        outs["ks"].append(kf.reshape(bs, 1, DIFF_KV_HEADS, 2 * DIFF_DH))
        outs["vs"].append(vf.reshape(bs, 1, DIFF_KV_HEADS, DIFF_DV))
        outs["rs"].append(s_new)
        outs["ls"].append(h_new)
        outs["cs"].append(jnp.concatenate([conv_prev[:, 1:], lx[:, None, :]], axis=1))

    st = lambda key: jnp.stack(outs[key], axis=1)
    return (hn_p.reshape(bp, sp, d), hn_s.reshape(bs, 1, d), st("kp"), st("vp"), st("ks"), st("vs"),
            st("rp"), st("rs"), st("lp"), st("ls"), st("cp"), st("cs"))
```

```python
import functools
import math

import numpy as np
import jax
import jax.numpy as jnp
from jax import lax
from jax.experimental import pallas as pl
from jax.experimental.pallas import tpu as pltpu

F32 = jnp.float32
BF16 = jnp.bfloat16

D_MODEL = 1024
D_PLE = 256
NORM_EPS = 1e-6
RET_HEADS = 4
RET_DK = 128
RET_DV = 256
RET_CHUNK = 128
ROPE_BASE = 10000.0
DIFF_HEADS = 8
DIFF_KV_HEADS = 4
DIFF_GROUP = DIFF_HEADS // DIFF_KV_HEADS
DIFF_DH = 64
DIFF_DV = 2 * DIFF_DH
REL_BUCKETS = 32
REL_MAX_DIST = 128
RNN_WIDTH = 1024
RNN_BLOCKS = 8
RNN_BLOCK_W = RNN_WIDTH // RNN_BLOCKS
CONV_WIDTH = 4
LRU_C = 8.0

RET_QK_W = RET_HEADS * RET_DK
RET_V_W = RET_HEADS * RET_DV
DIFF_Q_W = DIFF_HEADS * 2 * DIFF_DH
DIFF_K_W = DIFF_KV_HEADS * 2 * DIFF_DH
DIFF_V_W = DIFF_KV_HEADS * DIFF_DV
DIFF_OUT_W = DIFF_HEADS * DIFF_DV
IN_SPLITS = (RET_QK_W, RET_QK_W, RET_V_W, RET_V_W,
             DIFF_Q_W, DIFF_K_W, DIFF_V_W, DIFF_OUT_W,
             RNN_WIDTH, RNN_WIDTH, 3 * D_MODEL)
IN_OFFS = tuple(int(v) for v in np.cumsum((0,) + IN_SPLITS))

LANES = 128
SUBLANES = 8
MASK_VALUE = -1e30
LOG2E = math.log2(math.e)
VT_ONES = 2 * SUBLANES
VT_ROWS = DIFF_DV + VT_ONES
BIAS_DIAG, BIAS_PREV, BIAS_ZERO = 0, 1, 2
MIB = 1024 * 1024


def _cparams(sem, vmem_mib):
    return pltpu.CompilerParams(dimension_semantics=sem, vmem_limit_bytes=int(vmem_mib * MIB))


def _rmsnorm_kernel(x_ref, g_ref, o_ref):
    x = x_ref[...]
    y = x * lax.rsqrt(jnp.mean(x * x, axis=-1, keepdims=True) + NORM_EPS)
    o_ref[...] = (y * g_ref[...]).astype(o_ref.dtype)


def _rmsnorm(x, g, out_dtype, tm):
    t, d = x.shape
    return pl.pallas_call(
        _rmsnorm_kernel,
        grid=(t // tm,),
        in_specs=[pl.BlockSpec((tm, d), lambda i: (i, 0)),
                  pl.BlockSpec((1, d), lambda i: (0, 0))],
        out_specs=pl.BlockSpec((tm, d), lambda i: (i, 0)),
        out_shape=jax.ShapeDtypeStruct((t, d), out_dtype),
        compiler_params=_cparams(("parallel",), 32),
        name="rmsnorm",
    )(x, g.reshape(1, d))


def _proj_qkv_t_kernel(x_ref, wq_ref, wkv_ref, kf_ref, vf_ref, kb_ref, qt_ref, vt_ref):
    blk = qt_ref.shape[-1]

    def matmuls(j):
        x = x_ref[j * blk:(j + 1) * blk, :]
        return (jnp.dot(x, wq_ref[...], preferred_element_type=F32),
                jnp.dot(x, wkv_ref[...], preferred_element_type=F32))

    def epilogue(j, q, kv):
        rows = slice(j * blk, (j + 1) * blk)
        k = kv[:, :DIFF_K_W]
        v = kv[:, DIFF_K_W:]
        for h in range(DIFF_KV_HEADS):
            dst = pl.ds(j * blk * DIFF_KV_HEADS + h, blk, stride=DIFF_KV_HEADS)
            kf_ref[dst, :] = k[:, h * 2 * DIFF_DH:(h + 1) * 2 * DIFF_DH]
            vf_ref[dst, :] = v[:, h * DIFF_DV:(h + 1) * DIFF_DV]
        kb_ref[rows, :] = k.astype(BF16)
        qt_ref[j] = (q * (DIFF_DH ** -0.5 * LOG2E)).T.astype(BF16)
        vt = v.T.astype(BF16)
        for h in range(DIFF_KV_HEADS):
            vt_ref[j, h * VT_ROWS:h * VT_ROWS + DIFF_DV, :] = vt[h * DIFF_DV:(h + 1) * DIFF_DV]
            vt_ref[j, h * VT_ROWS + DIFF_DV:(h + 1) * VT_ROWS, :] = jnp.ones((VT_ONES, blk), BF16)

    _pipeline_rows(qt_ref.shape[0], matmuls, epilogue)


def _pipeline_rows(n_chunks, matmuls, epilogue):
    pending = {}
    for c in range(n_chunks + 1):
        if c < n_chunks:
            pending[c] = matmuls(c)
        if c >= 1:
            epilogue(c - 1, *pending.pop(c - 1))


def _sigmoid(x):
    return 0.5 * jnp.tanh(0.5 * x) + 0.5


def _proj_kernel(x_ref, w_ref, *refs, kind, chunk):
    tm = x_ref.shape[0]

    def matmuls(c):
        return (jnp.dot(x_ref[c * chunk:(c + 1) * chunk, :], w_ref[...], preferred_element_type=F32),)

    def epilogue(c, acc):
        rows = slice(c * chunk, (c + 1) * chunk)
        if kind == "plain":
            (o_ref,) = refs
            o_ref[rows, :] = acc.astype(o_ref.dtype)
        elif kind == "silu":
            (o_ref,) = refs
            o_ref[rows, :] = (acc * _sigmoid(acc)).astype(o_ref.dtype)
        elif kind == "sigmoid":
            (o_ref,) = refs
            o_ref[rows, :] = _sigmoid(acc).astype(o_ref.dtype)
        elif kind == "rotary":
            cos_ref, sin_ref, o_ref = refs
            cos = cos_ref[rows, :]
            sin = sin_ref[rows, :]
            for h in range(2 * RET_HEADS):
                z = acc[:, h * RET_DK:(h + 1) * RET_DK]
                r = z * cos + pltpu.roll(z, RET_DK // 2, 1) * sin
                if h >= RET_HEADS:
                    r = r * (RET_DK ** -0.5)
                o_ref[rows, h * RET_DK:(h + 1) * RET_DK] = r.astype(o_ref.dtype)
        elif kind == "kv":
            kf_ref, vf_ref, kb_ref, vb_ref = refs
            k = acc[:, :DIFF_K_W]
            v = acc[:, DIFF_K_W:]
            kf_ref[rows, :] = k
            vf_ref[rows, :] = v
            kb_ref[rows, :] = k.astype(BF16)
            vb_ref[rows, :] = v.astype(BF16)
        else:
            raise ValueError(kind)

    _pipeline_rows(tm // chunk, matmuls, epilogue)


PROJ_TN = 1024
PROJ_CHUNK = 256
COL_ROT = (0,)
COL_RV = (1,)
COL_RV_DQ = (1, 3)
COL_SILU = (2, 5, 7)
COL_DQ = 3
COL_KV = 4
COL_LX = (6,)
COL_MG = (8, 9, 10)
assert IN_OFFS[2] == PROJ_TN and IN_OFFS[5] == COL_KV * PROJ_TN and IN_OFFS[10] == COL_MG[0] * PROJ_TN


def _col_index(cols, j):
    if len(cols) == 1:
        return cols[0]
    step = cols[1] - cols[0]
    idx = cols[0] + step * j
    if cols[-1] != cols[0] + step * (len(cols) - 1):
        idx = idx - j // (len(cols) - 1)
    return idx


def _proj_qkv_t(hn, w_all, layer, tm, blk):
    t, d = hn.shape
    nj = tm // blk
    half = pl.BlockSpec((tm, DIFF_K_W), lambda i: (i, 0))
    per_head = pl.BlockSpec((tm * DIFF_KV_HEADS, DIFF_DV), lambda i: (i, 0))
    return pl.pallas_call(
        _proj_qkv_t_kernel,
        grid=(t // tm,),
        in_specs=[pl.BlockSpec((tm, d), lambda i: (i, 0)),
                  pl.BlockSpec((None, d, PROJ_TN), lambda i: (layer, 0, COL_DQ)),
                  pl.BlockSpec((None, d, PROJ_TN), lambda i: (layer, 0, COL_KV))],
        out_specs=[per_head, per_head, half,
                   pl.BlockSpec((nj, DIFF_Q_W, blk), lambda i: (i, 0, 0)),
                   pl.BlockSpec((nj, DIFF_KV_HEADS * VT_ROWS, blk), lambda i: (i, 0, 0))],
        out_shape=[jax.ShapeDtypeStruct((t * DIFF_KV_HEADS, 2 * DIFF_DH), F32),
                   jax.ShapeDtypeStruct((t * DIFF_KV_HEADS, DIFF_DV), F32),
                   jax.ShapeDtypeStruct((t, DIFF_K_W), BF16),
                   jax.ShapeDtypeStruct((t // blk, DIFF_Q_W, blk), BF16),
                   jax.ShapeDtypeStruct((t // blk, DIFF_KV_HEADS * VT_ROWS, blk), BF16)],
        compiler_params=_cparams(("parallel",), 48),
        name="proj_qkv_t",
    )(hn, w_all, w_all)


def _proj(hn, w_all, layer, cols, kind, tm, out_dtype=BF16, tables=None):
    t, d = hn.shape
    tn = PROJ_TN
    n = tn * len(cols)
    grid = (len(cols), t // tm)
    in_specs = [pl.BlockSpec((tm, d), lambda j, i: (i, 0)),
                pl.BlockSpec((None, d, tn), lambda j, i: (layer, 0, _col_index(cols, j)))]
    args = [hn, w_all]
    if kind == "rotary":
        cos, sin = tables
        nb = cos.shape[0] // tm
        spec = pl.BlockSpec((tm, RET_DK), lambda j, i: (i % nb, 0))
        in_specs += [spec, spec]
        args += [cos, sin]
    if kind == "kv":
        half = pl.BlockSpec((tm, DIFF_K_W), lambda j, i: (i, 0))
        out_specs = [half, half, half, half]
        out_shape = [jax.ShapeDtypeStruct((t, DIFF_K_W), F32),
                     jax.ShapeDtypeStruct((t, DIFF_V_W), F32),
                     jax.ShapeDtypeStruct((t, DIFF_K_W), BF16),
                     jax.ShapeDtypeStruct((t, DIFF_V_W), BF16)]
    else:
        out_specs = pl.BlockSpec((tm, tn), lambda j, i: (i, j))
        out_shape = jax.ShapeDtypeStruct((t, n), out_dtype)
    return pl.pallas_call(
        functools.partial(_proj_kernel, kind=kind, chunk=min(PROJ_CHUNK, tm)),
        grid=grid, in_specs=in_specs, out_specs=out_specs, out_shape=out_shape,
        compiler_params=_cparams(("parallel", "parallel"), 40),
        name="proj_" + kind,
    )(*args)


def _retention_kernel(qk_ref, v_ref, gate_ref, dmask_ref, qdec_ref, kdec_ref,
                      o_ref, sfin_ref, state_ref, *, chunk_dec):
    c = pl.program_id(1)

    @pl.when(c == 0)
    def _():
        state_ref[...] = jnp.zeros_like(state_ref)

    heads = range(RET_HEADS)
    dot = functools.partial(jnp.dot, preferred_element_type=F32)
    q = [qk_ref[:, h * RET_DK:(h + 1) * RET_DK] for h in heads]
    k = [qk_ref[:, RET_QK_W + h * RET_DK:RET_QK_W + (h + 1) * RET_DK] for h in heads]
    v = [v_ref[:, h * RET_DV:(h + 1) * RET_DV] for h in heads]
    s = [lax.dot_general(q[h], k[h], (((1,), (1,)), ((), ())), preferred_element_type=F32)
         for h in heads]
    cross = [dot(q[h], state_ref[h].astype(BF16)) for h in heads]
    kd = [(k[h].astype(F32) * kdec_ref[:, h:h + 1]).astype(BF16) for h in heads]
    kv = [lax.dot_general(kd[h], v[h], (((0,), (0,)), ((), ())), preferred_element_type=F32)
          for h in heads]
    inner = [dot((s[h] * dmask_ref[h]).astype(BF16), v[h]) for h in heads]
    for h in heads:
        state_ref[h] = state_ref[h] * chunk_dec[h] + kv[h]
        o = inner[h] + cross[h] * qdec_ref[:, h:h + 1]
        o = o * lax.rsqrt(jnp.mean(o * o, axis=-1, keepdims=True) + NORM_EPS)
        g = gate_ref[:, h * RET_DV:(h + 1) * RET_DV].astype(F32)
        o_ref[:, h * RET_DV:(h + 1) * RET_DV] = (o * g).astype(o_ref.dtype)

    @pl.when(c == pl.num_programs(1) - 1)
    def _():
        sfin_ref[...] = state_ref[...]


def _retention_prompt(qk, rvdq, gates, b, s):
    c = math.gcd(s, RET_CHUNK)
    nc = s // c
    log_g = np.log1p(-(2.0 ** (-5.0 - np.arange(RET_HEADS, dtype=np.float64))))
    idx = np.arange(c, dtype=np.float64)
    dist = idx[:, None] - idx[None, :]
    dmask = np.where(dist[None] >= 0, np.exp(np.maximum(dist, 0.0)[None] * log_g[:, None, None]), 0.0)
    qdec = np.exp((idx + 1.0)[:, None] * log_g[None, :])
    kdec = np.exp((c - 1.0 - idx)[:, None] * log_g[None, :])
    chunk_dec = tuple(float(x) for x in np.exp(c * log_g).astype(np.float32))
    t = b * s
    return pl.pallas_call(
        functools.partial(_retention_kernel, chunk_dec=chunk_dec),
        grid=(b, nc),
        in_specs=[pl.BlockSpec((c, 2 * RET_QK_W), lambda i, j: (i * nc + j, 0)),
                  pl.BlockSpec((c, RET_V_W), lambda i, j: (i * nc + j, 0)),
                  pl.BlockSpec((c, RET_V_W), lambda i, j: (i * nc + j, 0)),
                  pl.BlockSpec((RET_HEADS, c, c), lambda i, j: (0, 0, 0)),
                  pl.BlockSpec((c, RET_HEADS), lambda i, j: (0, 0)),
                  pl.BlockSpec((c, RET_HEADS), lambda i, j: (0, 0))],
        out_specs=[pl.BlockSpec((c, RET_V_W), lambda i, j: (i * nc + j, 0)),
                   pl.BlockSpec((None, RET_HEADS, RET_DK, RET_DV), lambda i, j: (i, 0, 0, 0))],
        out_shape=[jax.ShapeDtypeStruct((t, RET_V_W), BF16),
                   jax.ShapeDtypeStruct((b, RET_HEADS, RET_DK, RET_DV), F32)],
        scratch_shapes=[pltpu.VMEM((RET_HEADS, RET_DK, RET_DV), F32)],
        compiler_params=_cparams(("parallel", "arbitrary"), 32),
        name="retention_prompt",
    )(qk, rvdq, gates, jnp.asarray(dmask, F32), jnp.asarray(qdec, F32), jnp.asarray(kdec, F32))


def _retention_step_kernel(qk_ref, v_ref, gate_ref, s_ref, o_ref, snew_ref, *, decay):
    sb = qk_ref.shape[0]
    rowid = lax.broadcasted_iota(jnp.int32, (sb, 1), 0)

    def per_sequence(b, outs):
        new_outs = []
        for h in range(RET_HEADS):
            q = qk_ref[:, h * RET_DK:(h + 1) * RET_DK]
            k = qk_ref[:, RET_QK_W + h * RET_DK:RET_QK_W + (h + 1) * RET_DK]
            v = v_ref[:, h * RET_DV:(h + 1) * RET_DV]
            st = s_ref[b, h]
            qs = jnp.dot(q, st.astype(BF16), preferred_element_type=F32)
            new_outs.append(outs[h] + jnp.where(rowid == b, decay[h] * qs, 0.0))
            kb = jnp.where(rowid == b, k, jnp.zeros_like(k))
            snew_ref[b, h] = st * decay[h] + lax.dot_general(
                kb, v, (((0,), (0,)), ((), ())), preferred_element_type=F32)
        return tuple(new_outs)

    outs = []
    for h in range(RET_HEADS):
        q = qk_ref[:, h * RET_DK:(h + 1) * RET_DK].astype(F32)
        k = qk_ref[:, RET_QK_W + h * RET_DK:RET_QK_W + (h + 1) * RET_DK].astype(F32)
        v = v_ref[:, h * RET_DV:(h + 1) * RET_DV].astype(F32)
        outs.append(jnp.sum(q * k, axis=1, keepdims=True) * v)
    outs = lax.fori_loop(0, sb, per_sequence, tuple(outs))
    for h in range(RET_HEADS):
        o = outs[h]
        o = o * lax.rsqrt(jnp.mean(o * o, axis=-1, keepdims=True) + NORM_EPS)
        g = gate_ref[:, h * RET_DV:(h + 1) * RET_DV].astype(F32)
        o_ref[:, h * RET_DV:(h + 1) * RET_DV] = (o * g).astype(o_ref.dtype)


def _retention_step(qk, rvdq, gates, state_ret, layer):
    bs = qk.shape[0]
    sb = min(2 * SUBLANES, bs)
    log_g = np.log1p(-(2.0 ** (-5.0 - np.arange(RET_HEADS, dtype=np.float64))))
    decay = tuple(float(x) for x in np.exp(log_g).astype(np.float32))
    return pl.pallas_call(
        functools.partial(_retention_step_kernel, decay=decay),
        grid=(bs // sb,),
        in_specs=[pl.BlockSpec((sb, 2 * RET_QK_W), lambda i: (i, 0)),
                  pl.BlockSpec((sb, RET_V_W), lambda i: (i, 0)),
                  pl.BlockSpec((sb, RET_V_W), lambda i: (i, 0)),
                  pl.BlockSpec((sb, None, RET_HEADS, RET_DK, RET_DV),
                               lambda i: (i, layer, 0, 0, 0))],
        out_specs=[pl.BlockSpec((sb, RET_V_W), lambda i: (i, 0)),
                   pl.BlockSpec((sb, RET_HEADS, RET_DK, RET_DV), lambda i: (i, 0, 0, 0))],
        out_shape=[jax.ShapeDtypeStruct((bs, RET_V_W), BF16),
                   jax.ShapeDtypeStruct((bs, RET_HEADS, RET_DK, RET_DV), F32)],
        compiler_params=_cparams(("parallel",), 48),
        name="retention_step",
    )(qk, rvdq, gates, state_ret)


def _t5_bucket(rel):
    n = jnp.maximum(rel, 0)
    max_exact = REL_BUCKETS // 2
    nf = jnp.maximum(n.astype(F32), 1.0)
    large = max_exact + (jnp.log(nf / max_exact) / math.log(REL_MAX_DIST / max_exact)
                         * (REL_BUCKETS - max_exact)).astype(jnp.int32)
    large = jnp.minimum(large, REL_BUCKETS - 1)
    return jnp.where(n < max_exact, n, large)


def _bias_lookup(tab_ref, bucket, head):
    out = jnp.zeros(bucket.shape, F32)
    far = tab_ref[REL_BUCKETS - 1, head]
    for bkt in range(REL_BUCKETS - 1):
        out = jnp.where(bucket == bkt, tab_ref[bkt, head] - far, out)
    return out


def _bias_prompt_kernel(tab_ref, o_ref, *, blk):
    i = lax.broadcasted_iota(jnp.int32, (blk, blk), 1)
    j = lax.broadcasted_iota(jnp.int32, (blk, blk), 0)
    for which in (BIAS_DIAG, BIAS_PREV):
        rel = i - j + which * blk
        bucket = _t5_bucket(rel)
        for h in range(DIFF_HEADS):
            o_ref[which, h] = jnp.where(rel >= 0, _bias_lookup(tab_ref, bucket, h) * LOG2E, MASK_VALUE)
    o_ref[BIAS_ZERO] = jnp.zeros(o_ref.shape[1:], F32)


def _bias_prompt(rel_bias, blk):
    return pl.pallas_call(
        functools.partial(_bias_prompt_kernel, blk=blk),
        in_specs=[pl.BlockSpec(memory_space=pltpu.SMEM)],
        out_specs=pl.BlockSpec(memory_space=pltpu.VMEM),
        out_shape=jax.ShapeDtypeStruct((3, DIFF_HEADS, blk, blk), F32),
        compiler_params=pltpu.CompilerParams(vmem_limit_bytes=32 * MIB),
        name="bias_prompt",
    )(rel_bias)


def _bias_sample_kernel(tab_ref, o_ref, onew_ref, *, past, width):
    rows = 2 * DIFF_HEADS
    r = lax.broadcasted_iota(jnp.int32, (rows, width), 0)
    lane = lax.broadcasted_iota(jnp.int32, (rows, width), 1)
    head = r % DIFF_HEADS
    kvh = lane % DIFF_KV_HEADS
    pos = lane // DIFF_KV_HEADS
    bucket = _t5_bucket(past - pos)
    out = jnp.zeros((rows, width), F32)
    for h in range(DIFF_HEADS):
        out = jnp.where(head == h, _bias_lookup(tab_ref, bucket, h), out)
    o_ref[...] = jnp.where(kvh == head // DIFF_GROUP, out, MASK_VALUE)
    rn = lax.broadcasted_iota(jnp.int32, (rows, LANES), 0) % DIFF_HEADS
    new = jnp.zeros((rows, LANES), F32)
    zero_bucket = _t5_bucket(jnp.zeros((rows, LANES), jnp.int32))
    for h in range(DIFF_HEADS):
        new = jnp.where(rn == h, _bias_lookup(tab_ref, zero_bucket, h), new)
    onew_ref[...] = new


def _bias_sample(rel_bias, past):
    width = past * DIFF_KV_HEADS
    return pl.pallas_call(
        functools.partial(_bias_sample_kernel, past=past, width=width),
        in_specs=[pl.BlockSpec(memory_space=pltpu.SMEM)],
        out_specs=[pl.BlockSpec(memory_space=pltpu.VMEM), pl.BlockSpec(memory_space=pltpu.VMEM)],
        out_shape=[jax.ShapeDtypeStruct((2 * DIFF_HEADS, width), F32),
                   jax.ShapeDtypeStruct((2 * DIFF_HEADS, LANES), F32)],
        compiler_params=pltpu.CompilerParams(vmem_limit_bytes=32 * MIB),
        name="bias_sample",
    )(rel_bias)


def _lambda_full(lam_ref, lam0):
    a = jnp.sum(lam_ref[0:1, :] * lam_ref[1:2, :], axis=-1, keepdims=True)
    b = jnp.sum(lam_ref[2:3, :] * lam_ref[3:4, :], axis=-1, keepdims=True)
    return jnp.exp(a) - jnp.exp(b) + lam0


def _attn_kernel(qt_in_ref, k_ref, vt_ref, bias_ref, gate_ref, sg_ref, lam_ref, o_ref,
                 qt_ref, m_ref, acc_ref, *, blk, nq, lam0):
    pair = pl.program_id(2)
    tiles = (pair, nq - 1 - pair)
    n_chunks = 2 * DIFF_GROUP
    row = lax.broadcasted_iota(jnp.int32, (DIFF_DV, blk), 0)
    for slot in range(2):
        for g in range(DIFF_GROUP):
            qg = qt_in_ref[tiles[slot], g * DIFF_DV:(g + 1) * DIFF_DV, :]
            for m in range(2):
                keep = (row < DIFF_DH) if m == 0 else (row >= DIFF_DH)
                c = m * DIFF_GROUP + g
                qt_ref[slot, :, c * blk:(c + 1) * blk] = jnp.where(keep, qg, jnp.zeros_like(qg))
    m_ref[...] = jnp.full(m_ref.shape, MASK_VALUE, F32)
    acc_ref[...] = jnp.zeros(acc_ref.shape, F32)

    def run_blocks(blocks):
        s_lag = 3
        kbs =[k_ref[pl.ds(pl.multiple_of(ki * blk, blk), blk), :] for _, ki, _ in blocks]
        vts = [vt_ref[ki] for _, ki, _ in blocks]
        units = [(bi, c) for bi in range(len(blocks)) for c in range(n_chunks)]
        n_units = len(units)

        def scores(u):
            bi, c = units[u]
            slot, _, bias = blocks[bi]
            s = jnp.dot(kbs[bi], qt_ref[slot, :, c * blk:(c + 1) * blk], preferred_element_type=F32)
            if bias is not None:
                s = s + bias_ref[bias, c % DIFF_GROUP]
            return s

        def softmax(u, s):
            bi, c = units[u]
            slot = blocks[bi][0]
            cs = slice(c * blk, (c + 1) * blk)
            m_prev = m_ref[slot, :, cs]
            m_new = jnp.maximum(m_prev, jnp.max(s, axis=0, keepdims=True))
            p = jnp.exp2(s - m_new)
            alpha = jnp.exp2(m_prev - m_new)
            m_ref[slot, :, cs] = m_new
            return p.astype(BF16), alpha

        def accumulate(u, p, alpha):
            bi, c = units[u]
            slot = blocks[bi][0]
            cs = slice(c * blk, (c + 1) * blk)
            acc_ref[slot, :, cs] = acc_ref[slot, :, cs] * alpha + jnp.dot(
                vts[bi], p, preferred_element_type=F32)

        s_vals, p_vals = {}, {}
        for t in range(n_units + s_lag + 1):
            if t < n_units:
                s_vals[t] = scores(t)
            if 0 <= t - s_lag < n_units:
                p_vals[t - s_lag] = softmax(t - s_lag, s_vals.pop(t - s_lag))
            if 0 <= t - s_lag - 1 < n_units:
                accumulate(t - s_lag - 1, *p_vals.pop(t - s_lag - 1))

    n_b = jnp.minimum(nq - 2 - pair, nq - 3)
    blocks = []
    for j in range(nq - 3):
        if j < nq // 2 - 1:
            blocks.append((1, j, None))
        else:
            in_b = j < n_b
            blocks.append((jnp.where(in_b, 1, 0), jnp.where(in_b, j, j - n_b), None))
    first = pair == 0
    blocks.append((jnp.where(first, 1, 0), jnp.where(first, nq - 3, pair - 1),
                   jnp.where(first, BIAS_ZERO, BIAS_PREV)))
    blocks.append((0, pair, BIAS_DIAG))
    blocks.append((1, nq - 2 - pair, BIAS_PREV))
    blocks.append((1, nq - 1 - pair, BIAS_DIAG))
    run_blocks(blocks)

    lam = _lambda_full(lam_ref, lam0)
    for slot in range(2):
        rows = pl.ds(pl.multiple_of(tiles[slot] * blk, blk), blk)
        a = acc_ref[slot, 0:DIFF_DV, :] / acc_ref[slot, DIFF_DV:DIFF_DV + 1, :]
        for g in range(DIFF_GROUP):
            ot = a[:, g * blk:(g + 1) * blk] - lam * a[:, (DIFF_GROUP + g) * blk:(DIFF_GROUP + g + 1) * blk]
            ot = ot * lax.rsqrt(jnp.mean(ot * ot, axis=0, keepdims=True) + NORM_EPS)
            ot = ot * sg_ref[...] * (1.0 - lam0)
            gate = gate_ref[rows, g * DIFF_DV:(g + 1) * DIFF_DV].astype(F32)
            o_ref[rows, g * DIFF_DV:(g + 1) * DIFF_DV] = (ot.T * gate).astype(o_ref.dtype)


def _attention_prompt(qt, kb, vt, bias, gates, subln_g, lam_rows, lam0, b, s, blk):
    nq = s // blk
    assert nq % 2 == 0 and nq >= 4
    t = b * s
    gcol0 = RET_V_W // (DIFF_GROUP * DIFF_DV)
    kvw = 2 * DIFF_DH
    per_seq = lambda i, h, p: (i, h)
    return pl.pallas_call(
        functools.partial(_attn_kernel, blk=blk, nq=nq, lam0=lam0),
        grid=(b, DIFF_KV_HEADS, nq // 2),
        in_specs=[pl.BlockSpec((nq, DIFF_GROUP * DIFF_DV, blk), lambda i, h, p: (i, h, 0)),
                  pl.BlockSpec((s, kvw), per_seq),
                  pl.BlockSpec((nq, VT_ROWS, blk), lambda i, h, p: (i, h, 0)),
                  pl.BlockSpec((3, DIFF_GROUP, blk, blk), lambda i, h, p: (0, h, 0, 0)),
                  pl.BlockSpec((s, DIFF_GROUP * DIFF_DV), lambda i, h, p: (i, gcol0 + h)),
                  pl.BlockSpec((DIFF_DV, 1), lambda i, h, p: (0, 0)),
                  pl.BlockSpec((4, DIFF_DH), lambda i, h, p: (0, 0))],
        out_specs=pl.BlockSpec((s, DIFF_GROUP * DIFF_DV), per_seq),
        out_shape=jax.ShapeDtypeStruct((t, DIFF_OUT_W), BF16),
        scratch_shapes=[pltpu.VMEM((2, DIFF_DV, 4 * blk), BF16),
                        pltpu.VMEM((2, 1, 4 * blk), F32),
                        pltpu.VMEM((2, VT_ROWS, 4 * blk), F32)],
        compiler_params=_cparams(("parallel", "parallel", "arbitrary"), 48),
        name="attention_prompt",
    )(qt, kb, vt, bias, gates, subln_g.reshape(DIFF_DV, 1), lam_rows)


def _attn_decode_kernel(pt_ref, *refs, n_pages, rows_per_page, lam0):
    del pt_ref
    k_refs = refs[:n_pages]
    v_refs = refs[n_pages:2 * n_pages]
    (qt_ref, kn_ref, vn_ref, bias_ref, bnew_ref, gate_ref, sg_ref, lam_ref,
     o_ref, s_ref) = refs[2 * n_pages:]
    qt = qt_ref[...]
    w = rows_per_page
    for p in range(n_pages):
        kp = k_refs[p][...].astype(BF16)
        sp = lax.dot_general(qt, kp, (((1,), (1,)), ((), ())), preferred_element_type=F32)
        s_ref[:, p * w:(p + 1) * w] = sp + bias_ref[:, p * w:(p + 1) * w]
    s = s_ref[...]
    s_new = jnp.sum(qt.astype(F32) * kn_ref[...].astype(F32), axis=1, keepdims=True)
    s_new = s_new + bnew_ref[:, 0:1]
    m = jnp.maximum(jnp.max(s, axis=1, keepdims=True), s_new)
    e_new = jnp.exp(s_new - m)
    pr = jnp.exp(s - m)
    l = jnp.sum(pr, axis=1, keepdims=True) + e_new
    prb = pr.astype(BF16)
    acc = e_new * vn_ref[...].astype(F32)
    for p in range(n_pages):
        vp = v_refs[p][...].astype(BF16)
        acc = acc + jnp.dot(prb[:, p * w:(p + 1) * w], vp, preferred_element_type=F32)
    a = acc / l
    lam = _lambda_full(lam_ref, lam0)
    o = a[:DIFF_HEADS] - lam * a[DIFF_HEADS:]
    o = o * lax.rsqrt(jnp.mean(o * o, axis=-1, keepdims=True) + NORM_EPS)
    o = o * sg_ref[...] * (1.0 - lam0)
    o_ref[...] = o * gate_ref[...]


def _attention_decode(dq, kb_new, vb_new, gate, cache_k, cache_v, page_table, bias, bias_new,
                      subln_g, lam_rows, lam0, layer):
    bs = dq.shape[0]
    n_pool, depth, page = cache_k.shape[0], cache_k.shape[1], cache_k.shape[2]
    n_pages = page_table.shape[1]
    w = page * DIFF_KV_HEADS
    ck = cache_k.reshape(n_pool, depth, w, 2 * DIFF_DH)
    cv = cache_v.reshape(n_pool, depth, w, DIFF_DV)
    q = (dq * (DIFF_DH ** -0.5)).reshape(bs, DIFF_KV_HEADS, DIFF_GROUP, 2, DIFF_DH)
    q = jnp.transpose(q, (0, 3, 1, 2, 4)).reshape(bs, 2, DIFF_HEADS, DIFF_DH)
    z = jnp.zeros_like(q[:, 0])
    qt = jnp.concatenate([jnp.concatenate([q[:, 0], z], axis=-1),
                          jnp.concatenate([z, q[:, 1]], axis=-1)], axis=1)
    kn = jnp.tile(jnp.repeat(kb_new.reshape(bs, DIFF_KV_HEADS, 2 * DIFF_DH), DIFF_GROUP, axis=1), (1, 2, 1))
    vn = jnp.tile(jnp.repeat(vb_new.reshape(bs, DIFF_KV_HEADS, DIFF_DV), DIFF_GROUP, axis=1), (1, 2, 1))
    g3 = gate.astype(F32).reshape(bs, DIFF_HEADS, DIFF_DV)
    rows = 2 * DIFF_HEADS

    def page_spec(p):
        return pl.BlockSpec((None, None, w, 2 * DIFF_DH), lambda i, pt: (pt[i, p], layer, 0, 0))

    row_spec = pl.BlockSpec((None, rows, LANES), lambda i, pt: (i, 0, 0))
    in_specs = ([page_spec(p) for p in range(n_pages)] + [page_spec(p) for p in range(n_pages)]
                + [row_spec, row_spec, row_spec,
                   pl.BlockSpec((rows, n_pages * w), lambda i, pt: (0, 0)),
                   pl.BlockSpec((rows, LANES), lambda i, pt: (0, 0)),
                   pl.BlockSpec((None, DIFF_HEADS, DIFF_DV), lambda i, pt: (i, 0, 0)),
                   pl.BlockSpec((1, DIFF_DV), lambda i, pt: (0, 0)),
                   pl.BlockSpec((4, DIFF_DH), lambda i, pt: (0, 0))])
    out = pl.pallas_call(
        functools.partial(_attn_decode_kernel, n_pages=n_pages, rows_per_page=w, lam0=lam0),
        grid_spec=pltpu.PrefetchScalarGridSpec(
            num_scalar_prefetch=1, grid=(bs,), in_specs=in_specs,
            out_specs=pl.BlockSpec((None, DIFF_HEADS, DIFF_DV), lambda i, pt: (i, 0, 0)),
            scratch_shapes=[pltpu.VMEM((rows, n_pages * w), F32)]),
        out_shape=jax.ShapeDtypeStruct((bs, DIFF_HEADS, DIFF_DV), F32),
        compiler_params=_cparams(("arbitrary",), 48),
        name="attention_decode",
    )(page_table, *([ck] * n_pages), *([cv] * n_pages), qt, kn, vn, bias, bias_new, g3,
      subln_g.reshape(1, DIFF_DV), lam_rows)
    return out.reshape(bs, DIFF_OUT_W).astype(BF16)


def _softplus(x):
    return jnp.maximum(x, 0.0) + jnp.log1p(jnp.exp(-jnp.abs(x)))


def _lru_coeffs(xc, wg_ref, ba_ref, bx_ref, lam_ref, n):
    sl = slice(n * RNN_BLOCK_W, (n + 1) * RNN_BLOCK_W)
    xn = xc[:, sl]
    z = jnp.dot(xn.astype(BF16), wg_ref[n], preferred_element_type=F32)
    rg = _sigmoid(z[:, :RNN_BLOCK_W] + ba_ref[:, sl])
    ig = _sigmoid(z[:, RNN_BLOCK_W:] + bx_ref[:, sl])
    log_a = -LRU_C * rg * _softplus(-lam_ref[:, sl])
    a = jnp.exp(log_a)
    y = -jnp.tanh(log_a) * (1.0 + a * a)
    root = jnp.where(y > 0.0, y * lax.rsqrt(y), 0.0)
    bt = root * (ig * xn)
    return a, bt


def _lru_prompt_kernel(x_ref, gate_ref, cw_ref, cb_ref, wg_ref, ba_ref, bx_ref, lam_ref,
                       o_ref, hlast_ref, conv_ref, xbuf_ref, a_ref, b_ref, h_ref, *, tm):
    t = pl.program_id(1)
    pad = SUBLANES

    @pl.when(t == 0)
    def _():
        xbuf_ref[0:pad, :] = jnp.zeros((pad, RNN_WIDTH), F32)
        h_ref[...] = jnp.zeros(h_ref.shape, F32)

    xbuf_ref[pad:pad + tm, :] = x_ref[...]
    xc = cb_ref[...]
    for j in range(CONV_WIDTH):
        off = pad - (CONV_WIDTH - 1) + j
        xc = xc + xbuf_ref[off:off + tm, :] * cw_ref[j:j + 1, :]
    conv_ref[...] = xbuf_ref[pad + tm - (CONV_WIDTH - 1):pad + tm, :]
    xbuf_ref[0:pad, :] = xbuf_ref[tm:tm + pad, :]

    for n in range(RNN_BLOCKS):
        a, bt = _lru_coeffs(xc, wg_ref, ba_ref, bx_ref, lam_ref, n)
        sl = slice(n * RNN_BLOCK_W, (n + 1) * RNN_BLOCK_W)
        a = a.reshape(tm // SUBLANES, SUBLANES, RNN_BLOCK_W)
        bt = bt.reshape(tm // SUBLANES, SUBLANES, RNN_BLOCK_W)
        r8 = lax.broadcasted_iota(jnp.int32, a.shape, 1)
        for d in (1, 2, 4):
            a_sh = pltpu.roll(a, d, 1)
            b_sh = pltpu.roll(bt, d, 1)
            ok = r8 >= d
            bt = jnp.where(ok, a * b_sh + bt, bt)
            a = jnp.where(ok, a * a_sh, a)
        a_ref[:, sl] = a.reshape(tm, RNN_BLOCK_W)
        b_ref[:, sl] = bt.reshape(tm, RNN_BLOCK_W)

    h = h_ref[...]
    for gi in range(tm // SUBLANES):
        rs = slice(gi * SUBLANES, (gi + 1) * SUBLANES)
        hg = a_ref[rs, :] * h + b_ref[rs, :]
        b_ref[rs, :] = hg
        h = hg[SUBLANES - 1:SUBLANES, :]
    h_ref[...] = h
    hlast_ref[...] = h
    o_ref[...] = (b_ref[...] * gate_ref[...].astype(F32)).astype(o_ref.dtype)


def _lru_prompt(lx, gates, conv_w, conv_b, wg, ba, bx, lam, b, s, tm):
    nt = s // tm
    t = b * s
    gcol = 2
    vec = pl.BlockSpec((1, RNN_WIDTH), lambda i, j: (0, 0))
    return pl.pallas_call(
        functools.partial(_lru_prompt_kernel, tm=tm),
        grid=(b, nt),
        in_specs=[pl.BlockSpec((tm, RNN_WIDTH), lambda i, j: (i * nt + j, 0)),
                  pl.BlockSpec((tm, RNN_WIDTH), lambda i, j: (i * nt + j, gcol)),
                  pl.BlockSpec((CONV_WIDTH, RNN_WIDTH), lambda i, j: (0, 0)),
                  vec,
                  pl.BlockSpec((RNN_BLOCKS, RNN_BLOCK_W, 2 * RNN_BLOCK_W), lambda i, j: (0, 0, 0)),
                  vec, vec, vec],
        out_specs=[pl.BlockSpec((tm, RNN_WIDTH), lambda i, j: (i * nt + j, 0)),
                   pl.BlockSpec((None, 1, RNN_WIDTH), lambda i, j: (i, 0, 0)),
                   pl.BlockSpec((None, CONV_WIDTH - 1, RNN_WIDTH), lambda i, j: (i, 0, 0))],
        out_shape=[jax.ShapeDtypeStruct((t, RNN_WIDTH), BF16),
                   jax.ShapeDtypeStruct((b, 1, RNN_WIDTH), F32),
                   jax.ShapeDtypeStruct((b, CONV_WIDTH - 1, RNN_WIDTH), F32)],
        scratch_shapes=[pltpu.VMEM((tm + SUBLANES, RNN_WIDTH), F32),
                        pltpu.VMEM((tm, RNN_WIDTH), F32),
                        pltpu.VMEM((tm, RNN_WIDTH), F32),
                        pltpu.VMEM((1, RNN_WIDTH), F32)],
        compiler_params=_cparams(("parallel", "arbitrary"), 32),
        name="lru_prompt",
    )(lx, gates, conv_w, conv_b.reshape(1, -1), wg, ba.reshape(1, -1), bx.reshape(1, -1),
      lam.reshape(1, -1))


def _lru_step_kernel(x_ref, c0_ref, c1_ref, c2_ref, h0_ref, gate_ref, cw_ref, cb_ref, wg_ref,
                     ba_ref, bx_ref, lam_ref, o_ref, h_ref):
    xc = (cb_ref[...] + c0_ref[...] * cw_ref[0:1, :] + c1_ref[...] * cw_ref[1:2, :]
          + c2_ref[...] * cw_ref[2:3, :] + x_ref[...] * cw_ref[3:4, :])
    for n in range(RNN_BLOCKS):
        a, bt = _lru_coeffs(xc, wg_ref, ba_ref, bx_ref, lam_ref, n)
        sl = slice(n * RNN_BLOCK_W, (n + 1) * RNN_BLOCK_W)
        h = a * h0_ref[:, sl] + bt
        h_ref[:, sl] = h
        o_ref[:, sl] = (h * gate_ref[:, sl].astype(F32)).astype(o_ref.dtype)


def _lru_step(lx, conv_prev, h0, gate, conv_w, conv_b, wg, ba, bx, lam):
    bs = lx.shape[0]
    vm = pl.BlockSpec(memory_space=pltpu.VMEM)
    return pl.pallas_call(
        _lru_step_kernel,
        in_specs=[vm] * 12,
        out_specs=[vm, vm],
        out_shape=[jax.ShapeDtypeStruct((bs, RNN_WIDTH), BF16),
                   jax.ShapeDtypeStruct((bs, RNN_WIDTH), F32)],
        compiler_params=pltpu.CompilerParams(vmem_limit_bytes=32 * MIB),
        name="lru_step",
    )(lx, conv_prev[:, 0], conv_prev[:, 1], conv_prev[:, 2], h0, gate, conv_w,
      conv_b.reshape(1, -1), wg, ba.reshape(1, -1), bx.reshape(1, -1), lam.reshape(1, -1))


def _merge_kernel(ro_ref, do_ref, lo_ref, mg_ref, x_ref, p_ref, wr_ref, wd_ref, wl_ref, wo_ref,
                  wpg_ref, wp_ref, gn_ref, x_out_ref, hn_ref):
    d = D_MODEL
    tm = x_ref.shape[0]
    chunk = min(MERGE_CHUNK, tm)
    chunks = [slice(c * chunk, (c + 1) * chunk) for c in range(tm // chunk)]
    dot = functools.partial(jnp.dot, preferred_element_type=F32)
    merged = [(mg_ref[rs, 0:d].astype(F32) * dot(ro_ref[rs, :], wr_ref[...])
               + mg_ref[rs, d:2 * d].astype(F32) * dot(do_ref[rs, :], wd_ref[...])
               + mg_ref[rs, 2 * d:3 * d].astype(F32) * dot(lo_ref[rs, :], wl_ref[...])).astype(BF16)
              for rs in chunks]
    ple = [dot(p_ref[rs, :].astype(BF16), wp_ref[...]) for rs in chunks]
    x1 = [x_ref[rs, :] + dot(m, wo_ref[...]) for rs, m in zip(chunks, merged)]
    gate = [dot(v.astype(BF16), wpg_ref[...]) for v in x1]
    for rs, v, g, e in zip(chunks, x1, gate, ple):
        x2 = v + _sigmoid(g) * e
        x_out_ref[rs, :] = x2
        y = x2 * lax.rsqrt(jnp.mean(x2 * x2, axis=-1, keepdims=True) + NORM_EPS)
        hn_ref[rs, :] = (y * gn_ref[...]).astype(hn_ref.dtype)


MERGE_CHUNK = 256


def _merge(ro, do, lo, mg, x, p_all, layer, wr, wd, wl, wo, wpg, wp, g_next, hn_dtype, tm):
    t = x.shape[0]
    d = D_MODEL
    row = lambda wdt: pl.BlockSpec((tm, wdt), lambda i: (i, 0))
    full = lambda a: pl.BlockSpec(a.shape, lambda i: (0, 0), pipeline_mode=pl.Buffered(1))
    return pl.pallas_call(
        _merge_kernel,
        grid=(t // tm,),
        in_specs=[row(RET_V_W), row(DIFF_OUT_W), row(RNN_WIDTH), row(3 * d), row(d),
                  pl.BlockSpec((None, tm, D_PLE), lambda i: (layer, i, 0)),
                  full(wr), full(wd), full(wl), full(wo), full(wpg), full(wp),
                  pl.BlockSpec((1, d), lambda i: (0, 0))],
        out_specs=[row(d), row(d)],
        out_shape=[jax.ShapeDtypeStruct((t, d), F32), jax.ShapeDtypeStruct((t, d), hn_dtype)],
        compiler_params=_cparams(("parallel",), 56),
        name="merge",
    )(ro, do, lo, mg, x, p_all, wr, wd, wl, wo, wpg, wp, g_next.reshape(1, d))


def _lambda_init(layer):
    return 0.8 - 0.6 * math.exp(-0.3 * layer)


def _rope_tables(pos):
    half = RET_DK // 2
    inv = ROPE_BASE ** (-jnp.arange(half, dtype=F32) / half)
    ang = pos.astype(F32)[:, None] * inv[None, :]
    cos = jnp.cos(ang)
    sin = jnp.sin(ang)
    return jnp.concatenate([cos, cos], axis=-1), jnp.concatenate([-sin, sin], axis=-1)


def kernel(x_prompt, x_sample, cache_k, cache_v, state_ret, state_lru, state_conv, page_table, p_prompt, p_sample, rel_bias, norm_g, w_in, lambda_q1, lambda_k1, lambda_q2, lambda_k2, subln_g, conv_w, conv_b, gate_a_w, gate_a_b, gate_x_w, gate_x_b, lru_lambda, w_ret_out, w_diff_out, w_lru_out, w_o, w_ple, w_ple_gate, final_norm_g):
    bp, sp, d = x_prompt.shape
    bs, ss, _ = x_sample.shape
    assert d == D_MODEL and ss == 1
    depth = w_in.shape[0]
    n_pages, page = page_table.shape[1], cache_k.shape[2]
    past = n_pages * page
    tp = bp * sp
    blk = min(256, sp)
    tm_p = min(1024, sp)
    tm_qkv = min(512, sp)
    tm_lru = min(256, sp)
    tm_merge = min(512, sp)
    p_prompt_all = p_prompt.reshape(depth, tp, D_PLE)
    p_sample_all = p_sample.reshape(depth, bs, D_PLE)

    cos_p, sin_p = _rope_tables(jnp.arange(sp, dtype=jnp.int32))
    cos_s, sin_s = _rope_tables(jnp.full((bs,), past, dtype=jnp.int32))
    bias_p = _bias_prompt(rel_bias, blk)
    bias_s, bias_s_new = _bias_sample(rel_bias, past)

    w_in_b = w_in.astype(BF16)
    wg_all = jnp.concatenate([gate_a_w, gate_x_w], axis=-1).astype(BF16)
    xp = x_prompt.reshape(tp, d)
    xs = x_sample.reshape(bs, d)
    hn_p = _rmsnorm(xp, norm_g[0], BF16, tm_p)
    hn_s = _rmsnorm(xs, norm_g[0], BF16, bs)

    outs = {k: [] for k in ("kp", "vp", "ks", "vs", "rp", "rs", "lp", "ls", "cp", "cs")}
    for l in range(depth):
        wg = wg_all[l]
        wr, wd, wl = w_ret_out[l].astype(BF16), w_diff_out[l].astype(BF16), w_lru_out[l].astype(BF16)
        wo, wpg, wp = w_o[l].astype(BF16), w_ple_gate[l].astype(BF16), w_ple[l].astype(BF16)
        lam0 = _lambda_init(l)
        lam_rows = jnp.stack([lambda_q1[l], lambda_k1[l], lambda_q2[l], lambda_k2[l]], axis=0)
        last = l == depth - 1
        g_next = final_norm_g if last else norm_g[l + 1]
        hn_dtype = F32 if last else BF16

        qk = _proj(hn_p, w_in_b, l, COL_ROT, "rotary", tm_p, tables=(cos_p, sin_p))
        rv = _proj(hn_p, w_in_b, l, COL_RV, "plain", tm_p)
        gates = _proj(hn_p, w_in_b, l, COL_SILU, "silu", tm_p)
        kf, vf, kb, qt, vt = _proj_qkv_t(hn_p, w_in_b, l, tm_qkv, blk)
        lx = _proj(hn_p, w_in_b, l, COL_LX, "plain", tm_p, out_dtype=F32)
        mg = _proj(hn_p, w_in_b, l, COL_MG, "sigmoid", tm_p)
        ro, s_fin = _retention_prompt(qk, rv, gates, bp, sp)
        do = _attention_prompt(qt, kb, vt, bias_p, gates, subln_g[l], lam_rows, lam0, bp, sp, blk)
        lo, h_last, conv_new = _lru_prompt(lx, gates, conv_w[l], conv_b[l], wg, gate_a_b[l],
                                           gate_x_b[l], lru_lambda[l], bp, sp, tm_lru)
        xp, hn_p = _merge(ro, do, lo, mg, xp, p_prompt_all, l, wr, wd, wl, wo, wpg, wp,
                          g_next, hn_dtype, tm_merge)
        outs["kp"].append(kf.reshape(bp, sp, DIFF_KV_HEADS, 2 * DIFF_DH))
        outs["vp"].append(vf.reshape(bp, sp, DIFF_KV_HEADS, DIFF_DV))
        outs["rp"].append(s_fin)
        outs["lp"].append(h_last.reshape(bp, RNN_WIDTH))
        outs["cp"].append(conv_new)

        qk = _proj(hn_s, w_in_b, l, COL_ROT, "rotary", bs, tables=(cos_s, sin_s))
        rvdq = _proj(hn_s, w_in_b, l, COL_RV_DQ, "plain", bs)
        gates = _proj(hn_s, w_in_b, l, COL_SILU, "silu", bs)
        kf, vf, kb, vb = _proj(hn_s, w_in_b, l, (COL_KV,), "kv", bs)
        lx = _proj(hn_s, w_in_b, l, COL_LX, "plain", bs, out_dtype=F32)
        mg = _proj(hn_s, w_in_b, l, COL_MG, "sigmoid", bs)
        ro, s_new = _retention_step(qk, rvdq, gates, state_ret, l)
        do = _attention_decode(rvdq[:, RET_V_W:], kb, vb, gates[:, RET_V_W:RET_V_W + DIFF_OUT_W],
                               cache_k, cache_v, page_table, bias_s, bias_s_new, subln_g[l],
                               lam_rows, lam0, l)
        conv_prev = state_conv[:, l]
        lo, h_new = _lru_step(lx, conv_prev, state_lru[:, l], gates[:, 2 * RET_V_W:], conv_w[l],
                              conv_b[l], wg, gate_a_b[l], gate_x_b[l], lru_lambda[l])
        xs, hn_s = _merge(ro, do, lo, mg, xs, p_sample_all, l, wr, wd, wl, wo, wpg, wp,
                          g_next, hn_dtype, bs)
        outs["ks"].append(kf.reshape(bs, 1, DIFF_KV_HEADS, 2 * DIFF_DH))
        outs["vs"].append(vf.reshape(bs, 1, DIFF_KV_HEADS, DIFF_DV))
        outs["rs"].append(s_new)
        outs["ls"].append(h_new)
        outs["cs"].append(jnp.concatenate([conv_prev[:, 1:], lx[:, None, :]], axis=1))

    st = lambda key: jnp.stack(outs[key], axis=1)
    return (hn_p.reshape(bp, sp, d), hn_s.reshape(bs, 1, d), st("kp"), st("vp"), st("ks"), st("vs"),
            st("rp"), st("rs"), st("lp"), st("ls"), st("cp"), st("cs"))
```

```python
import functools
import math

import numpy as np
import jax
import jax.numpy as jnp
from jax import lax
from jax.experimental import pallas as pl
from jax.experimental.pallas import tpu as pltpu

F32 = jnp.float32
BF16 = jnp.bfloat16

D_MODEL = 1024
D_PLE = 256
NORM_EPS = 1e-6
RET_HEADS = 4
RET_DK = 128
RET_DV = 256
RET_CHUNK = 128
RET_CHUNKS_PER_STEP = 4
ROPE_BASE = 10000.0
DIFF_HEADS = 8
DIFF_KV_HEADS = 4
DIFF_GROUP = DIFF_HEADS // DIFF_KV_HEADS
DIFF_DH = 64
DIFF_DV = 2 * DIFF_DH
REL_BUCKETS = 32
REL_MAX_DIST = 128
RNN_WIDTH = 1024
RNN_BLOCKS = 8
RNN_BLOCK_W = RNN_WIDTH // RNN_BLOCKS
CONV_WIDTH = 4
LRU_C = 8.0

RET_QK_W = RET_HEADS * RET_DK
RET_V_W = RET_HEADS * RET_DV
DIFF_Q_W = DIFF_HEADS * 2 * DIFF_DH
DIFF_K_W = DIFF_KV_HEADS * 2 * DIFF_DH
DIFF_V_W = DIFF_KV_HEADS * DIFF_DV
DIFF_OUT_W = DIFF_HEADS * DIFF_DV
IN_SPLITS = (RET_QK_W, RET_QK_W, RET_V_W, RET_V_W,
             DIFF_Q_W, DIFF_K_W, DIFF_V_W, DIFF_OUT_W,
             RNN_WIDTH, RNN_WIDTH, 3 * D_MODEL)
IN_OFFS = tuple(int(v) for v in np.cumsum((0,) + IN_SPLITS))

LANES = 128
SUBLANES = 8
MASK_VALUE = -1e30
LOG2E = math.log2(math.e)
VT_ONES = 2 * SUBLANES
VT_ROWS = DIFF_DV + VT_ONES
BIAS_DIAG, BIAS_PREV, BIAS_ZERO = 0, 1, 2
MIB = 1024 * 1024


def _cparams(sem, vmem_mib):
    return pltpu.CompilerParams(dimension_semantics=sem, vmem_limit_bytes=int(vmem_mib * MIB))


def _rmsnorm_kernel(x_ref, g_ref, o_ref):
    x = x_ref[...]
    y = x * lax.rsqrt(jnp.mean(x * x, axis=-1, keepdims=True) + NORM_EPS)
    o_ref[...] = (y * g_ref[...]).astype(o_ref.dtype)


def _rmsnorm(x, g, out_dtype, tm):
    t, d = x.shape
    return pl.pallas_call(
        _rmsnorm_kernel,
        grid=(t // tm,),
        in_specs=[pl.BlockSpec((tm, d), lambda i: (i, 0)),
                  pl.BlockSpec((1, d), lambda i: (0, 0))],
        out_specs=pl.BlockSpec((tm, d), lambda i: (i, 0)),
        out_shape=jax.ShapeDtypeStruct((t, d), out_dtype),
        compiler_params=_cparams(("parallel",), 32),
        name="rmsnorm",
    )(x, g.reshape(1, d))


def _proj_qkv_t_kernel(x_ref, wq_ref, wkv_ref, kf_ref, vf_ref, kb_ref, qt_ref, vt_ref):
    blk = qt_ref.shape[-1]

    def matmuls(j):
        x = x_ref[j * blk:(j + 1) * blk, :]
        return (jnp.dot(x, wq_ref[...], preferred_element_type=F32),
                jnp.dot(x, wkv_ref[...], preferred_element_type=F32))

    def epilogue(j, q, kv):
        rows = slice(j * blk, (j + 1) * blk)
        k = kv[:, :DIFF_K_W]
        v = kv[:, DIFF_K_W:]
        for h in range(DIFF_KV_HEADS):
            dst = pl.ds(j * blk * DIFF_KV_HEADS + h, blk, stride=DIFF_KV_HEADS)
            kf_ref[dst, :] = k[:, h * 2 * DIFF_DH:(h + 1) * 2 * DIFF_DH]
            vf_ref[dst, :] = v[:, h * DIFF_DV:(h + 1) * DIFF_DV]
        kb_ref[rows, :] = k.astype(BF16)
        qt_ref[j] = (q * (DIFF_DH ** -0.5 * LOG2E)).T.astype(BF16)
        vt = v.T.astype(BF16)
        for h in range(DIFF_KV_HEADS):
            vt_ref[j, h * VT_ROWS:h * VT_ROWS + DIFF_DV, :] = vt[h * DIFF_DV:(h + 1) * DIFF_DV]
            vt_ref[j, h * VT_ROWS + DIFF_DV:(h + 1) * VT_ROWS, :] = jnp.ones((VT_ONES, blk), BF16)

    _pipeline_rows(qt_ref.shape[0], matmuls, epilogue)


def _pipeline_rows(n_chunks, matmuls, epilogue):
    pending = {}
    for c in range(n_chunks + 1):
        if c < n_chunks:
            pending[c] = matmuls(c)
        if c >= 1:
            epilogue(c - 1, *pending.pop(c - 1))


def _sigmoid(x):
    return 0.5 * jnp.tanh(0.5 * x) + 0.5


def _proj_kernel(x_ref, w_ref, *refs, kind, chunk):
    tm = x_ref.shape[0]

    def matmuls(c):
        return (jnp.dot(x_ref[c * chunk:(c + 1) * chunk, :], w_ref[...], preferred_element_type=F32),)

    def epilogue(c, acc):
        rows = slice(c * chunk, (c + 1) * chunk)
        if kind == "plain":
            (o_ref,) = refs
            o_ref[rows, :] = acc.astype(o_ref.dtype)
        elif kind == "silu":
            (o_ref,) = refs
            o_ref[rows, :] = (acc * _sigmoid(acc)).astype(o_ref.dtype)
        elif kind == "sigmoid":
            (o_ref,) = refs
            o_ref[rows, :] = _sigmoid(acc).astype(o_ref.dtype)
        elif kind == "rotary":
            cos_ref, sin_ref, o_ref = refs
            cos = cos_ref[rows, :]
            sin = sin_ref[rows, :]
            for h in range(2 * RET_HEADS):
                z = acc[:, h * RET_DK:(h + 1) * RET_DK]
                r = z * cos + pltpu.roll(z, RET_DK // 2, 1) * sin
                if h >= RET_HEADS:
                    r = r * (RET_DK ** -0.5)
                o_ref[rows, h * RET_DK:(h + 1) * RET_DK] = r.astype(o_ref.dtype)
        elif kind == "kv":
            kf_ref, vf_ref, kb_ref, vb_ref = refs
            k = acc[:, :DIFF_K_W]
            v = acc[:, DIFF_K_W:]
            kf_ref[rows, :] = k
            vf_ref[rows, :] = v
            kb_ref[rows, :] = k.astype(BF16)
            vb_ref[rows, :] = v.astype(BF16)
        else:
            raise ValueError(kind)

    _pipeline_rows(tm // chunk, matmuls, epilogue)


PROJ_TN = 1024
PROJ_CHUNK = 256
COL_ROT = (0,)
COL_RV = (1,)
COL_RV_DQ = (1, 3)
COL_SILU = (2, 5, 7)
COL_DQ = 3
COL_KV = 4
COL_LX = (6,)
COL_MG = (8, 9, 10)
assert IN_OFFS[2] == PROJ_TN and IN_OFFS[5] == COL_KV * PROJ_TN and IN_OFFS[10] == COL_MG[0] * PROJ_TN


def _col_index(cols, j):
    if len(cols) == 1:
        return cols[0]
    step = cols[1] - cols[0]
    idx = cols[0] + step * j
    if cols[-1] != cols[0] + step * (len(cols) - 1):
        idx = idx - j // (len(cols) - 1)
    return idx


def _proj_qkv_t(hn, w_all, layer, tm, blk):
    t, d = hn.shape
    nj = tm // blk
    half = pl.BlockSpec((tm, DIFF_K_W), lambda i: (i, 0))
    per_head = pl.BlockSpec((tm * DIFF_KV_HEADS, DIFF_DV), lambda i: (i, 0))
    return pl.pallas_call(
        _proj_qkv_t_kernel,
        grid=(t // tm,),
        in_specs=[pl.BlockSpec((tm, d), lambda i: (i, 0)),
                  pl.BlockSpec((None, d, PROJ_TN), lambda i: (layer, 0, COL_DQ)),
                  pl.BlockSpec((None, d, PROJ_TN), lambda i: (layer, 0, COL_KV))],
        out_specs=[per_head, per_head, half,
                   pl.BlockSpec((nj, DIFF_Q_W, blk), lambda i: (i, 0, 0)),
                   pl.BlockSpec((nj, DIFF_KV_HEADS * VT_ROWS, blk), lambda i: (i, 0, 0))],
        out_shape=[jax.ShapeDtypeStruct((t * DIFF_KV_HEADS, 2 * DIFF_DH), F32),
                   jax.ShapeDtypeStruct((t * DIFF_KV_HEADS, DIFF_DV), F32),
                   jax.ShapeDtypeStruct((t, DIFF_K_W), BF16),
                   jax.ShapeDtypeStruct((t // blk, DIFF_Q_W, blk), BF16),
                   jax.ShapeDtypeStruct((t // blk, DIFF_KV_HEADS * VT_ROWS, blk), BF16)],
        compiler_params=_cparams(("parallel",), 48),
        name="proj_qkv_t",
    )(hn, w_all, w_all)


def _proj(hn, w_all, layer, cols, kind, tm, out_dtype=BF16, tables=None):
    t, d = hn.shape
    tn = PROJ_TN
    n = tn * len(cols)
    grid = (len(cols), t // tm)
    in_specs = [pl.BlockSpec((tm, d), lambda j, i: (i, 0)),
                pl.BlockSpec((None, d, tn), lambda j, i: (layer, 0, _col_index(cols, j)))]
    args = [hn, w_all]
    if kind == "rotary":
        cos, sin = tables
        nb = cos.shape[0] // tm
        spec = pl.BlockSpec((tm, RET_DK), lambda j, i: (i % nb, 0))
        in_specs += [spec, spec]
        args += [cos, sin]
    if kind == "kv":
        half = pl.BlockSpec((tm, DIFF_K_W), lambda j, i: (i, 0))
        out_specs = [half, half, half, half]
        out_shape = [jax.ShapeDtypeStruct((t, DIFF_K_W), F32),
                     jax.ShapeDtypeStruct((t, DIFF_V_W), F32),
                     jax.ShapeDtypeStruct((t, DIFF_K_W), BF16),
                     jax.ShapeDtypeStruct((t, DIFF_V_W), BF16)]
    else:
        out_specs = pl.BlockSpec((tm, tn), lambda j, i: (i, j))
        out_shape = jax.ShapeDtypeStruct((t, n), out_dtype)
    return pl.pallas_call(
        functools.partial(_proj_kernel, kind=kind, chunk=min(PROJ_CHUNK, tm)),
        grid=grid, in_specs=in_specs, out_specs=out_specs, out_shape=out_shape,
        compiler_params=_cparams(("parallel", "parallel"), 40),
        name="proj_" + kind,
    )(*args)


def _retention_kernel(qk_ref, v_ref, gate_ref, dmask_ref, qdec_ref, kdec_ref,
                      o_ref, sfin_ref, state_ref, *, chunk_dec):
    c = pl.program_id(1)

    @pl.when(c == 0)
    def _():
        state_ref[...] = jnp.zeros_like(state_ref)

    heads = range(RET_HEADS)
    dot = functools.partial(jnp.dot, preferred_element_type=F32)
    cw = dmask_ref.shape[-1]
    for ci in range(qk_ref.shape[0] // cw):
        rows = slice(ci * cw, (ci + 1) * cw)
        q = [qk_ref[rows, h * RET_DK:(h + 1) * RET_DK] for h in heads]
        k = [qk_ref[rows, RET_QK_W + h * RET_DK:RET_QK_W + (h + 1) * RET_DK] for h in heads]
        v = [v_ref[rows, h * RET_DV:(h + 1) * RET_DV] for h in heads]
        s = [lax.dot_general(q[h], k[h], (((1,), (1,)), ((), ())), preferred_element_type=F32)
             for h in heads]
        cross = [dot(q[h], state_ref[h].astype(BF16)) for h in heads]
        kd = [(k[h].astype(F32) * kdec_ref[:, h:h + 1]).astype(BF16) for h in heads]
        kv = [lax.dot_general(kd[h], v[h], (((0,), (0,)), ((), ())), preferred_element_type=F32)
              for h in heads]
        inner = [dot((s[h] * dmask_ref[h]).astype(BF16), v[h]) for h in heads]
        for h in heads:
            state_ref[h] = state_ref[h] * chunk_dec[h] + kv[h]
            o = inner[h] + cross[h] * qdec_ref[:, h:h + 1]
            o = o * lax.rsqrt(jnp.mean(o * o, axis=-1, keepdims=True) + NORM_EPS)
            g = gate_ref[rows, h * RET_DV:(h + 1) * RET_DV].astype(F32)
            o_ref[rows, h * RET_DV:(h + 1) * RET_DV] = (o * g).astype(o_ref.dtype)

    @pl.when(c == pl.num_programs(1) - 1)
    def _():
        sfin_ref[...] = state_ref[...]


def _retention_prompt(qk, rvdq, gates, b, s):
    c = math.gcd(s, RET_CHUNK)
    nc = s // c
    log_g = np.log1p(-(2.0 ** (-5.0 - np.arange(RET_HEADS, dtype=np.float64))))
    idx = np.arange(c, dtype=np.float64)
    dist = idx[:, None] - idx[None, :]
    dmask = np.where(dist[None] >= 0, np.exp(np.maximum(dist, 0.0)[None] * log_g[:, None, None]), 0.0)
    qdec = np.exp((idx + 1.0)[:, None] * log_g[None, :])
    kdec = np.exp((c - 1.0 - idx)[:, None] * log_g[None, :])
    chunk_dec = tuple(float(x) for x in np.exp(c * log_g).astype(np.float32))
    t = b * s
    per_step = math.gcd(nc, RET_CHUNKS_PER_STEP)
    rows = c * per_step
    nc = nc // per_step
    return pl.pallas_call(
        functools.partial(_retention_kernel, chunk_dec=chunk_dec),
        grid=(b, nc),
        in_specs=[pl.BlockSpec((rows, 2 * RET_QK_W), lambda i, j: (i * nc + j, 0)),
                  pl.BlockSpec((rows, RET_V_W), lambda i, j: (i * nc + j, 0)),
                  pl.BlockSpec((rows, RET_V_W), lambda i, j: (i * nc + j, 0)),
                  pl.BlockSpec((RET_HEADS, c, c), lambda i, j: (0, 0, 0)),
                  pl.BlockSpec((c, RET_HEADS), lambda i, j: (0, 0)),
                  pl.BlockSpec((c, RET_HEADS), lambda i, j: (0, 0))],
        out_specs=[pl.BlockSpec((rows, RET_V_W), lambda i, j: (i * nc + j, 0)),
                   pl.BlockSpec((None, RET_HEADS, RET_DK, RET_DV), lambda i, j: (i, 0, 0, 0))],
        out_shape=[jax.ShapeDtypeStruct((t, RET_V_W), BF16),
                   jax.ShapeDtypeStruct((b, RET_HEADS, RET_DK, RET_DV), F32)],
        scratch_shapes=[pltpu.VMEM((RET_HEADS, RET_DK, RET_DV), F32)],
        compiler_params=_cparams(("parallel", "arbitrary"), 32),
        name="retention_prompt",
    )(qk, rvdq, gates, jnp.asarray(dmask, F32), jnp.asarray(qdec, F32), jnp.asarray(kdec, F32))


def _retention_step_kernel(qk_ref, v_ref, gate_ref, s_ref, o_ref, snew_ref, *, decay):
    sb = qk_ref.shape[0]
    rowid = lax.broadcasted_iota(jnp.int32, (sb, 1), 0)

    def per_sequence(b, outs):
        new_outs = []
        for h in range(RET_HEADS):
            q = qk_ref[:, h * RET_DK:(h + 1) * RET_DK]
            k = qk_ref[:, RET_QK_W + h * RET_DK:RET_QK_W + (h + 1) * RET_DK]
            v = v_ref[:, h * RET_DV:(h + 1) * RET_DV]
            st = s_ref[b, h]
            qs = jnp.dot(q, st.astype(BF16), preferred_element_type=F32)
            new_outs.append(outs[h] + jnp.where(rowid == b, decay[h] * qs, 0.0))
            kb = jnp.where(rowid == b, k, jnp.zeros_like(k))
            snew_ref[b, h] = st * decay[h] + lax.dot_general(
                kb, v, (((0,), (0,)), ((), ())), preferred_element_type=F32)
        return tuple(new_outs)

    outs = []
    for h in range(RET_HEADS):
        q = qk_ref[:, h * RET_DK:(h + 1) * RET_DK].astype(F32)
        k = qk_ref[:, RET_QK_W + h * RET_DK:RET_QK_W + (h + 1) * RET_DK].astype(F32)
        v = v_ref[:, h * RET_DV:(h + 1) * RET_DV].astype(F32)
        outs.append(jnp.sum(q * k, axis=1, keepdims=True) * v)
    outs = lax.fori_loop(0, sb, per_sequence, tuple(outs))
    for h in range(RET_HEADS):
        o = outs[h]
        o = o * lax.rsqrt(jnp.mean(o * o, axis=-1, keepdims=True) + NORM_EPS)
        g = gate_ref[:, h * RET_DV:(h + 1) * RET_DV].astype(F32)
        o_ref[:, h * RET_DV:(h + 1) * RET_DV] = (o * g).astype(o_ref.dtype)


def _retention_step(qk, rvdq, gates, state_ret, layer):
    bs = qk.shape[0]
    sb = min(2 * SUBLANES, bs)
    log_g = np.log1p(-(2.0 ** (-5.0 - np.arange(RET_HEADS, dtype=np.float64))))
    decay = tuple(float(x) for x in np.exp(log_g).astype(np.float32))
    return pl.pallas_call(
        functools.partial(_retention_step_kernel, decay=decay),
        grid=(bs // sb,),
        in_specs=[pl.BlockSpec((sb, 2 * RET_QK_W), lambda i: (i, 0)),
                  pl.BlockSpec((sb, RET_V_W), lambda i: (i, 0)),
                  pl.BlockSpec((sb, RET_V_W), lambda i: (i, 0)),
                  pl.BlockSpec((sb, None, RET_HEADS, RET_DK, RET_DV),
                               lambda i: (i, layer, 0, 0, 0))],
        out_specs=[pl.BlockSpec((sb, RET_V_W), lambda i: (i, 0)),
                   pl.BlockSpec((sb, RET_HEADS, RET_DK, RET_DV), lambda i: (i, 0, 0, 0))],
        out_shape=[jax.ShapeDtypeStruct((bs, RET_V_W), BF16),
                   jax.ShapeDtypeStruct((bs, RET_HEADS, RET_DK, RET_DV), F32)],
        compiler_params=_cparams(("parallel",), 48),
        name="retention_step",
    )(qk, rvdq, gates, state_ret)


def _t5_bucket(rel):
    n = jnp.maximum(rel, 0)
    max_exact = REL_BUCKETS // 2
    nf = jnp.maximum(n.astype(F32), 1.0)
    large = max_exact + (jnp.log(nf / max_exact) / math.log(REL_MAX_DIST / max_exact)
                         * (REL_BUCKETS - max_exact)).astype(jnp.int32)
    large = jnp.minimum(large, REL_BUCKETS - 1)
    return jnp.where(n < max_exact, n, large)


def _bias_lookup(tab_ref, bucket, head):
    out = jnp.zeros(bucket.shape, F32)
    far = tab_ref[REL_BUCKETS - 1, head]
    for bkt in range(REL_BUCKETS - 1):
        out = jnp.where(bucket == bkt, tab_ref[bkt, head] - far, out)
    return out


def _bias_prompt_kernel(tab_ref, o_ref, *, blk):
    i = lax.broadcasted_iota(jnp.int32, (blk, blk), 1)
    j = lax.broadcasted_iota(jnp.int32, (blk, blk), 0)
    for which in (BIAS_DIAG, BIAS_PREV):
        rel = i - j + which * blk
        bucket = _t5_bucket(rel)
        for h in range(DIFF_HEADS):
            o_ref[which, h] = jnp.where(rel >= 0, _bias_lookup(tab_ref, bucket, h) * LOG2E, MASK_VALUE)
    o_ref[BIAS_ZERO] = jnp.zeros(o_ref.shape[1:], F32)


def _bias_prompt(rel_bias, blk):
    return pl.pallas_call(
        functools.partial(_bias_prompt_kernel, blk=blk),
        in_specs=[pl.BlockSpec(memory_space=pltpu.SMEM)],
        out_specs=pl.BlockSpec(memory_space=pltpu.VMEM),
        out_shape=jax.ShapeDtypeStruct((3, DIFF_HEADS, blk, blk), F32),
        compiler_params=pltpu.CompilerParams(vmem_limit_bytes=32 * MIB),
        name="bias_prompt",
    )(rel_bias)


def _bias_sample_kernel(tab_ref, o_ref, onew_ref, *, past, width):
    rows = 2 * DIFF_HEADS
    r = lax.broadcasted_iota(jnp.int32, (rows, width), 0)
    lane = lax.broadcasted_iota(jnp.int32, (rows, width), 1)
    head = r % DIFF_HEADS
    kvh = lane % DIFF_KV_HEADS
    pos = lane // DIFF_KV_HEADS
    bucket = _t5_bucket(past - pos)
    out = jnp.zeros((rows, width), F32)
    for h in range(DIFF_HEADS):
        out = jnp.where(head == h, _bias_lookup(tab_ref, bucket, h), out)
    o_ref[...] = jnp.where(kvh == head // DIFF_GROUP, out, MASK_VALUE)
    rn = lax.broadcasted_iota(jnp.int32, (rows, LANES), 0) % DIFF_HEADS
    new = jnp.zeros((rows, LANES), F32)
    zero_bucket = _t5_bucket(jnp.zeros((rows, LANES), jnp.int32))
    for h in range(DIFF_HEADS):
        new = jnp.where(rn == h, _bias_lookup(tab_ref, zero_bucket, h), new)
    onew_ref[...] = new


def _bias_sample(rel_bias, past):
    width = past * DIFF_KV_HEADS
    return pl.pallas_call(
        functools.partial(_bias_sample_kernel, past=past, width=width),
        in_specs=[pl.BlockSpec(memory_space=pltpu.SMEM)],
        out_specs=[pl.BlockSpec(memory_space=pltpu.VMEM), pl.BlockSpec(memory_space=pltpu.VMEM)],
        out_shape=[jax.ShapeDtypeStruct((2 * DIFF_HEADS, width), F32),
                   jax.ShapeDtypeStruct((2 * DIFF_HEADS, LANES), F32)],
        compiler_params=pltpu.CompilerParams(vmem_limit_bytes=32 * MIB),
        name="bias_sample",
    )(rel_bias)


def _lambda_full(lam_ref, lam0):
    a = jnp.sum(lam_ref[0:1, :] * lam_ref[1:2, :], axis=-1, keepdims=True)
    b = jnp.sum(lam_ref[2:3, :] * lam_ref[3:4, :], axis=-1, keepdims=True)
    return jnp.exp(a) - jnp.exp(b) + lam0


def _attn_prompt_body(qt_in_ref, k_ref, vt_ref, bias_ref, gate_ref, sg_ref, lam_ref, o_ref,
                      qt_ref, m_ref, acc_ref, *, blk, nq, lam0, side_work=()):
    pair = pl.program_id(2)
    tiles = (pair, nq - 1 - pair)
    n_chunks = 2 * DIFF_GROUP
    row = lax.broadcasted_iota(jnp.int32, (DIFF_DV, blk), 0)
    for slot in range(2):
        for g in range(DIFF_GROUP):
            qg = qt_in_ref[tiles[slot], g * DIFF_DV:(g + 1) * DIFF_DV, :]
            for m in range(2):
                keep = (row < DIFF_DH) if m == 0 else (row >= DIFF_DH)
                c = m * DIFF_GROUP + g
                qt_ref[slot, :, c * blk:(c + 1) * blk] = jnp.where(keep, qg, jnp.zeros_like(qg))
    m_ref[...] = jnp.full(m_ref.shape, MASK_VALUE, F32)
    acc_ref[...] = jnp.zeros(acc_ref.shape, F32)

    def run_blocks(blocks):
        s_lag = 3
        kbs =[k_ref[pl.ds(pl.multiple_of(ki * blk, blk), blk), :] for _, ki, _ in blocks]
        vts = [vt_ref[ki] for _, ki, _ in blocks]
        units = [(bi, c) for bi in range(len(blocks)) for c in range(n_chunks)]
        n_units = len(units)

        def scores(u):
            bi, c = units[u]
            slot, _, bias = blocks[bi]
            s = jnp.dot(kbs[bi], qt_ref[slot, :, c * blk:(c + 1) * blk], preferred_element_type=F32)
            if bias is not None:
                s = s + bias_ref[bias, c % DIFF_GROUP]
            return s

        def softmax(u, s):
            bi, c = units[u]
            slot = blocks[bi][0]
            cs = slice(c * blk, (c + 1) * blk)
            m_prev = m_ref[slot, :, cs]
            m_new = jnp.maximum(m_prev, jnp.max(s, axis=0, keepdims=True))
            p = jnp.exp2(s - m_new)
            alpha = jnp.exp2(m_prev - m_new)
            m_ref[slot, :, cs] = m_new
            return p.astype(BF16), alpha

        def accumulate(u, p, alpha):
            bi, c = units[u]
            slot = blocks[bi][0]
            cs = slice(c * blk, (c + 1) * blk)
            acc_ref[slot, :, cs] = acc_ref[slot, :, cs] * alpha + jnp.dot(
                vts[bi], p, preferred_element_type=F32)

        s_vals, p_vals = {}, {}
        n_steps = n_units + s_lag + 1
        for t in range(n_steps):
            if t < n_units:
                s_vals[t] = scores(t)
            if 0 <= t - s_lag < n_units:
                p_vals[t - s_lag] = softmax(t - s_lag, s_vals.pop(t - s_lag))
            if 0 <= t - s_lag - 1 < n_units:
                accumulate(t - s_lag - 1, *p_vals.pop(t - s_lag - 1))
            lo, hi = (t * len(side_work)) // n_steps, ((t + 1) * len(side_work)) // n_steps
            for piece in side_work[lo:hi]:
                piece()

    n_b = jnp.minimum(nq - 2 - pair, nq - 3)
    blocks = []
    for j in range(nq - 3):
        if j < nq // 2 - 1:
            blocks.append((1, j, None))
        else:
            in_b = j < n_b
            blocks.append((jnp.where(in_b, 1, 0), jnp.where(in_b, j, j - n_b), None))
    first = pair == 0
    blocks.append((jnp.where(first, 1, 0), jnp.where(first, nq - 3, pair - 1),
                   jnp.where(first, BIAS_ZERO, BIAS_PREV)))
    blocks.append((0, pair, BIAS_DIAG))
    blocks.append((1, nq - 2 - pair, BIAS_PREV))
    blocks.append((1, nq - 1 - pair, BIAS_DIAG))
    run_blocks(blocks)

    lam = _lambda_full(lam_ref, lam0)
    for slot in range(2):
        rows = pl.ds(pl.multiple_of(tiles[slot] * blk, blk), blk)
        a = acc_ref[slot, 0:DIFF_DV, :] / acc_ref[slot, DIFF_DV:DIFF_DV + 1, :]
        for g in range(DIFF_GROUP):
            ot = a[:, g * blk:(g + 1) * blk] - lam * a[:, (DIFF_GROUP + g) * blk:(DIFF_GROUP + g + 1) * blk]
            ot = ot * lax.rsqrt(jnp.mean(ot * ot, axis=0, keepdims=True) + NORM_EPS)
            ot = ot * sg_ref[...] * (1.0 - lam0)
            gate = gate_ref[rows, g * DIFF_DV:(g + 1) * DIFF_DV].astype(F32)
            o_ref[rows, g * DIFF_DV:(g + 1) * DIFF_DV] = (ot.T * gate).astype(o_ref.dtype)


def _attn_decode_pieces(k_refs, v_refs, qt_ref, kn_ref, vn_ref, bias_ref, bnew_ref, gate_ref,
                        sg_ref, lam_ref, o_ref, s_ref, p_ref, *, rows_per_page, lam0):
    n_pages = len(k_refs)
    w = rows_per_page
    state = {}

    def score_page(p):
        def run():
            kp = k_refs[p][...].astype(BF16)
            sp = lax.dot_general(qt_ref[...], kp, (((1,), (1,)), ((), ())),
                                 preferred_element_type=F32)
            s_ref[:, p * w:(p + 1) * w] = sp + bias_ref[:, p * w:(p + 1) * w]
        return run

    def softmax():
        s = s_ref[...]
        s_new = jnp.sum(qt_ref[...].astype(F32) * kn_ref[...].astype(F32), axis=1, keepdims=True)
        s_new = s_new + bnew_ref[:, 0:1]
        m = jnp.maximum(jnp.max(s, axis=1, keepdims=True), s_new)
        e_new = jnp.exp(s_new - m)
        pr = jnp.exp(s - m)
        state["l"] = jnp.sum(pr, axis=1, keepdims=True) + e_new
        state["acc"] = e_new * vn_ref[...].astype(F32)
        p_ref[...] = pr.astype(BF16)

    def value_page(p):
        def run():
            vp = v_refs[p][...].astype(BF16)
            state["acc"] = state["acc"] + jnp.dot(p_ref[:, p * w:(p + 1) * w], vp,
                                                  preferred_element_type=F32)
        return run

    def finish():
        a = state["acc"] / state["l"]
        lam = _lambda_full(lam_ref, lam0)
        o = a[:DIFF_HEADS] - lam * a[DIFF_HEADS:]
        o = o * lax.rsqrt(jnp.mean(o * o, axis=-1, keepdims=True) + NORM_EPS)
        o = o * sg_ref[...] * (1.0 - lam0)
        o_ref[...] = o * gate_ref[...]

    return ([score_page(p) for p in range(n_pages)] + [softmax]
            + [value_page(p) for p in range(n_pages)] + [finish])


N_PROMPT_IN = 7
N_DECODE_IN = 7


def _attn_kernel(pt_ref, *refs, n_pages, rows_per_page, blk, nq, lam0):
    del pt_ref
    prompt_in = refs[:N_PROMPT_IN]
    k_refs = refs[N_PROMPT_IN:N_PROMPT_IN + n_pages]
    v_refs = refs[N_PROMPT_IN + n_pages:N_PROMPT_IN + 2 * n_pages]
    decode_in = refs[N_PROMPT_IN + 2 * n_pages:N_PROMPT_IN + 2 * n_pages + N_DECODE_IN]
    o_ref, od_ref, qt_s, m_s, acc_s, sd_s, pd_s = refs[N_PROMPT_IN + 2 * n_pages + N_DECODE_IN:]
    lam_ref = prompt_in[-1]
    pieces = _attn_decode_pieces(k_refs, v_refs, *decode_in, lam_ref, od_ref, sd_s, pd_s,
                                 rows_per_page=rows_per_page, lam0=lam0)
    _attn_prompt_body(*prompt_in, o_ref, qt_s, m_s, acc_s, blk=blk, nq=nq, lam0=lam0,
                      side_work=pieces)


def _attention(qt_p, kb_p, vt_p, bias_p, gates_p, b, s, blk,
               dq, kb_new, vb_new, gate, cache_k, cache_v, page_table, bias, bias_new,
               subln_g, lam_rows, lam0, layer):
    nq = s // blk
    assert nq % 2 == 0 and nq >= 4
    bs = dq.shape[0]
    assert bs == b * DIFF_KV_HEADS * (nq // 2), "one sample sequence per prompt grid step"
    t = b * s
    gcol0 = RET_V_W // (DIFF_GROUP * DIFF_DV)
    n_pool, depth, page = cache_k.shape[0], cache_k.shape[1], cache_k.shape[2]
    n_pages = page_table.shape[1]
    w = page * DIFF_KV_HEADS
    ck = cache_k.reshape(n_pool, depth, w, 2 * DIFF_DH)
    cv = cache_v.reshape(n_pool, depth, w, DIFF_DV)
    q = (dq * (DIFF_DH ** -0.5)).reshape(bs, DIFF_KV_HEADS, DIFF_GROUP, 2, DIFF_DH)
    q = jnp.transpose(q, (0, 3, 1, 2, 4)).reshape(bs, 2, DIFF_HEADS, DIFF_DH)
    z = jnp.zeros_like(q[:, 0])
    qt = jnp.concatenate([jnp.concatenate([q[:, 0], z], axis=-1),
                          jnp.concatenate([z, q[:, 1]], axis=-1)], axis=1)
    kn = jnp.tile(jnp.repeat(kb_new.reshape(bs, DIFF_KV_HEADS, 2 * DIFF_DH), DIFF_GROUP, axis=1), (1, 2, 1))
    vn = jnp.tile(jnp.repeat(vb_new.reshape(bs, DIFF_KV_HEADS, DIFF_DV), DIFF_GROUP, axis=1), (1, 2, 1))
    g3 = gate.astype(F32).reshape(bs, DIFF_HEADS, DIFF_DV)
    rows = 2 * DIFF_HEADS

    npair = nq // 2

    def seq(i, h, p):
        return (i * DIFF_KV_HEADS + h) * npair + p

    def page_spec(pg):
        return pl.BlockSpec((None, None, w, 2 * DIFF_DH),
                            lambda i, h, p, pt: (pt[seq(i, h, p), pg], layer, 0, 0))

    per_seq = lambda i, h, p, pt: (i, h)
    const2 = lambda i, h, p, pt: (0, 0)
    row_spec = pl.BlockSpec((None, rows, LANES), lambda i, h, p, pt: (seq(i, h, p), 0, 0))
    head_spec = pl.BlockSpec((None, DIFF_HEADS, DIFF_DV), lambda i, h, p, pt: (seq(i, h, p), 0, 0))
    in_specs = ([pl.BlockSpec((nq, DIFF_GROUP * DIFF_DV, blk), lambda i, h, p, pt: (i, h, 0)),
                 pl.BlockSpec((s, 2 * DIFF_DH), per_seq),
                 pl.BlockSpec((nq, VT_ROWS, blk), lambda i, h, p, pt: (i, h, 0)),
                 pl.BlockSpec((3, DIFF_GROUP, blk, blk), lambda i, h, p, pt: (0, h, 0, 0)),
                 pl.BlockSpec((s, DIFF_GROUP * DIFF_DV), lambda i, h, p, pt: (i, gcol0 + h)),
                 pl.BlockSpec((DIFF_DV, 1), const2),
                 pl.BlockSpec((4, DIFF_DH), const2)]
                + [page_spec(pg) for pg in range(n_pages)] + [page_spec(pg) for pg in range(n_pages)]
                + [row_spec, row_spec, row_spec,
                   pl.BlockSpec((rows, n_pages * w), const2),
                   pl.BlockSpec((rows, LANES), const2),
                   head_spec,
                   pl.BlockSpec((1, DIFF_DV), const2)])
    assert len(in_specs) == N_PROMPT_IN + 2 * n_pages + N_DECODE_IN
    do_p, do_s = pl.pallas_call(
        functools.partial(_attn_kernel, n_pages=n_pages, rows_per_page=w, blk=blk, nq=nq, lam0=lam0),
        grid_spec=pltpu.PrefetchScalarGridSpec(
            num_scalar_prefetch=1, grid=(b, DIFF_KV_HEADS, npair), in_specs=in_specs,
            out_specs=[pl.BlockSpec((s, DIFF_GROUP * DIFF_DV), per_seq), head_spec],
            scratch_shapes=[pltpu.VMEM((2, DIFF_DV, 4 * blk), BF16),
                            pltpu.VMEM((2, 1, 4 * blk), F32),
                            pltpu.VMEM((2, VT_ROWS, 4 * blk), F32),
                            pltpu.VMEM((rows, n_pages * w), F32),
                            pltpu.VMEM((rows, n_pages * w), BF16)]),
        out_shape=[jax.ShapeDtypeStruct((t, DIFF_OUT_W), BF16),
                   jax.ShapeDtypeStruct((bs, DIFF_HEADS, DIFF_DV), F32)],
        compiler_params=_cparams(("arbitrary", "arbitrary", "arbitrary"), 56),
        name="attention",
    )(page_table, qt_p, kb_p, vt_p, bias_p, gates_p, subln_g.reshape(DIFF_DV, 1), lam_rows,
      *([ck] * n_pages), *([cv] * n_pages), qt, kn, vn, bias, bias_new, g3,
      subln_g.reshape(1, DIFF_DV))
    return do_p, do_s.reshape(bs, DIFF_OUT_W).astype(BF16)


def _softplus(x):
    return jnp.maximum(x, 0.0) + jnp.log1p(jnp.exp(-jnp.abs(x)))


def _lru_coeffs(xc, wg_ref, ba_ref, bx_ref, lam_ref, n):
    sl = slice(n * RNN_BLOCK_W, (n + 1) * RNN_BLOCK_W)
    xn = xc[:, sl]
    z = jnp.dot(xn.astype(BF16), wg_ref[n], preferred_element_type=F32)
    rg = _sigmoid(z[:, :RNN_BLOCK_W] + ba_ref[:, sl])
    ig = _sigmoid(z[:, RNN_BLOCK_W:] + bx_ref[:, sl])
    log_a = -LRU_C * rg * _softplus(-lam_ref[:, sl])
    a = jnp.exp(log_a)
    y = -jnp.tanh(log_a) * (1.0 + a * a)
    root = jnp.where(y > 0.0, y * lax.rsqrt(y), 0.0)
    bt = root * (ig * xn)
    return a, bt


def _lru_prompt_kernel(x_ref, gate_ref, cw_ref, cb_ref, wg_ref, ba_ref, bx_ref, lam_ref,
                       o_ref, hlast_ref, conv_ref, xbuf_ref, a_ref, b_ref, h_ref, *, tm):
    t = pl.program_id(1)
    pad = SUBLANES

    @pl.when(t == 0)
    def _():
        xbuf_ref[0:pad, :] = jnp.zeros((pad, RNN_WIDTH), F32)
        h_ref[...] = jnp.zeros(h_ref.shape, F32)

    xbuf_ref[pad:pad + tm, :] = x_ref[...]
    xc = cb_ref[...]
    for j in range(CONV_WIDTH):
        off = pad - (CONV_WIDTH - 1) + j
        xc = xc + xbuf_ref[off:off + tm, :] * cw_ref[j:j + 1, :]
    conv_ref[...] = xbuf_ref[pad + tm - (CONV_WIDTH - 1):pad + tm, :]
    xbuf_ref[0:pad, :] = xbuf_ref[tm:tm + pad, :]

    for n in range(RNN_BLOCKS):
        a, bt = _lru_coeffs(xc, wg_ref, ba_ref, bx_ref, lam_ref, n)
        sl = slice(n * RNN_BLOCK_W, (n + 1) * RNN_BLOCK_W)
        a = a.reshape(tm // SUBLANES, SUBLANES, RNN_BLOCK_W)
        bt = bt.reshape(tm // SUBLANES, SUBLANES, RNN_BLOCK_W)
        r8 = lax.broadcasted_iota(jnp.int32, a.shape, 1)
        for d in (1, 2, 4):
            a_sh = pltpu.roll(a, d, 1)
            b_sh = pltpu.roll(bt, d, 1)
            ok = r8 >= d
            bt = jnp.where(ok, a * b_sh + bt, bt)
            a = jnp.where(ok, a * a_sh, a)
        a_ref[:, sl] = a.reshape(tm, RNN_BLOCK_W)
        b_ref[:, sl] = bt.reshape(tm, RNN_BLOCK_W)

    h = h_ref[...]
    for gi in range(tm // SUBLANES):
        rs = slice(gi * SUBLANES, (gi + 1) * SUBLANES)
        hg = a_ref[rs, :] * h + b_ref[rs, :]
        b_ref[rs, :] = hg
        h = hg[SUBLANES - 1:SUBLANES, :]
    h_ref[...] = h
    hlast_ref[...] = h
    o_ref[...] = (b_ref[...] * gate_ref[...].astype(F32)).astype(o_ref.dtype)


def _lru_prompt(lx, gates, conv_w, conv_b, wg, ba, bx, lam, b, s, tm):
    nt = s // tm
    t = b * s
    gcol = 2
    vec = pl.BlockSpec((1, RNN_WIDTH), lambda i, j: (0, 0))
    return pl.pallas_call(
        functools.partial(_lru_prompt_kernel, tm=tm),
        grid=(b, nt),
        in_specs=[pl.BlockSpec((tm, RNN_WIDTH), lambda i, j: (i * nt + j, 0)),
                  pl.BlockSpec((tm, RNN_WIDTH), lambda i, j: (i * nt + j, gcol)),
                  pl.BlockSpec((CONV_WIDTH, RNN_WIDTH), lambda i, j: (0, 0)),
                  vec,
                  pl.BlockSpec((RNN_BLOCKS, RNN_BLOCK_W, 2 * RNN_BLOCK_W), lambda i, j: (0, 0, 0)),
                  vec, vec, vec],
        out_specs=[pl.BlockSpec((tm, RNN_WIDTH), lambda i, j: (i * nt + j, 0)),
                   pl.BlockSpec((None, 1, RNN_WIDTH), lambda i, j: (i, 0, 0)),
                   pl.BlockSpec((None, CONV_WIDTH - 1, RNN_WIDTH), lambda i, j: (i, 0, 0))],
        out_shape=[jax.ShapeDtypeStruct((t, RNN_WIDTH), BF16),
                   jax.ShapeDtypeStruct((b, 1, RNN_WIDTH), F32),
                   jax.ShapeDtypeStruct((b, CONV_WIDTH - 1, RNN_WIDTH), F32)],
        scratch_shapes=[pltpu.VMEM((tm + SUBLANES, RNN_WIDTH), F32),
                        pltpu.VMEM((tm, RNN_WIDTH), F32),
                        pltpu.VMEM((tm, RNN_WIDTH), F32),
                        pltpu.VMEM((1, RNN_WIDTH), F32)],
        compiler_params=_cparams(("parallel", "arbitrary"), 32),
        name="lru_prompt",
    )(lx, gates, conv_w, conv_b.reshape(1, -1), wg, ba.reshape(1, -1), bx.reshape(1, -1),
      lam.reshape(1, -1))


def _lru_step_kernel(x_ref, c0_ref, c1_ref, c2_ref, h0_ref, gate_ref, cw_ref, cb_ref, wg_ref,
                     ba_ref, bx_ref, lam_ref, o_ref, h_ref):
    xc = (cb_ref[...] + c0_ref[...] * cw_ref[0:1, :] + c1_ref[...] * cw_ref[1:2, :]
          + c2_ref[...] * cw_ref[2:3, :] + x_ref[...] * cw_ref[3:4, :])
    for n in range(RNN_BLOCKS):
        a, bt = _lru_coeffs(xc, wg_ref, ba_ref, bx_ref, lam_ref, n)
        sl = slice(n * RNN_BLOCK_W, (n + 1) * RNN_BLOCK_W)
        h = a * h0_ref[:, sl] + bt
        h_ref[:, sl] = h
        o_ref[:, sl] = (h * gate_ref[:, sl].astype(F32)).astype(o_ref.dtype)


def _lru_step(lx, conv_prev, h0, gate, conv_w, conv_b, wg, ba, bx, lam):
    bs = lx.shape[0]
    vm = pl.BlockSpec(memory_space=pltpu.VMEM)
    return pl.pallas_call(
        _lru_step_kernel,
        in_specs=[vm] * 12,
        out_specs=[vm, vm],
        out_shape=[jax.ShapeDtypeStruct((bs, RNN_WIDTH), BF16),
                   jax.ShapeDtypeStruct((bs, RNN_WIDTH), F32)],
        compiler_params=pltpu.CompilerParams(vmem_limit_bytes=32 * MIB),
        name="lru_step",
    )(lx, conv_prev[:, 0], conv_prev[:, 1], conv_prev[:, 2], h0, gate, conv_w,
      conv_b.reshape(1, -1), wg, ba.reshape(1, -1), bx.reshape(1, -1), lam.reshape(1, -1))


def _merge_kernel(ro_ref, do_ref, lo_ref, mg_ref, x_ref, p_ref, wr_ref, wd_ref, wl_ref, wo_ref,
                  wpg_ref, wp_ref, gn_ref, x_out_ref, hn_ref):
    d = D_MODEL
    tm = x_ref.shape[0]
    chunk = min(MERGE_CHUNK, tm)
    chunks = [slice(c * chunk, (c + 1) * chunk) for c in range(tm // chunk)]
    dot = functools.partial(jnp.dot, preferred_element_type=F32)
    merged = [(mg_ref[rs, 0:d].astype(F32) * dot(ro_ref[rs, :], wr_ref[...])
               + mg_ref[rs, d:2 * d].astype(F32) * dot(do_ref[rs, :], wd_ref[...])
               + mg_ref[rs, 2 * d:3 * d].astype(F32) * dot(lo_ref[rs, :], wl_ref[...])).astype(BF16)
              for rs in chunks]
    ple = [dot(p_ref[rs, :].astype(BF16), wp_ref[...]) for rs in chunks]
    x1 = [x_ref[rs, :] + dot(m, wo_ref[...]) for rs, m in zip(chunks, merged)]
    gate = [dot(v.astype(BF16), wpg_ref[...]) for v in x1]
    for rs, v, g, e in zip(chunks, x1, gate, ple):
        x2 = v + _sigmoid(g) * e
        x_out_ref[rs, :] = x2
        y = x2 * lax.rsqrt(jnp.mean(x2 * x2, axis=-1, keepdims=True) + NORM_EPS)
        hn_ref[rs, :] = (y * gn_ref[...]).astype(hn_ref.dtype)


MERGE_CHUNK = 256


def _merge(ro, do, lo, mg, x, p_all, layer, wr, wd, wl, wo, wpg, wp, g_next, hn_dtype, tm):
    t = x.shape[0]
    d = D_MODEL
    row = lambda wdt: pl.BlockSpec((tm, wdt), lambda i: (i, 0))
    full = lambda a: pl.BlockSpec(a.shape, lambda i: (0, 0), pipeline_mode=pl.Buffered(1))
    return pl.pallas_call(
        _merge_kernel,
        grid=(t // tm,),
        in_specs=[row(RET_V_W), row(DIFF_OUT_W), row(RNN_WIDTH), row(3 * d), row(d),
                  pl.BlockSpec((None, tm, D_PLE), lambda i: (layer, i, 0)),
                  full(wr), full(wd), full(wl), full(wo), full(wpg), full(wp),
                  pl.BlockSpec((1, d), lambda i: (0, 0))],
        out_specs=[row(d), row(d)],
        out_shape=[jax.ShapeDtypeStruct((t, d), F32), jax.ShapeDtypeStruct((t, d), hn_dtype)],
        compiler_params=_cparams(("parallel",), 56),
        name="merge",
    )(ro, do, lo, mg, x, p_all, wr, wd, wl, wo, wpg, wp, g_next.reshape(1, d))


def _lambda_init(layer):
    return 0.8 - 0.6 * math.exp(-0.3 * layer)


def _rope_tables(pos):
    half = RET_DK // 2
    inv = ROPE_BASE ** (-jnp.arange(half, dtype=F32) / half)
    ang = pos.astype(F32)[:, None] * inv[None, :]
    cos = jnp.cos(ang)
    sin = jnp.sin(ang)
    return jnp.concatenate([cos, cos], axis=-1), jnp.concatenate([-sin, sin], axis=-1)


def kernel(x_prompt, x_sample, cache_k, cache_v, state_ret, state_lru, state_conv, page_table, p_prompt, p_sample, rel_bias, norm_g, w_in, lambda_q1, lambda_k1, lambda_q2, lambda_k2, subln_g, conv_w, conv_b, gate_a_w, gate_a_b, gate_x_w, gate_x_b, lru_lambda, w_ret_out, w_diff_out, w_lru_out, w_o, w_ple, w_ple_gate, final_norm_g):
    bp, sp, d = x_prompt.shape
    bs, ss, _ = x_sample.shape
    assert d == D_MODEL and ss == 1
    depth = w_in.shape[0]
    n_pages, page = page_table.shape[1], cache_k.shape[2]
    past = n_pages * page
    tp = bp * sp
    blk = min(256, sp)
    tm_p = min(1024, sp)
    tm_qkv = min(512, sp)
    tm_lru = min(256, sp)
    tm_merge = min(512, sp)
    p_prompt_all = p_prompt.reshape(depth, tp, D_PLE)
    p_sample_all = p_sample.reshape(depth, bs, D_PLE)

    cos_p, sin_p = _rope_tables(jnp.arange(sp, dtype=jnp.int32))
    cos_s, sin_s = _rope_tables(jnp.full((bs,), past, dtype=jnp.int32))
    bias_p = _bias_prompt(rel_bias, blk)
    bias_s, bias_s_new = _bias_sample(rel_bias, past)

    w_in_b = w_in.astype(BF16)
    wg_all = jnp.concatenate([gate_a_w, gate_x_w], axis=-1).astype(BF16)
    xp = x_prompt.reshape(tp, d)
    xs = x_sample.reshape(bs, d)
    hn_p = _rmsnorm(xp, norm_g[0], BF16, tm_p)
    hn_s = _rmsnorm(xs, norm_g[0], BF16, bs)

    outs = {k: [] for k in ("kp", "vp", "ks", "vs", "rp", "rs", "lp", "ls", "cp", "cs")}
    for l in range(depth):
        wg = wg_all[l]
        wr, wd, wl = w_ret_out[l].astype(BF16), w_diff_out[l].astype(BF16), w_lru_out[l].astype(BF16)
        wo, wpg, wp = w_o[l].astype(BF16), w_ple_gate[l].astype(BF16), w_ple[l].astype(BF16)
        lam0 = _lambda_init(l)
        lam_rows = jnp.stack([lambda_q1[l], lambda_k1[l], lambda_q2[l], lambda_k2[l]], axis=0)
        last = l == depth - 1
        g_next = final_norm_g if last else norm_g[l + 1]
        hn_dtype = F32 if last else BF16

        qk = _proj(hn_p, w_in_b, l, COL_ROT, "rotary", tm_p, tables=(cos_p, sin_p))
        rv = _proj(hn_p, w_in_b, l, COL_RV, "plain", tm_p)
        gates = _proj(hn_p, w_in_b, l, COL_SILU, "silu", tm_p)
        kf, vf, kb, qt, vt = _proj_qkv_t(hn_p, w_in_b, l, tm_qkv, blk)
        lx = _proj(hn_p, w_in_b, l, COL_LX, "plain", tm_p, out_dtype=F32)
        mg = _proj(hn_p, w_in_b, l, COL_MG, "sigmoid", tm_p)
        qk_s = _proj(hn_s, w_in_b, l, COL_ROT, "rotary", bs, tables=(cos_s, sin_s))
        rvdq_s = _proj(hn_s, w_in_b, l, COL_RV_DQ, "plain", bs)
        gates_s = _proj(hn_s, w_in_b, l, COL_SILU, "silu", bs)
        kf_s, vf_s, kb_s, vb_s = _proj(hn_s, w_in_b, l, (COL_KV,), "kv", bs)
        lx_s = _proj(hn_s, w_in_b, l, COL_LX, "plain", bs, out_dtype=F32)
        mg_s = _proj(hn_s, w_in_b, l, COL_MG, "sigmoid", bs)

        do, do_s = _attention(qt, kb, vt, bias_p, gates, bp, sp, blk,
                              rvdq_s[:, RET_V_W:], kb_s, vb_s, gates_s[:, RET_V_W:RET_V_W + DIFF_OUT_W],
                              cache_k, cache_v, page_table, bias_s, bias_s_new, subln_g[l],
                              lam_rows, lam0, l)

        ro, s_fin = _retention_prompt(qk, rv, gates, bp, sp)
        lo, h_last, conv_new = _lru_prompt(lx, gates, conv_w[l], conv_b[l], wg, gate_a_b[l],
                                           gate_x_b[l], lru_lambda[l], bp, sp, tm_lru)
        xp, hn_p = _merge(ro, do, lo, mg, xp, p_prompt_all, l, wr, wd, wl, wo, wpg, wp,
                          g_next, hn_dtype, tm_merge)
        outs["kp"].append(kf.reshape(bp, sp, DIFF_KV_HEADS, 2 * DIFF_DH))
        outs["vp"].append(vf.reshape(bp, sp, DIFF_KV_HEADS, DIFF_DV))
        outs["rp"].append(s_fin)
        outs["lp"].append(h_last.reshape(bp, RNN_WIDTH))
        outs["cp"].append(conv_new)

        lx = lx_s
        ro, s_new = _retention_step(qk_s, rvdq_s, gates_s, state_ret, l)
        conv_prev = state_conv[:, l]
        lo, h_new = _lru_step(lx, conv_prev, state_lru[:, l], gates_s[:, 2 * RET_V_W:], conv_w[l],
                              conv_b[l], wg, gate_a_b[l], gate_x_b[l], lru_lambda[l])
        xs, hn_s = _merge(ro, do_s, lo, mg_s, xs, p_sample_all, l, wr, wd, wl, wo, wpg, wp,
                          g_next, hn_dtype, bs)
        outs["ks"].append(kf_s.reshape(bs, 1, DIFF_KV_HEADS, 2 * DIFF_DH))
        outs["vs"].append(vf_s.reshape(bs, 1, DIFF_KV_HEADS, DIFF_DV))
        outs["rs"].append(s_new)
        outs["ls"].append(h_new)
        outs["cs"].append(jnp.concatenate([conv_prev[:, 1:], lx[:, None, :]], axis=1))

    st = lambda key: jnp.stack(outs[key], axis=1)
    return (hn_p.reshape(bp, sp, d), hn_s.reshape(bs, 1, d), st("kp"), st("vp"), st("ks"), st("vs"),
            st("rp"), st("rs"), st("lp"), st("ls"), st("cp"), st("cs"))
```

```python
import functools
import math

import numpy as np
import jax
import jax.numpy as jnp
from jax import lax
from jax.experimental import pallas as pl
from jax.experimental.pallas import tpu as pltpu

F32 = jnp.float32
BF16 = jnp.bfloat16

D_MODEL = 1024
D_PLE = 256
NORM_EPS = 1e-6
RET_HEADS = 4
RET_DK = 128
RET_DV = 256
RET_CHUNK = 128
RET_CHUNKS_PER_STEP = 4
ROPE_BASE = 10000.0
DIFF_HEADS = 8
DIFF_KV_HEADS = 4
DIFF_GROUP = DIFF_HEADS // DIFF_KV_HEADS
DIFF_DH = 64
DIFF_DV = 2 * DIFF_DH
REL_BUCKETS = 32
REL_MAX_DIST = 128
RNN_WIDTH = 1024
RNN_BLOCKS = 8
RNN_BLOCK_W = RNN_WIDTH // RNN_BLOCKS
CONV_WIDTH = 4
LRU_C = 8.0

RET_QK_W = RET_HEADS * RET_DK
RET_V_W = RET_HEADS * RET_DV
DIFF_Q_W = DIFF_HEADS * 2 * DIFF_DH
DIFF_K_W = DIFF_KV_HEADS * 2 * DIFF_DH
DIFF_V_W = DIFF_KV_HEADS * DIFF_DV
DIFF_OUT_W = DIFF_HEADS * DIFF_DV
IN_SPLITS = (RET_QK_W, RET_QK_W, RET_V_W, RET_V_W,
             DIFF_Q_W, DIFF_K_W, DIFF_V_W, DIFF_OUT_W,
             RNN_WIDTH, RNN_WIDTH, 3 * D_MODEL)
IN_OFFS = tuple(int(v) for v in np.cumsum((0,) + IN_SPLITS))

LANES = 128
SUBLANES = 8
MASK_VALUE = -1e30
LOG2E = math.log2(math.e)
VT_ONES = 2 * SUBLANES
VT_ROWS = DIFF_DV + VT_ONES
BIAS_DIAG, BIAS_PREV, BIAS_ZERO = 0, 1, 2
MIB = 1024 * 1024


def _cparams(sem, vmem_mib):
    return pltpu.CompilerParams(dimension_semantics=sem, vmem_limit_bytes=int(vmem_mib * MIB))


def _rmsnorm_kernel(x_ref, g_ref, o_ref):
    x = x_ref[...]
    y = x * lax.rsqrt(jnp.mean(x * x, axis=-1, keepdims=True) + NORM_EPS)
    o_ref[...] = (y * g_ref[...]).astype(o_ref.dtype)


def _rmsnorm(x, g, out_dtype, tm):
    t, d = x.shape
    return pl.pallas_call(
        _rmsnorm_kernel,
        grid=(t // tm,),
        in_specs=[pl.BlockSpec((tm, d), lambda i: (i, 0)),
                  pl.BlockSpec((1, d), lambda i: (0, 0))],
        out_specs=pl.BlockSpec((tm, d), lambda i: (i, 0)),
        out_shape=jax.ShapeDtypeStruct((t, d), out_dtype),
        compiler_params=_cparams(("parallel",), 32),
        name="rmsnorm",
    )(x, g.reshape(1, d))


def _proj_qkv_t_kernel(x_ref, wq_ref, wkv_ref, k_all_ref, v_all_ref, kf_ref, vf_ref, kb_ref,
                       qt_ref, vt_ref):
    del k_all_ref, v_all_ref
    blk = qt_ref.shape[-1]

    def matmuls(j):
        x = x_ref[j * blk:(j + 1) * blk, :]
        return (jnp.dot(x, wq_ref[...], preferred_element_type=F32),
                jnp.dot(x, wkv_ref[...], preferred_element_type=F32))

    def epilogue(j, q, kv):
        rows = slice(j * blk, (j + 1) * blk)
        k = kv[:, :DIFF_K_W]
        v = kv[:, DIFF_K_W:]
        for h in range(DIFF_KV_HEADS):
            dst = pl.ds(j * blk * DIFF_KV_HEADS + h, blk, stride=DIFF_KV_HEADS)
            kf_ref[dst, :] = k[:, h * 2 * DIFF_DH:(h + 1) * 2 * DIFF_DH]
            vf_ref[dst, :] = v[:, h * DIFF_DV:(h + 1) * DIFF_DV]
        kb_ref[rows, :] = k.astype(BF16)
        qt_ref[j] = (q * (DIFF_DH ** -0.5 * LOG2E)).T.astype(BF16)
        vt = v.T.astype(BF16)
        for h in range(DIFF_KV_HEADS):
            vt_ref[j, h * VT_ROWS:h * VT_ROWS + DIFF_DV, :] = vt[h * DIFF_DV:(h + 1) * DIFF_DV]
            vt_ref[j, h * VT_ROWS + DIFF_DV:(h + 1) * VT_ROWS, :] = jnp.ones((VT_ONES, blk), BF16)

    _pipeline_rows(qt_ref.shape[0], matmuls, epilogue)


def _pipeline_rows(n_chunks, matmuls, epilogue):
    pending = {}
    for c in range(n_chunks + 1):
        if c < n_chunks:
            pending[c] = matmuls(c)
        if c >= 1:
            epilogue(c - 1, *pending.pop(c - 1))


def _sigmoid(x):
    return 0.5 * jnp.tanh(0.5 * x) + 0.5


def _proj_kernel(x_ref, w_ref, *refs, kind, chunk):
    tm = x_ref.shape[0]

    def matmuls(c):
        return (jnp.dot(x_ref[c * chunk:(c + 1) * chunk, :], w_ref[...], preferred_element_type=F32),)

    def epilogue(c, acc):
        rows = slice(c * chunk, (c + 1) * chunk)
        if kind == "plain":
            (o_ref,) = refs
            o_ref[rows, :] = acc.astype(o_ref.dtype)
        elif kind == "silu":
            (o_ref,) = refs
            o_ref[rows, :] = (acc * _sigmoid(acc)).astype(o_ref.dtype)
        elif kind == "sigmoid":
            (o_ref,) = refs
            o_ref[rows, :] = _sigmoid(acc).astype(o_ref.dtype)
        elif kind == "rotary":
            cos_ref, sin_ref, o_ref = refs
            cos = cos_ref[rows, :]
            sin = sin_ref[rows, :]
            for h in range(2 * RET_HEADS):
                z = acc[:, h * RET_DK:(h + 1) * RET_DK]
                r = z * cos + pltpu.roll(z, RET_DK // 2, 1) * sin
                if h >= RET_HEADS:
                    r = r * (RET_DK ** -0.5)
                o_ref[rows, h * RET_DK:(h + 1) * RET_DK] = r.astype(o_ref.dtype)
        elif kind == "kv":
            kf_ref, vf_ref, kb_ref, vb_ref = refs
            k = acc[:, :DIFF_K_W]
            v = acc[:, DIFF_K_W:]
            kf_ref[rows, :] = k
            vf_ref[rows, :] = v
            kb_ref[rows, :] = k.astype(BF16)
            vb_ref[rows, :] = v.astype(BF16)
        else:
            raise ValueError(kind)

    _pipeline_rows(tm // chunk, matmuls, epilogue)


PROJ_TN = 1024
PROJ_CHUNK = 256
COL_ROT = (0,)
COL_RV = (1,)
COL_RV_DQ = (1, 3)
COL_SILU = (2, 5, 7)
COL_DQ = 3
COL_KV = 4
COL_LX = (6,)
COL_MG = (8, 9, 10)
assert IN_OFFS[2] == PROJ_TN and IN_OFFS[5] == COL_KV * PROJ_TN and IN_OFFS[10] == COL_MG[0] * PROJ_TN


def _col_index(cols, j):
    if len(cols) == 1:
        return cols[0]
    step = cols[1] - cols[0]
    idx = cols[0] + step * j
    if cols[-1] != cols[0] + step * (len(cols) - 1):
        idx = idx - j // (len(cols) - 1)
    return idx


def _proj_qkv_t(hn, w_all, layer, tm, blk, k_all, v_all):
    t, d = hn.shape
    nj = tm // blk
    per_seq = k_all.shape[2] // (tm * DIFF_KV_HEADS)
    half = pl.BlockSpec((tm, DIFF_K_W), lambda i: (i, 0))
    per_head = pl.BlockSpec((None, None, tm * DIFF_KV_HEADS, DIFF_DV),
                            lambda i: (i // per_seq, layer, i % per_seq, 0))
    untouched = pl.BlockSpec(memory_space=pl.ANY)
    return pl.pallas_call(
        _proj_qkv_t_kernel,
        grid=(t // tm,),
        in_specs=[pl.BlockSpec((tm, d), lambda i: (i, 0)),
                  pl.BlockSpec((None, d, PROJ_TN), lambda i: (layer, 0, COL_DQ)),
                  pl.BlockSpec((None, d, PROJ_TN), lambda i: (layer, 0, COL_KV)),
                  untouched, untouched],
        out_specs=[per_head, per_head, half,
                   pl.BlockSpec((nj, DIFF_Q_W, blk), lambda i: (i, 0, 0)),
                   pl.BlockSpec((nj, DIFF_KV_HEADS * VT_ROWS, blk), lambda i: (i, 0, 0))],
        out_shape=[jax.ShapeDtypeStruct(k_all.shape, F32),
                   jax.ShapeDtypeStruct(v_all.shape, F32),
                   jax.ShapeDtypeStruct((t, DIFF_K_W), BF16),
                   jax.ShapeDtypeStruct((t // blk, DIFF_Q_W, blk), BF16),
                   jax.ShapeDtypeStruct((t // blk, DIFF_KV_HEADS * VT_ROWS, blk), BF16)],
        input_output_aliases={3: 0, 4: 1},
        compiler_params=_cparams(("parallel",), 48),
        name="proj_qkv_t",
    )(hn, w_all, w_all, k_all, v_all)


def _proj(hn, w_all, layer, cols, kind, tm, out_dtype=BF16, tables=None):
    t, d = hn.shape
    tn = PROJ_TN
    n = tn * len(cols)
    grid = (len(cols), t // tm)
    in_specs = [pl.BlockSpec((tm, d), lambda j, i: (i, 0)),
                pl.BlockSpec((None, d, tn), lambda j, i: (layer, 0, _col_index(cols, j)))]
    args = [hn, w_all]
    if kind == "rotary":
        cos, sin = tables
        nb = cos.shape[0] // tm
        spec = pl.BlockSpec((tm, RET_DK), lambda j, i: (i % nb, 0))
        in_specs += [spec, spec]
        args += [cos, sin]
    if kind == "kv":
        half = pl.BlockSpec((tm, DIFF_K_W), lambda j, i: (i, 0))
        out_specs = [half, half, half, half]
        out_shape = [jax.ShapeDtypeStruct((t, DIFF_K_W), F32),
                     jax.ShapeDtypeStruct((t, DIFF_V_W), F32),
                     jax.ShapeDtypeStruct((t, DIFF_K_W), BF16),
                     jax.ShapeDtypeStruct((t, DIFF_V_W), BF16)]
    else:
        out_specs = pl.BlockSpec((tm, tn), lambda j, i: (i, j))
        out_shape = jax.ShapeDtypeStruct((t, n), out_dtype)
    return pl.pallas_call(
        functools.partial(_proj_kernel, kind=kind, chunk=min(PROJ_CHUNK, tm)),
        grid=grid, in_specs=in_specs, out_specs=out_specs, out_shape=out_shape,
        compiler_params=_cparams(("parallel", "parallel"), 40),
        name="proj_" + kind,
    )(*args)


def _retention_chunk(ci, qk_ref, v_ref, gate_ref, dmask_ref, qdec_ref, kdec_ref, o_ref, state_ref,
                     chunk_dec):
    heads = range(RET_HEADS)
    dot = functools.partial(jnp.dot, preferred_element_type=F32)
    cw = dmask_ref.shape[-1]
    rows = slice(ci * cw, (ci + 1) * cw)
    q = [qk_ref[rows, h * RET_DK:(h + 1) * RET_DK] for h in heads]
    k = [qk_ref[rows, RET_QK_W + h * RET_DK:RET_QK_W + (h + 1) * RET_DK] for h in heads]
    v = [v_ref[rows, h * RET_DV:(h + 1) * RET_DV] for h in heads]
    s = [lax.dot_general(q[h], k[h], (((1,), (1,)), ((), ())), preferred_element_type=F32)
         for h in heads]
    cross = [dot(q[h], state_ref[h].astype(BF16)) for h in heads]
    kd = [(k[h].astype(F32) * kdec_ref[:, h:h + 1]).astype(BF16) for h in heads]
    kv = [lax.dot_general(kd[h], v[h], (((0,), (0,)), ((), ())), preferred_element_type=F32)
          for h in heads]
    inner = [dot((s[h] * dmask_ref[h]).astype(BF16), v[h]) for h in heads]
    for h in heads:
        state_ref[h] = state_ref[h] * chunk_dec[h] + kv[h]
        o = inner[h] + cross[h] * qdec_ref[:, h:h + 1]
        o = o * lax.rsqrt(jnp.mean(o * o, axis=-1, keepdims=True) + NORM_EPS)
        g = gate_ref[rows, h * RET_DV:(h + 1) * RET_DV].astype(F32)
        o_ref[rows, h * RET_DV:(h + 1) * RET_DV] = (o * g).astype(o_ref.dtype)


def _retention_decays(c):
    log_g = np.log1p(-(2.0 ** (-5.0 - np.arange(RET_HEADS, dtype=np.float64))))
    idx = np.arange(c, dtype=np.float64)
    dist = idx[:, None] - idx[None, :]
    dmask = np.where(dist[None] >= 0, np.exp(np.maximum(dist, 0.0)[None] * log_g[:, None, None]), 0.0)
    qdec = np.exp((idx + 1.0)[:, None] * log_g[None, :])
    kdec = np.exp((c - 1.0 - idx)[:, None] * log_g[None, :])
    chunk_dec = tuple(float(x) for x in np.exp(c * log_g).astype(np.float32))
    return jnp.asarray(dmask, F32), jnp.asarray(qdec, F32), jnp.asarray(kdec, F32), chunk_dec


def _retention_step_kernel(qk_ref, v_ref, gate_ref, s_ref, s_all_ref, o_ref, snew_ref, *, decay):
    del s_all_ref
    sb = qk_ref.shape[0]
    rowid = lax.broadcasted_iota(jnp.int32, (sb, 1), 0)

    def per_sequence(b, outs):
        new_outs = []
        for h in range(RET_HEADS):
            q = qk_ref[:, h * RET_DK:(h + 1) * RET_DK]
            k = qk_ref[:, RET_QK_W + h * RET_DK:RET_QK_W + (h + 1) * RET_DK]
            v = v_ref[:, h * RET_DV:(h + 1) * RET_DV]
            st = s_ref[b, h]
            qs = jnp.dot(q, st.astype(BF16), preferred_element_type=F32)
            new_outs.append(outs[h] + jnp.where(rowid == b, decay[h] * qs, 0.0))
            kb = jnp.where(rowid == b, k, jnp.zeros_like(k))
            snew_ref[b, h] = st * decay[h] + lax.dot_general(
                kb, v, (((0,), (0,)), ((), ())), preferred_element_type=F32)
        return tuple(new_outs)

    outs = []
    for h in range(RET_HEADS):
        q = qk_ref[:, h * RET_DK:(h + 1) * RET_DK].astype(F32)
        k = qk_ref[:, RET_QK_W + h * RET_DK:RET_QK_W + (h + 1) * RET_DK].astype(F32)
        v = v_ref[:, h * RET_DV:(h + 1) * RET_DV].astype(F32)
        outs.append(jnp.sum(q * k, axis=1, keepdims=True) * v)
    outs = lax.fori_loop(0, sb, per_sequence, tuple(outs))
    for h in range(RET_HEADS):
        o = outs[h]
        o = o * lax.rsqrt(jnp.mean(o * o, axis=-1, keepdims=True) + NORM_EPS)
        g = gate_ref[:, h * RET_DV:(h + 1) * RET_DV].astype(F32)
        o_ref[:, h * RET_DV:(h + 1) * RET_DV] = (o * g).astype(o_ref.dtype)


def _retention_step(qk, rvdq, gates, state_ret, state_new, layer):
    bs = qk.shape[0]
    sb = min(2 * SUBLANES, bs)
    log_g = np.log1p(-(2.0 ** (-5.0 - np.arange(RET_HEADS, dtype=np.float64))))
    decay = tuple(float(x) for x in np.exp(log_g).astype(np.float32))
    layer_state = pl.BlockSpec((sb, None, RET_HEADS, RET_DK, RET_DV), lambda i: (i, layer, 0, 0, 0))
    return pl.pallas_call(
        functools.partial(_retention_step_kernel, decay=decay),
        grid=(bs // sb,),
        in_specs=[pl.BlockSpec((sb, 2 * RET_QK_W), lambda i: (i, 0)),
                  pl.BlockSpec((sb, RET_V_W), lambda i: (i, 0)),
                  pl.BlockSpec((sb, RET_V_W), lambda i: (i, 0)),
                  layer_state,
                  pl.BlockSpec(memory_space=pl.ANY)],
        out_specs=[pl.BlockSpec((sb, RET_V_W), lambda i: (i, 0)), layer_state],
        out_shape=[jax.ShapeDtypeStruct((bs, RET_V_W), BF16),
                   jax.ShapeDtypeStruct(state_new.shape, F32)],
        input_output_aliases={4: 1},
        compiler_params=_cparams(("parallel",), 48),
        name="retention_step",
    )(qk, rvdq, gates, state_ret, state_new)


def _t5_bucket(rel):
    n = jnp.maximum(rel, 0)
    max_exact = REL_BUCKETS // 2
    nf = jnp.maximum(n.astype(F32), 1.0)
    large = max_exact + (jnp.log(nf / max_exact) / math.log(REL_MAX_DIST / max_exact)
                         * (REL_BUCKETS - max_exact)).astype(jnp.int32)
    large = jnp.minimum(large, REL_BUCKETS - 1)
    return jnp.where(n < max_exact, n, large)


def _bias_lookup(tab_ref, bucket, head):
    out = jnp.zeros(bucket.shape, F32)
    far = tab_ref[REL_BUCKETS - 1, head]
    for bkt in range(REL_BUCKETS - 1):
        out = jnp.where(bucket == bkt, tab_ref[bkt, head] - far, out)
    return out


def _bias_prompt_kernel(tab_ref, o_ref, *, blk):
    i = lax.broadcasted_iota(jnp.int32, (blk, blk), 1)
    j = lax.broadcasted_iota(jnp.int32, (blk, blk), 0)
    for which in (BIAS_DIAG, BIAS_PREV):
        rel = i - j + which * blk
        bucket = _t5_bucket(rel)
        for h in range(DIFF_HEADS):
            o_ref[which, h] = jnp.where(rel >= 0, _bias_lookup(tab_ref, bucket, h) * LOG2E, MASK_VALUE)
    o_ref[BIAS_ZERO] = jnp.zeros(o_ref.shape[1:], F32)


def _bias_prompt(rel_bias, blk):
    return pl.pallas_call(
        functools.partial(_bias_prompt_kernel, blk=blk),
        in_specs=[pl.BlockSpec(memory_space=pltpu.SMEM)],
        out_specs=pl.BlockSpec(memory_space=pltpu.VMEM),
        out_shape=jax.ShapeDtypeStruct((3, DIFF_HEADS, blk, blk), F32),
        compiler_params=pltpu.CompilerParams(vmem_limit_bytes=32 * MIB),
        name="bias_prompt",
    )(rel_bias)


def _bias_sample_kernel(tab_ref, o_ref, onew_ref, *, past, width):
    rows = 2 * DIFF_HEADS
    r = lax.broadcasted_iota(jnp.int32, (rows, width), 0)
    lane = lax.broadcasted_iota(jnp.int32, (rows, width), 1)
    head = r % DIFF_HEADS
    kvh = lane % DIFF_KV_HEADS
    pos = lane // DIFF_KV_HEADS
    bucket = _t5_bucket(past - pos)
    out = jnp.zeros((rows, width), F32)
    for h in range(DIFF_HEADS):
        out = jnp.where(head == h, _bias_lookup(tab_ref, bucket, h), out)
    o_ref[...] = jnp.where(kvh == head // DIFF_GROUP, out, MASK_VALUE)
    rn = lax.broadcasted_iota(jnp.int32, (rows, LANES), 0) % DIFF_HEADS
    new = jnp.zeros((rows, LANES), F32)
    zero_bucket = _t5_bucket(jnp.zeros((rows, LANES), jnp.int32))
    for h in range(DIFF_HEADS):
        new = jnp.where(rn == h, _bias_lookup(tab_ref, zero_bucket, h), new)
    onew_ref[...] = new


def _bias_sample(rel_bias, past):
    width = past * DIFF_KV_HEADS
    return pl.pallas_call(
        functools.partial(_bias_sample_kernel, past=past, width=width),
        in_specs=[pl.BlockSpec(memory_space=pltpu.SMEM)],
        out_specs=[pl.BlockSpec(memory_space=pltpu.VMEM), pl.BlockSpec(memory_space=pltpu.VMEM)],
        out_shape=[jax.ShapeDtypeStruct((2 * DIFF_HEADS, width), F32),
                   jax.ShapeDtypeStruct((2 * DIFF_HEADS, LANES), F32)],
        compiler_params=pltpu.CompilerParams(vmem_limit_bytes=32 * MIB),
        name="bias_sample",
    )(rel_bias)


def _lambda_full(lam_ref, lam0):
    a = jnp.sum(lam_ref[0:1, :] * lam_ref[1:2, :], axis=-1, keepdims=True)
    b = jnp.sum(lam_ref[2:3, :] * lam_ref[3:4, :], axis=-1, keepdims=True)
    return jnp.exp(a) - jnp.exp(b) + lam0


def _attn_prompt_body(qt_in_ref, k_ref, vt_ref, bias_ref, gate_ref, sg_ref, lam_ref, o_ref,
                      qt_ref, m_ref, acc_ref, *, blk, nq, lam0, side_work=()):
    pair = pl.program_id(2)
    tiles = (pair, nq - 1 - pair)
    n_chunks = 2 * DIFF_GROUP
    row = lax.broadcasted_iota(jnp.int32, (DIFF_DV, blk), 0)
    for slot in range(2):
        for g in range(DIFF_GROUP):
            qg = qt_in_ref[tiles[slot], g * DIFF_DV:(g + 1) * DIFF_DV, :]
            for m in range(2):
                keep = (row < DIFF_DH) if m == 0 else (row >= DIFF_DH)
                c = m * DIFF_GROUP + g
                qt_ref[slot, :, c * blk:(c + 1) * blk] = jnp.where(keep, qg, jnp.zeros_like(qg))
    m_ref[...] = jnp.full(m_ref.shape, MASK_VALUE, F32)
    acc_ref[...] = jnp.zeros(acc_ref.shape, F32)

    def run_blocks(blocks):
        s_lag = 3
        kbs =[k_ref[pl.ds(pl.multiple_of(ki * blk, blk), blk), :] for _, ki, _ in blocks]
        vts = [vt_ref[ki] for _, ki, _ in blocks]
        units = [(bi, c) for bi in range(len(blocks)) for c in range(n_chunks)]
        n_units = len(units)

        def scores(u):
            bi, c = units[u]
            slot, _, bias = blocks[bi]
            s = jnp.dot(kbs[bi], qt_ref[slot, :, c * blk:(c + 1) * blk], preferred_element_type=F32)
            if bias is not None:
                s = s + bias_ref[bias, c % DIFF_GROUP]
            return s

        def softmax(u, s):
            bi, c = units[u]
            slot = blocks[bi][0]
            cs = slice(c * blk, (c + 1) * blk)
            m_prev = m_ref[slot, :, cs]
            m_new = jnp.maximum(m_prev, jnp.max(s, axis=0, keepdims=True))
            p = jnp.exp2(s - m_new)
            alpha = jnp.exp2(m_prev - m_new)
            m_ref[slot, :, cs] = m_new
            return p.astype(BF16), alpha

        def accumulate(u, p, alpha):
            bi, c = units[u]
            slot = blocks[bi][0]
            cs = slice(c * blk, (c + 1) * blk)
            acc_ref[slot, :, cs] = acc_ref[slot, :, cs] * alpha + jnp.dot(
                vts[bi], p, preferred_element_type=F32)

        s_vals, p_vals = {}, {}
        n_steps = n_units + s_lag + 1
        for t in range(n_steps):
            if t < n_units:
                s_vals[t] = scores(t)
            if 0 <= t - s_lag < n_units:
                p_vals[t - s_lag] = softmax(t - s_lag, s_vals.pop(t - s_lag))
            if 0 <= t - s_lag - 1 < n_units:
                accumulate(t - s_lag - 1, *p_vals.pop(t - s_lag - 1))
            lo, hi = (t * len(side_work)) // n_steps, ((t + 1) * len(side_work)) // n_steps
            for piece in side_work[lo:hi]:
                piece()

    n_b = jnp.minimum(nq - 2 - pair, nq - 3)
    blocks = []
    for j in range(nq - 3):
        if j < nq // 2 - 1:
            blocks.append((1, j, None))
        else:
            in_b = j < n_b
            blocks.append((jnp.where(in_b, 1, 0), jnp.where(in_b, j, j - n_b), None))
    first = pair == 0
    blocks.append((jnp.where(first, 1, 0), jnp.where(first, nq - 3, pair - 1),
                   jnp.where(first, BIAS_ZERO, BIAS_PREV)))
    blocks.append((0, pair, BIAS_DIAG))
    blocks.append((1, nq - 2 - pair, BIAS_PREV))
    blocks.append((1, nq - 1 - pair, BIAS_DIAG))
    run_blocks(blocks)

    lam = _lambda_full(lam_ref, lam0)
    for slot in range(2):
        rows = pl.ds(pl.multiple_of(tiles[slot] * blk, blk), blk)
        a = acc_ref[slot, 0:DIFF_DV, :] / acc_ref[slot, DIFF_DV:DIFF_DV + 1, :]
        for g in range(DIFF_GROUP):
            ot = a[:, g * blk:(g + 1) * blk] - lam * a[:, (DIFF_GROUP + g) * blk:(DIFF_GROUP + g + 1) * blk]
            ot = ot * lax.rsqrt(jnp.mean(ot * ot, axis=0, keepdims=True) + NORM_EPS)
            ot = ot * sg_ref[...] * (1.0 - lam0)
            gate = gate_ref[rows, g * DIFF_DV:(g + 1) * DIFF_DV].astype(F32)
            o_ref[rows, g * DIFF_DV:(g + 1) * DIFF_DV] = (ot.T * gate).astype(o_ref.dtype)


def _attn_decode_pieces(k_refs, v_refs, qt_ref, kn_ref, vn_ref, bias_ref, bnew_ref, gate_ref,
                        sg_ref, lam_ref, o_ref, s_ref, p_ref, *, rows_per_page, lam0):
    n_pages = len(k_refs)
    w = rows_per_page
    state = {}

    def score_page(p):
        def run():
            kp = k_refs[p][...].astype(BF16)
            sp = lax.dot_general(qt_ref[...], kp, (((1,), (1,)), ((), ())),
                                 preferred_element_type=F32)
            s_ref[:, p * w:(p + 1) * w] = sp + bias_ref[:, p * w:(p + 1) * w]
        return run

    def softmax():
        s = s_ref[...]
        s_new = jnp.sum(qt_ref[...].astype(F32) * kn_ref[...].astype(F32), axis=1, keepdims=True)
        s_new = s_new + bnew_ref[:, 0:1]
        m = jnp.maximum(jnp.max(s, axis=1, keepdims=True), s_new)
        e_new = jnp.exp(s_new - m)
        pr = jnp.exp(s - m)
        state["l"] = jnp.sum(pr, axis=1, keepdims=True) + e_new
        state["acc"] = e_new * vn_ref[...].astype(F32)
        p_ref[...] = pr.astype(BF16)

    def value_page(p):
        def run():
            vp = v_refs[p][...].astype(BF16)
            state["acc"] = state["acc"] + jnp.dot(p_ref[:, p * w:(p + 1) * w], vp,
                                                  preferred_element_type=F32)
        return run

    def finish():
        a = state["acc"] / state["l"]
        lam = _lambda_full(lam_ref, lam0)
        o = a[:DIFF_HEADS] - lam * a[DIFF_HEADS:]
        o = o * lax.rsqrt(jnp.mean(o * o, axis=-1, keepdims=True) + NORM_EPS)
        o = o * sg_ref[...] * (1.0 - lam0)
        o_ref[...] = o * gate_ref[...]

    return ([score_page(p) for p in range(n_pages)] + [softmax]
            + [value_page(p) for p in range(n_pages)] + [finish])


N_PROMPT_IN = 7
N_DECODE_IN = 7


def _attn_kernel(pt_ref, *refs, n_pages, rows_per_page, blk, nq, lam0):
    del pt_ref
    prompt_in = refs[:N_PROMPT_IN]
    k_refs = refs[N_PROMPT_IN:N_PROMPT_IN + n_pages]
    v_refs = refs[N_PROMPT_IN + n_pages:N_PROMPT_IN + 2 * n_pages]
    decode_in = refs[N_PROMPT_IN + 2 * n_pages:N_PROMPT_IN + 2 * n_pages + N_DECODE_IN]
    o_ref, od_ref, qt_s, m_s, acc_s, sd_s, pd_s = refs[N_PROMPT_IN + 2 * n_pages + N_DECODE_IN:]
    lam_ref = prompt_in[-1]
    pieces = _attn_decode_pieces(k_refs, v_refs, *decode_in, lam_ref, od_ref, sd_s, pd_s,
                                 rows_per_page=rows_per_page, lam0=lam0)
    _attn_prompt_body(*prompt_in, o_ref, qt_s, m_s, acc_s, blk=blk, nq=nq, lam0=lam0,
                      side_work=pieces)


def _attention(qt_p, kb_p, vt_p, bias_p, gates_p, b, s, blk,
               dq, kb_new, vb_new, gate, cache_k, cache_v, page_table, bias, bias_new,
               subln_g, lam_rows, lam0, layer):
    nq = s // blk
    assert nq % 2 == 0 and nq >= 4
    bs = dq.shape[0]
    assert bs == b * DIFF_KV_HEADS * (nq // 2), "one sample sequence per prompt grid step"
    t = b * s
    gcol0 = RET_V_W // (DIFF_GROUP * DIFF_DV)
    n_pool, depth, page = cache_k.shape[0], cache_k.shape[1], cache_k.shape[2]
    n_pages = page_table.shape[1]
    w = page * DIFF_KV_HEADS
    ck = cache_k.reshape(n_pool, depth, w, 2 * DIFF_DH)
    cv = cache_v.reshape(n_pool, depth, w, DIFF_DV)
    q = (dq * (DIFF_DH ** -0.5)).reshape(bs, DIFF_KV_HEADS, DIFF_GROUP, 2, DIFF_DH)
    q = jnp.transpose(q, (0, 3, 1, 2, 4)).reshape(bs, 2, DIFF_HEADS, DIFF_DH)
    z = jnp.zeros_like(q[:, 0])
    qt = jnp.concatenate([jnp.concatenate([q[:, 0], z], axis=-1),
                          jnp.concatenate([z, q[:, 1]], axis=-1)], axis=1)
    kn = jnp.tile(jnp.repeat(kb_new.reshape(bs, DIFF_KV_HEADS, 2 * DIFF_DH), DIFF_GROUP, axis=1), (1, 2, 1))
    vn = jnp.tile(jnp.repeat(vb_new.reshape(bs, DIFF_KV_HEADS, DIFF_DV), DIFF_GROUP, axis=1), (1, 2, 1))
    g3 = gate.astype(F32).reshape(bs, DIFF_HEADS, DIFF_DV)
    rows = 2 * DIFF_HEADS

    npair = nq // 2

    def seq(i, h, p):
        return (i * DIFF_KV_HEADS + h) * npair + p

    def page_spec(pg):
        return pl.BlockSpec((None, None, w, 2 * DIFF_DH),
                            lambda i, h, p, pt: (pt[seq(i, h, p), pg], layer, 0, 0))

    per_seq = lambda i, h, p, pt: (i, h)
    const2 = lambda i, h, p, pt: (0, 0)
    row_spec = pl.BlockSpec((None, rows, LANES), lambda i, h, p, pt: (seq(i, h, p), 0, 0))
    head_spec = pl.BlockSpec((None, DIFF_HEADS, DIFF_DV), lambda i, h, p, pt: (seq(i, h, p), 0, 0))
    in_specs = ([pl.BlockSpec((nq, DIFF_GROUP * DIFF_DV, blk), lambda i, h, p, pt: (i, h, 0)),
                 pl.BlockSpec((s, 2 * DIFF_DH), per_seq),
                 pl.BlockSpec((nq, VT_ROWS, blk), lambda i, h, p, pt: (i, h, 0)),
                 pl.BlockSpec((3, DIFF_GROUP, blk, blk), lambda i, h, p, pt: (0, h, 0, 0)),
                 pl.BlockSpec((s, DIFF_GROUP * DIFF_DV), lambda i, h, p, pt: (i, gcol0 + h)),
                 pl.BlockSpec((DIFF_DV, 1), const2),
                 pl.BlockSpec((4, DIFF_DH), const2)]
                + [page_spec(pg) for pg in range(n_pages)] + [page_spec(pg) for pg in range(n_pages)]
                + [row_spec, row_spec, row_spec,
                   pl.BlockSpec((rows, n_pages * w), const2),
                   pl.BlockSpec((rows, LANES), const2),
                   head_spec,
                   pl.BlockSpec((1, DIFF_DV), const2)])
    assert len(in_specs) == N_PROMPT_IN + 2 * n_pages + N_DECODE_IN
    do_p, do_s = pl.pallas_call(
        functools.partial(_attn_kernel, n_pages=n_pages, rows_per_page=w, blk=blk, nq=nq, lam0=lam0),
        grid_spec=pltpu.PrefetchScalarGridSpec(
            num_scalar_prefetch=1, grid=(b, DIFF_KV_HEADS, npair), in_specs=in_specs,
            out_specs=[pl.BlockSpec((s, DIFF_GROUP * DIFF_DV), per_seq), head_spec],
            scratch_shapes=[pltpu.VMEM((2, DIFF_DV, 4 * blk), BF16),
                            pltpu.VMEM((2, 1, 4 * blk), F32),
                            pltpu.VMEM((2, VT_ROWS, 4 * blk), F32),
                            pltpu.VMEM((rows, n_pages * w), F32),
                            pltpu.VMEM((rows, n_pages * w), BF16)]),
        out_shape=[jax.ShapeDtypeStruct((t, DIFF_OUT_W), BF16),
                   jax.ShapeDtypeStruct((bs, DIFF_HEADS, DIFF_DV), F32)],
        compiler_params=_cparams(("arbitrary", "arbitrary", "arbitrary"), 56),
        name="attention",
    )(page_table, qt_p, kb_p, vt_p, bias_p, gates_p, subln_g.reshape(DIFF_DV, 1), lam_rows,
      *([ck] * n_pages), *([cv] * n_pages), qt, kn, vn, bias, bias_new, g3,
      subln_g.reshape(1, DIFF_DV))
    return do_p, do_s.reshape(bs, DIFF_OUT_W).astype(BF16)


def _softplus(x):
    return jnp.maximum(x, 0.0) + jnp.log1p(jnp.exp(-jnp.abs(x)))


def _lru_coeffs(xc, wg_ref, ba_ref, bx_ref, lam_ref, n):
    sl = slice(n * RNN_BLOCK_W, (n + 1) * RNN_BLOCK_W)
    xn = xc[:, sl]
    z = jnp.dot(xn.astype(BF16), wg_ref[n], preferred_element_type=F32)
    rg = _sigmoid(z[:, :RNN_BLOCK_W] + ba_ref[:, sl])
    ig = _sigmoid(z[:, RNN_BLOCK_W:] + bx_ref[:, sl])
    log_a = -LRU_C * rg * _softplus(-lam_ref[:, sl])
    a = jnp.exp(log_a)
    y = -jnp.tanh(log_a) * (1.0 + a * a)
    root = jnp.where(y > 0.0, y * lax.rsqrt(y), 0.0)
    bt = root * (ig * xn)
    return a, bt


def _lru_tile(ti, tm, x_ref, gate_ref, cw_ref, cb_ref, wg_ref, ba_ref, bx_ref, lam_ref,
              o_ref, conv_ref, xbuf_ref, a_ref, b_ref, h_ref):
    pad = SUBLANES
    rows = slice(ti * tm, (ti + 1) * tm)
    xbuf_ref[pad:pad + tm, :] = x_ref[rows, :]
    xc = cb_ref[...]
    for j in range(CONV_WIDTH):
        off = pad - (CONV_WIDTH - 1) + j
        xc = xc + xbuf_ref[off:off + tm, :] * cw_ref[j:j + 1, :]
    conv_ref[...] = xbuf_ref[pad + tm - (CONV_WIDTH - 1):pad + tm, :]
    xbuf_ref[0:pad, :] = xbuf_ref[tm:tm + pad, :]

    for n in range(RNN_BLOCKS):
        a, bt = _lru_coeffs(xc, wg_ref, ba_ref, bx_ref, lam_ref, n)
        sl = slice(n * RNN_BLOCK_W, (n + 1) * RNN_BLOCK_W)
        a = a.reshape(tm // SUBLANES, SUBLANES, RNN_BLOCK_W)
        bt = bt.reshape(tm // SUBLANES, SUBLANES, RNN_BLOCK_W)
        r8 = lax.broadcasted_iota(jnp.int32, a.shape, 1)
        for d in (1, 2, 4):
            a_sh = pltpu.roll(a, d, 1)
            b_sh = pltpu.roll(bt, d, 1)
            ok = r8 >= d
            bt = jnp.where(ok, a * b_sh + bt, bt)
            a = jnp.where(ok, a * a_sh, a)
        a_ref[:, sl] = a.reshape(tm, RNN_BLOCK_W)
        b_ref[:, sl] = bt.reshape(tm, RNN_BLOCK_W)

    h = h_ref[...]
    for gi in range(tm // SUBLANES):
        rs = slice(gi * SUBLANES, (gi + 1) * SUBLANES)
        hg = a_ref[rs, :] * h + b_ref[rs, :]
        b_ref[rs, :] = hg
        h = hg[SUBLANES - 1:SUBLANES, :]
    h_ref[...] = h
    o_ref[rows, :] = (b_ref[...] * gate_ref[rows, :].astype(F32)).astype(o_ref.dtype)


def _ret_lru_kernel(qk_ref, v_ref, gr_ref, dmask_ref, qdec_ref, kdec_ref,
                    x_ref, gl_ref, cw_ref, cb_ref, wg_ref, ba_ref, bx_ref, lam_ref,
                    o_ret_ref, sfin_ref, o_lru_ref, hlast_ref, conv_ref,
                    state_ref, xbuf_ref, a_ref, b_ref, h_ref, *, chunk_dec, tm):
    j = pl.program_id(1)

    @pl.when(j == 0)
    def _():
        state_ref[...] = jnp.zeros_like(state_ref)
        xbuf_ref[0:SUBLANES, :] = jnp.zeros((SUBLANES, RNN_WIDTH), F32)
        h_ref[...] = jnp.zeros(h_ref.shape, F32)

    rows = qk_ref.shape[0]
    n_chunks = rows // dmask_ref.shape[-1]
    n_tiles = rows // tm
    ret = [functools.partial(_retention_chunk, ci, qk_ref, v_ref, gr_ref, dmask_ref, qdec_ref,
                             kdec_ref, o_ret_ref, state_ref, chunk_dec) for ci in range(n_chunks)]
    lru = [functools.partial(_lru_tile, ti, tm, x_ref, gl_ref, cw_ref, cb_ref, wg_ref, ba_ref, bx_ref,
                             lam_ref, o_lru_ref, conv_ref, xbuf_ref, a_ref, b_ref, h_ref)
           for ti in range(n_tiles)]
    per_tile = n_chunks // n_tiles
    for ti in range(n_tiles):
        ret[ti * per_tile]()
        lru[ti]()
        for piece in ret[ti * per_tile + 1:(ti + 1) * per_tile]:
            piece()
    hlast_ref[...] = h_ref[...]

    @pl.when(j == pl.num_programs(1) - 1)
    def _():
        sfin_ref[...] = state_ref[...]


def _ret_lru_prompt(qk, rv, gates, lx, conv_w, conv_b, wg, ba, bx, lam, b, s, tm):
    c = math.gcd(s, RET_CHUNK)
    dmask, qdec, kdec, chunk_dec = _retention_decays(c)
    rows = c * math.gcd(s // c, RET_CHUNKS_PER_STEP)
    assert rows % tm == 0 and (rows // c) % (rows // tm) == 0
    nt = s // rows
    t = b * s
    lcol = 2
    row_blk = lambda w, col=0: pl.BlockSpec((rows, w), lambda i, j: (i * nt + j, col))
    const2 = lambda shape: pl.BlockSpec(shape, lambda i, j: (0, 0))
    vec = const2((1, RNN_WIDTH))
    return pl.pallas_call(
        functools.partial(_ret_lru_kernel, chunk_dec=chunk_dec, tm=tm),
        grid=(b, nt),
        in_specs=[row_blk(2 * RET_QK_W), row_blk(RET_V_W), row_blk(RET_V_W),
                  pl.BlockSpec((RET_HEADS, c, c), lambda i, j: (0, 0, 0)),
                  const2((c, RET_HEADS)), const2((c, RET_HEADS)),
                  row_blk(RNN_WIDTH), row_blk(RNN_WIDTH, lcol),
                  const2((CONV_WIDTH, RNN_WIDTH)), vec,
                  pl.BlockSpec((RNN_BLOCKS, RNN_BLOCK_W, 2 * RNN_BLOCK_W), lambda i, j: (0, 0, 0)),
                  vec, vec, vec],
        out_specs=[row_blk(RET_V_W),
                   pl.BlockSpec((None, RET_HEADS, RET_DK, RET_DV), lambda i, j: (i, 0, 0, 0)),
                   row_blk(RNN_WIDTH),
                   pl.BlockSpec((None, 1, RNN_WIDTH), lambda i, j: (i, 0, 0)),
                   pl.BlockSpec((None, CONV_WIDTH - 1, RNN_WIDTH), lambda i, j: (i, 0, 0))],
        out_shape=[jax.ShapeDtypeStruct((t, RET_V_W), BF16),
                   jax.ShapeDtypeStruct((b, RET_HEADS, RET_DK, RET_DV), F32),
                   jax.ShapeDtypeStruct((t, RNN_WIDTH), BF16),
                   jax.ShapeDtypeStruct((b, 1, RNN_WIDTH), F32),
                   jax.ShapeDtypeStruct((b, CONV_WIDTH - 1, RNN_WIDTH), F32)],
        scratch_shapes=[pltpu.VMEM((RET_HEADS, RET_DK, RET_DV), F32),
                        pltpu.VMEM((tm + SUBLANES, RNN_WIDTH), F32),
                        pltpu.VMEM((tm, RNN_WIDTH), F32),
                        pltpu.VMEM((tm, RNN_WIDTH), F32),
                        pltpu.VMEM((1, RNN_WIDTH), F32)],
        compiler_params=_cparams(("parallel", "arbitrary"), 40),
        name="ret_lru_prompt",
    )(qk, rv, gates, dmask, qdec, kdec, lx, gates, conv_w, conv_b.reshape(1, -1), wg,
      ba.reshape(1, -1), bx.reshape(1, -1), lam.reshape(1, -1))


def _lru_step_kernel(x_ref, c0_ref, c1_ref, c2_ref, h0_ref, gate_ref, cw_ref, cb_ref, wg_ref,
                     ba_ref, bx_ref, lam_ref, o_ref, h_ref):
    xc = (cb_ref[...] + c0_ref[...] * cw_ref[0:1, :] + c1_ref[...] * cw_ref[1:2, :]
          + c2_ref[...] * cw_ref[2:3, :] + x_ref[...] * cw_ref[3:4, :])
    for n in range(RNN_BLOCKS):
        a, bt = _lru_coeffs(xc, wg_ref, ba_ref, bx_ref, lam_ref, n)
        sl = slice(n * RNN_BLOCK_W, (n + 1) * RNN_BLOCK_W)
        h = a * h0_ref[:, sl] + bt
        h_ref[:, sl] = h
        o_ref[:, sl] = (h * gate_ref[:, sl].astype(F32)).astype(o_ref.dtype)


def _lru_step(lx, conv_prev, h0, gate, conv_w, conv_b, wg, ba, bx, lam):
    bs = lx.shape[0]
    vm = pl.BlockSpec(memory_space=pltpu.VMEM)
    return pl.pallas_call(
        _lru_step_kernel,
        in_specs=[vm] * 12,
        out_specs=[vm, vm],
        out_shape=[jax.ShapeDtypeStruct((bs, RNN_WIDTH), BF16),
                   jax.ShapeDtypeStruct((bs, RNN_WIDTH), F32)],
        compiler_params=pltpu.CompilerParams(vmem_limit_bytes=32 * MIB),
        name="lru_step",
    )(lx, conv_prev[:, 0], conv_prev[:, 1], conv_prev[:, 2], h0, gate, conv_w,
      conv_b.reshape(1, -1), wg, ba.reshape(1, -1), bx.reshape(1, -1), lam.reshape(1, -1))


def _merge_kernel(ro_ref, do_ref, lo_ref, mg_ref, x_ref, p_ref, wr_ref, wd_ref, wl_ref, wo_ref,
                  wpg_ref, wp_ref, gn_ref, x_out_ref, hn_ref):
    d = D_MODEL
    tm = x_ref.shape[0]
    chunk = min(MERGE_CHUNK, tm)
    chunks = [slice(c * chunk, (c + 1) * chunk) for c in range(tm // chunk)]
    dot = functools.partial(jnp.dot, preferred_element_type=F32)
    merged = [(mg_ref[rs, 0:d].astype(F32) * dot(ro_ref[rs, :], wr_ref[...])
               + mg_ref[rs, d:2 * d].astype(F32) * dot(do_ref[rs, :], wd_ref[...])
               + mg_ref[rs, 2 * d:3 * d].astype(F32) * dot(lo_ref[rs, :], wl_ref[...])).astype(BF16)
              for rs in chunks]
    ple = [dot(p_ref[rs, :].astype(BF16), wp_ref[...]) for rs in chunks]
    x1 = [x_ref[rs, :] + dot(m, wo_ref[...]) for rs, m in zip(chunks, merged)]
    gate = [dot(v.astype(BF16), wpg_ref[...]) for v in x1]
    for rs, v, g, e in zip(chunks, x1, gate, ple):
        x2 = v + _sigmoid(g) * e
        x_out_ref[rs, :] = x2
        y = x2 * lax.rsqrt(jnp.mean(x2 * x2, axis=-1, keepdims=True) + NORM_EPS)
        hn_ref[rs, :] = (y * gn_ref[...]).astype(hn_ref.dtype)


MERGE_CHUNK = 256


def _merge(ro, do, lo, mg, x, p_all, layer, wr, wd, wl, wo, wpg, wp, g_next, hn_dtype, tm):
    t = x.shape[0]
    d = D_MODEL
    row = lambda wdt: pl.BlockSpec((tm, wdt), lambda i: (i, 0))
    full = lambda a: pl.BlockSpec(a.shape, lambda i: (0, 0), pipeline_mode=pl.Buffered(1))
    return pl.pallas_call(
        _merge_kernel,
        grid=(t // tm,),
        in_specs=[row(RET_V_W), row(DIFF_OUT_W), row(RNN_WIDTH), row(3 * d), row(d),
                  pl.BlockSpec((None, tm, D_PLE), lambda i: (layer, i, 0)),
                  full(wr), full(wd), full(wl), full(wo), full(wpg), full(wp),
                  pl.BlockSpec((1, d), lambda i: (0, 0))],
        out_specs=[row(d), row(d)],
        out_shape=[jax.ShapeDtypeStruct((t, d), F32), jax.ShapeDtypeStruct((t, d), hn_dtype)],
        compiler_params=_cparams(("parallel",), 56),
        name="merge",
    )(ro, do, lo, mg, x, p_all, wr, wd, wl, wo, wpg, wp, g_next.reshape(1, d))


def _lambda_init(layer):
    return 0.8 - 0.6 * math.exp(-0.3 * layer)


def _rope_tables(pos):
    half = RET_DK // 2
    inv = ROPE_BASE ** (-jnp.arange(half, dtype=F32) / half)
    ang = pos.astype(F32)[:, None] * inv[None, :]
    cos = jnp.cos(ang)
    sin = jnp.sin(ang)
    return jnp.concatenate([cos, cos], axis=-1), jnp.concatenate([-sin, sin], axis=-1)


def kernel(x_prompt, x_sample, cache_k, cache_v, state_ret, state_lru, state_conv, page_table, p_prompt, p_sample, rel_bias, norm_g, w_in, lambda_q1, lambda_k1, lambda_q2, lambda_k2, subln_g, conv_w, conv_b, gate_a_w, gate_a_b, gate_x_w, gate_x_b, lru_lambda, w_ret_out, w_diff_out, w_lru_out, w_o, w_ple, w_ple_gate, final_norm_g):
    bp, sp, d = x_prompt.shape
    bs, ss, _ = x_sample.shape
    assert d == D_MODEL and ss == 1
    depth = w_in.shape[0]
    n_pages, page = page_table.shape[1], cache_k.shape[2]
    past = n_pages * page
    tp = bp * sp
    blk = min(256, sp)
    tm_p = min(1024, sp)
    tm_qkv = min(512, sp)
    tm_lru = min(256, sp)
    tm_merge = min(512, sp)
    p_prompt_all = p_prompt.reshape(depth, tp, D_PLE)
    p_sample_all = p_sample.reshape(depth, bs, D_PLE)

    cos_p, sin_p = _rope_tables(jnp.arange(sp, dtype=jnp.int32))
    cos_s, sin_s = _rope_tables(jnp.full((bs,), past, dtype=jnp.int32))
    bias_p = _bias_prompt(rel_bias, blk)
    bias_s, bias_s_new = _bias_sample(rel_bias, past)

    w_in_b = w_in.astype(BF16)
    wg_all = jnp.concatenate([gate_a_w, gate_x_w], axis=-1).astype(BF16)
    xp = x_prompt.reshape(tp, d)
    xs = x_sample.reshape(bs, d)
    hn_p = _rmsnorm(xp, norm_g[0], BF16, tm_p)
    hn_s = _rmsnorm(xs, norm_g[0], BF16, bs)

    outs = {k: [] for k in ("ks", "vs", "rp", "lp", "ls", "cp", "cs")}
    kp_all = jnp.zeros((bp, depth, sp * DIFF_KV_HEADS, 2 * DIFF_DH), F32)
    vp_all = jnp.zeros((bp, depth, sp * DIFF_KV_HEADS, DIFF_DV), F32)
    rs_all = jnp.zeros(state_ret.shape, F32)
    for l in range(depth):
        wg = wg_all[l]
        wr, wd, wl = w_ret_out[l].astype(BF16), w_diff_out[l].astype(BF16), w_lru_out[l].astype(BF16)
        wo, wpg, wp = w_o[l].astype(BF16), w_ple_gate[l].astype(BF16), w_ple[l].astype(BF16)
        lam0 = _lambda_init(l)
        lam_rows = jnp.stack([lambda_q1[l], lambda_k1[l], lambda_q2[l], lambda_k2[l]], axis=0)
        last = l == depth - 1
        g_next = final_norm_g if last else norm_g[l + 1]
        hn_dtype = F32 if last else BF16

        qk = _proj(hn_p, w_in_b, l, COL_ROT, "rotary", tm_p, tables=(cos_p, sin_p))
        rv = _proj(hn_p, w_in_b, l, COL_RV, "plain", tm_p)
        gates = _proj(hn_p, w_in_b, l, COL_SILU, "silu", tm_p)
        kp_all, vp_all, kb, qt, vt = _proj_qkv_t(hn_p, w_in_b, l, tm_qkv, blk, kp_all, vp_all)
        lx = _proj(hn_p, w_in_b, l, COL_LX, "plain", tm_p, out_dtype=F32)
        mg = _proj(hn_p, w_in_b, l, COL_MG, "sigmoid", tm_p)
        qk_s = _proj(hn_s, w_in_b, l, COL_ROT, "rotary", bs, tables=(cos_s, sin_s))
        rvdq_s = _proj(hn_s, w_in_b, l, COL_RV_DQ, "plain", bs)
        gates_s = _proj(hn_s, w_in_b, l, COL_SILU, "silu", bs)
        kf_s, vf_s, kb_s, vb_s = _proj(hn_s, w_in_b, l, (COL_KV,), "kv", bs)
        lx_s = _proj(hn_s, w_in_b, l, COL_LX, "plain", bs, out_dtype=F32)
        mg_s = _proj(hn_s, w_in_b, l, COL_MG, "sigmoid", bs)

        do, do_s = _attention(qt, kb, vt, bias_p, gates, bp, sp, blk,
                              rvdq_s[:, RET_V_W:], kb_s, vb_s, gates_s[:, RET_V_W:RET_V_W + DIFF_OUT_W],
                              cache_k, cache_v, page_table, bias_s, bias_s_new, subln_g[l],
                              lam_rows, lam0, l)

        ro, s_fin, lo, h_last, conv_new = _ret_lru_prompt(
            qk, rv, gates, lx, conv_w[l], conv_b[l], wg, gate_a_b[l], gate_x_b[l], lru_lambda[l],
            bp, sp, tm_lru)
        xp, hn_p = _merge(ro, do, lo, mg, xp, p_prompt_all, l, wr, wd, wl, wo, wpg, wp,
                          g_next, hn_dtype, tm_merge)
        outs["rp"].append(s_fin)
        outs["lp"].append(h_last.reshape(bp, RNN_WIDTH))
        outs["cp"].append(conv_new)

        lx = lx_s
        ro, rs_all = _retention_step(qk_s, rvdq_s, gates_s, state_ret, rs_all, l)
        conv_prev = state_conv[:, l]
        lo, h_new = _lru_step(lx, conv_prev, state_lru[:, l], gates_s[:, 2 * RET_V_W:], conv_w[l],
                              conv_b[l], wg, gate_a_b[l], gate_x_b[l], lru_lambda[l])
        xs, hn_s = _merge(ro, do_s, lo, mg_s, xs, p_sample_all, l, wr, wd, wl, wo, wpg, wp,
                          g_next, hn_dtype, bs)
        outs["ks"].append(kf_s.reshape(bs, 1, DIFF_KV_HEADS, 2 * DIFF_DH))
        outs["vs"].append(vf_s.reshape(bs, 1, DIFF_KV_HEADS, DIFF_DV))
        outs["ls"].append(h_new)
        outs["cs"].append(jnp.concatenate([conv_prev[:, 1:], lx[:, None, :]], axis=1))

    st = lambda key: jnp.stack(outs[key], axis=1)
    return (hn_p.reshape(bp, sp, d), hn_s.reshape(bs, 1, d),
            kp_all.reshape(bp, depth, sp, DIFF_KV_HEADS, 2 * DIFF_DH),
            vp_all.reshape(bp, depth, sp, DIFF_KV_HEADS, DIFF_DV), st("ks"), st("vs"),
            st("rp"), rs_all, st("lp"), st("ls"), st("cp"), st("cs"))
```

```python
import functools
import math

import numpy as np
import jax
import jax.numpy as jnp
from jax import lax
from jax.experimental import pallas as pl
from jax.experimental.pallas import tpu as pltpu

F32 = jnp.float32
BF16 = jnp.bfloat16

D_MODEL = 1024
D_PLE = 256
NORM_EPS = 1e-6
RET_HEADS = 4
RET_DK = 128
RET_DV = 256
RET_CHUNK = 128
RET_CHUNKS_PER_STEP = 4
ROPE_BASE = 10000.0
DIFF_HEADS = 8
DIFF_KV_HEADS = 4
DIFF_GROUP = DIFF_HEADS // DIFF_KV_HEADS
DIFF_DH = 64
DIFF_DV = 2 * DIFF_DH
REL_BUCKETS = 32
REL_MAX_DIST = 128
RNN_WIDTH = 1024
RNN_BLOCKS = 8
RNN_BLOCK_W = RNN_WIDTH // RNN_BLOCKS
CONV_WIDTH = 4
LRU_C = 8.0

RET_QK_W = RET_HEADS * RET_DK
RET_V_W = RET_HEADS * RET_DV
DIFF_Q_W = DIFF_HEADS * 2 * DIFF_DH
DIFF_K_W = DIFF_KV_HEADS * 2 * DIFF_DH
DIFF_V_W = DIFF_KV_HEADS * DIFF_DV
DIFF_OUT_W = DIFF_HEADS * DIFF_DV
IN_SPLITS = (RET_QK_W, RET_QK_W, RET_V_W, RET_V_W,
             DIFF_Q_W, DIFF_K_W, DIFF_V_W, DIFF_OUT_W,
             RNN_WIDTH, RNN_WIDTH, 3 * D_MODEL)
IN_OFFS = tuple(int(v) for v in np.cumsum((0,) + IN_SPLITS))

LANES = 128
SUBLANES = 8
MASK_VALUE = -1e30
LOG2E = math.log2(math.e)
VT_ONES = 2 * SUBLANES
VT_ROWS = DIFF_DV + VT_ONES
BIAS_DIAG, BIAS_PREV, BIAS_ZERO = 0, 1, 2
MERGE_CHUNK = 256
ATTN_BLOCK = 256
PROJ_ROWS = 1024
QKV_ROWS = 512
LRU_ROWS = 256
MERGE_ROWS = 512
MIB = 1024 * 1024


def _cparams(sem, vmem_mib):
    return pltpu.CompilerParams(dimension_semantics=sem, vmem_limit_bytes=int(vmem_mib * MIB))


def _rmsnorm_kernel(x_ref, g_ref, o_ref):
    x = x_ref[...]
    y = x * lax.rsqrt(jnp.mean(x * x, axis=-1, keepdims=True) + NORM_EPS)
    o_ref[...] = (y * g_ref[...]).astype(o_ref.dtype)


def _rmsnorm(x, g, out_dtype, tm):
    t, d = x.shape
    return pl.pallas_call(
        _rmsnorm_kernel,
        grid=(t // tm,),
        in_specs=[pl.BlockSpec((tm, d), lambda i: (i, 0)),
                  pl.BlockSpec((1, d), lambda i: (0, 0))],
        out_specs=pl.BlockSpec((tm, d), lambda i: (i, 0)),
        out_shape=jax.ShapeDtypeStruct((t, d), out_dtype),
        compiler_params=_cparams(("parallel",), 32),
        name="rmsnorm",
    )(x, g.reshape(1, d))


def _proj_qkv_t_kernel(x_ref, wq_ref, wkv_ref, k_all_ref, v_all_ref, kf_ref, vf_ref, kb_ref,
                       qt_ref, vt_ref):
    del k_all_ref, v_all_ref
    blk = qt_ref.shape[-1]

    def matmuls(j):
        x = x_ref[j * blk:(j + 1) * blk, :]
        return (jnp.dot(x, wq_ref[...], preferred_element_type=F32),
                jnp.dot(x, wkv_ref[...], preferred_element_type=F32))

    def epilogue(j, q, kv):
        rows = slice(j * blk, (j + 1) * blk)
        k = kv[:, :DIFF_K_W]
        v = kv[:, DIFF_K_W:]
        for h in range(DIFF_KV_HEADS):
            dst = pl.ds(j * blk * DIFF_KV_HEADS + h, blk, stride=DIFF_KV_HEADS)
            kf_ref[dst, :] = k[:, h * 2 * DIFF_DH:(h + 1) * 2 * DIFF_DH]
            vf_ref[dst, :] = v[:, h * DIFF_DV:(h + 1) * DIFF_DV]
        kb_ref[rows, :] = k.astype(BF16)
        qt_ref[j] = (q * (DIFF_DH ** -0.5 * LOG2E)).T.astype(BF16)
        vt = v.T.astype(BF16)
        for h in range(DIFF_KV_HEADS):
            vt_ref[j, h * VT_ROWS:h * VT_ROWS + DIFF_DV, :] = vt[h * DIFF_DV:(h + 1) * DIFF_DV]
            vt_ref[j, h * VT_ROWS + DIFF_DV:(h + 1) * VT_ROWS, :] = jnp.ones((VT_ONES, blk), BF16)

    _pipeline_rows(qt_ref.shape[0], matmuls, epilogue)


def _pipeline_rows(n_chunks, matmuls, epilogue):
    pending = {}
    for c in range(n_chunks + 1):
        if c < n_chunks:
            pending[c] = matmuls(c)
        if c >= 1:
            epilogue(c - 1, *pending.pop(c - 1))


def _sigmoid(x):
    return 0.5 * jnp.tanh(0.5 * x) + 0.5


def _proj_kernel(x_ref, w_ref, *refs, kind, chunk):
    tm = x_ref.shape[0]

    def matmuls(c):
        return (jnp.dot(x_ref[c * chunk:(c + 1) * chunk, :], w_ref[...], preferred_element_type=F32),)

    def epilogue(c, acc):
        rows = slice(c * chunk, (c + 1) * chunk)
        if kind == "plain":
            (o_ref,) = refs
            o_ref[rows, :] = acc.astype(o_ref.dtype)
        elif kind == "silu":
            (o_ref,) = refs
            o_ref[rows, :] = (acc * _sigmoid(acc)).astype(o_ref.dtype)
        elif kind == "sigmoid":
            (o_ref,) = refs
            o_ref[rows, :] = _sigmoid(acc).astype(o_ref.dtype)
        elif kind == "rotary":
            cos_ref, sin_ref, o_ref = refs
            cos = cos_ref[rows, :]
            sin = sin_ref[rows, :]
            for h in range(2 * RET_HEADS):
                z = acc[:, h * RET_DK:(h + 1) * RET_DK]
                r = z * cos + pltpu.roll(z, RET_DK // 2, 1) * sin
                if h >= RET_HEADS:
                    r = r * (RET_DK ** -0.5)
                o_ref[rows, h * RET_DK:(h + 1) * RET_DK] = r.astype(o_ref.dtype)
        elif kind == "kv":
            kf_ref, vf_ref, kb_ref, vb_ref = refs
            k = acc[:, :DIFF_K_W]
            v = acc[:, DIFF_K_W:]
            kf_ref[rows, :] = k
            vf_ref[rows, :] = v
            kb_ref[rows, :] = k.astype(BF16)
            vb_ref[rows, :] = v.astype(BF16)
        else:
            raise ValueError(kind)

    _pipeline_rows(tm // chunk, matmuls, epilogue)


PROJ_TN = 1024
PROJ_CHUNK = 256
COL_ROT = (0,)
COL_RV = (1,)
COL_RV_DQ = (1, 3)
COL_SILU = (2, 5, 7)
COL_DQ = 3
COL_KV = 4
COL_LX = (6,)
COL_MG = (8, 9, 10)
assert IN_OFFS[2] == PROJ_TN and IN_OFFS[5] == COL_KV * PROJ_TN and IN_OFFS[10] == COL_MG[0] * PROJ_TN


def _col_index(cols, j):
    if len(cols) == 1:
        return cols[0]
    step = cols[1] - cols[0]
    idx = cols[0] + step * j
    if cols[-1] != cols[0] + step * (len(cols) - 1):
        idx = idx - j // (len(cols) - 1)
    return idx


def _proj_qkv_t(hn, w_all, layer, tm, blk, k_all, v_all):
    t, d = hn.shape
    nj = tm // blk
    per_seq = k_all.shape[2] // (tm * DIFF_KV_HEADS)
    half = pl.BlockSpec((tm, DIFF_K_W), lambda i: (i, 0))
    per_head = pl.BlockSpec((None, None, tm * DIFF_KV_HEADS, DIFF_DV),
                            lambda i: (i // per_seq, layer, i % per_seq, 0))
    untouched = pl.BlockSpec(memory_space=pl.ANY)
    return pl.pallas_call(
        _proj_qkv_t_kernel,
        grid=(t // tm,),
        in_specs=[pl.BlockSpec((tm, d), lambda i: (i, 0)),
                  pl.BlockSpec((None, d, PROJ_TN), lambda i: (layer, 0, COL_DQ)),
                  pl.BlockSpec((None, d, PROJ_TN), lambda i: (layer, 0, COL_KV)),
                  untouched, untouched],
        out_specs=[per_head, per_head, half,
                   pl.BlockSpec((nj, DIFF_Q_W, blk), lambda i: (i, 0, 0)),
                   pl.BlockSpec((nj, DIFF_KV_HEADS * VT_ROWS, blk), lambda i: (i, 0, 0))],
        out_shape=[jax.ShapeDtypeStruct(k_all.shape, F32),
                   jax.ShapeDtypeStruct(v_all.shape, F32),
                   jax.ShapeDtypeStruct((t, DIFF_K_W), BF16),
                   jax.ShapeDtypeStruct((t // blk, DIFF_Q_W, blk), BF16),
                   jax.ShapeDtypeStruct((t // blk, DIFF_KV_HEADS * VT_ROWS, blk), BF16)],
        input_output_aliases={3: 0, 4: 1},
        compiler_params=_cparams(("parallel",), 48),
        name="proj_qkv_t",
    )(hn, w_all, w_all, k_all, v_all)


def _proj(hn, w_all, layer, cols, kind, tm, out_dtype=BF16, tables=None):
    t, d = hn.shape
    tn = PROJ_TN
    n = tn * len(cols)
    grid = (len(cols), t // tm)
    in_specs = [pl.BlockSpec((tm, d), lambda j, i: (i, 0)),
                pl.BlockSpec((None, d, tn), lambda j, i: (layer, 0, _col_index(cols, j)))]
    args = [hn, w_all]
    if kind == "rotary":
        cos, sin = tables
        nb = cos.shape[0] // tm
        spec = pl.BlockSpec((tm, RET_DK), lambda j, i: (i % nb, 0))
        in_specs += [spec, spec]
        args += [cos, sin]
    if kind == "kv":
        half = pl.BlockSpec((tm, DIFF_K_W), lambda j, i: (i, 0))
        out_specs = [half, half, half, half]
        out_shape = [jax.ShapeDtypeStruct((t, DIFF_K_W), F32),
                     jax.ShapeDtypeStruct((t, DIFF_V_W), F32),
                     jax.ShapeDtypeStruct((t, DIFF_K_W), BF16),
                     jax.ShapeDtypeStruct((t, DIFF_V_W), BF16)]
    else:
        out_specs = pl.BlockSpec((tm, tn), lambda j, i: (i, j))
        out_shape = jax.ShapeDtypeStruct((t, n), out_dtype)
    return pl.pallas_call(
        functools.partial(_proj_kernel, kind=kind, chunk=min(PROJ_CHUNK, tm)),
        grid=grid, in_specs=in_specs, out_specs=out_specs, out_shape=out_shape,
        compiler_params=_cparams(("parallel", "parallel"), 40),
        name="proj_" + kind,
    )(*args)


def _retention_chunk(ci, qk_ref, v_ref, gate_ref, dmask_ref, qdec_ref, kdec_ref, o_ref, state_ref,
                     chunk_dec):
    heads = range(RET_HEADS)
    dot = functools.partial(jnp.dot, preferred_element_type=F32)
    cw = dmask_ref.shape[-1]
    rows = slice(ci * cw, (ci + 1) * cw)
    q = [qk_ref[rows, h * RET_DK:(h + 1) * RET_DK] for h in heads]
    k = [qk_ref[rows, RET_QK_W + h * RET_DK:RET_QK_W + (h + 1) * RET_DK] for h in heads]
    v = [v_ref[rows, h * RET_DV:(h + 1) * RET_DV] for h in heads]
    s = [lax.dot_general(q[h], k[h], (((1,), (1,)), ((), ())), preferred_element_type=F32)
         for h in heads]
    cross = [dot(q[h], state_ref[h].astype(BF16)) for h in heads]
    kd = [(k[h].astype(F32) * kdec_ref[:, h:h + 1]).astype(BF16) for h in heads]
    kv = [lax.dot_general(kd[h], v[h], (((0,), (0,)), ((), ())), preferred_element_type=F32)
          for h in heads]
    inner = [dot((s[h] * dmask_ref[h]).astype(BF16), v[h]) for h in heads]
    for h in heads:
        state_ref[h] = state_ref[h] * chunk_dec[h] + kv[h]
        o = inner[h] + cross[h] * qdec_ref[:, h:h + 1]
        o = o * lax.rsqrt(jnp.mean(o * o, axis=-1, keepdims=True) + NORM_EPS)
        g = gate_ref[rows, h * RET_DV:(h + 1) * RET_DV].astype(F32)
        o_ref[rows, h * RET_DV:(h + 1) * RET_DV] = (o * g).astype(o_ref.dtype)


def _retention_decays(c):
    log_g = np.log1p(-(2.0 ** (-5.0 - np.arange(RET_HEADS, dtype=np.float64))))
    idx = np.arange(c, dtype=np.float64)
    dist = idx[:, None] - idx[None, :]
    dmask = np.where(dist[None] >= 0, np.exp(np.maximum(dist, 0.0)[None] * log_g[:, None, None]), 0.0)
    qdec = np.exp((idx + 1.0)[:, None] * log_g[None, :])
    kdec = np.exp((c - 1.0 - idx)[:, None] * log_g[None, :])
    chunk_dec = tuple(float(x) for x in np.exp(c * log_g).astype(np.float32))
    return jnp.asarray(dmask, F32), jnp.asarray(qdec, F32), jnp.asarray(kdec, F32), chunk_dec


def _retention_step_kernel(qk_ref, v_ref, gate_ref, s_ref, s_all_ref, o_ref, snew_ref, *, decay):
    del s_all_ref
    sb = qk_ref.shape[0]
    rowid = lax.broadcasted_iota(jnp.int32, (sb, 1), 0)

    def per_sequence(b, outs):
        new_outs = []
        for h in range(RET_HEADS):
            q = qk_ref[:, h * RET_DK:(h + 1) * RET_DK]
            k = qk_ref[:, RET_QK_W + h * RET_DK:RET_QK_W + (h + 1) * RET_DK]
            v = v_ref[:, h * RET_DV:(h + 1) * RET_DV]
            st = s_ref[b, h]
            qs = jnp.dot(q, st.astype(BF16), preferred_element_type=F32)
            new_outs.append(outs[h] + jnp.where(rowid == b, decay[h] * qs, 0.0))
            kb = jnp.where(rowid == b, k, jnp.zeros_like(k))
            snew_ref[b, h] = st * decay[h] + lax.dot_general(
                kb, v, (((0,), (0,)), ((), ())), preferred_element_type=F32)
        return tuple(new_outs)

    outs = []
    for h in range(RET_HEADS):
        q = qk_ref[:, h * RET_DK:(h + 1) * RET_DK].astype(F32)
        k = qk_ref[:, RET_QK_W + h * RET_DK:RET_QK_W + (h + 1) * RET_DK].astype(F32)
        v = v_ref[:, h * RET_DV:(h + 1) * RET_DV].astype(F32)
        outs.append(jnp.sum(q * k, axis=1, keepdims=True) * v)
    outs = lax.fori_loop(0, sb, per_sequence, tuple(outs))
    for h in range(RET_HEADS):
        o = outs[h]
        o = o * lax.rsqrt(jnp.mean(o * o, axis=-1, keepdims=True) + NORM_EPS)
        g = gate_ref[:, h * RET_DV:(h + 1) * RET_DV].astype(F32)
        o_ref[:, h * RET_DV:(h + 1) * RET_DV] = (o * g).astype(o_ref.dtype)


def _retention_step(qk, rvdq, gates, state_ret, state_new, layer):
    bs = qk.shape[0]
    sb = min(2 * SUBLANES, bs)
    log_g = np.log1p(-(2.0 ** (-5.0 - np.arange(RET_HEADS, dtype=np.float64))))
    decay = tuple(float(x) for x in np.exp(log_g).astype(np.float32))
    layer_state = pl.BlockSpec((sb, None, RET_HEADS, RET_DK, RET_DV), lambda i: (i, layer, 0, 0, 0))
    return pl.pallas_call(
        functools.partial(_retention_step_kernel, decay=decay),
        grid=(bs // sb,),
        in_specs=[pl.BlockSpec((sb, 2 * RET_QK_W), lambda i: (i, 0)),
                  pl.BlockSpec((sb, RET_V_W), lambda i: (i, 0)),
                  pl.BlockSpec((sb, RET_V_W), lambda i: (i, 0)),
                  layer_state,
                  pl.BlockSpec(memory_space=pl.ANY)],
        out_specs=[pl.BlockSpec((sb, RET_V_W), lambda i: (i, 0)), layer_state],
        out_shape=[jax.ShapeDtypeStruct((bs, RET_V_W), BF16),
                   jax.ShapeDtypeStruct(state_new.shape, F32)],
        input_output_aliases={4: 1},
        compiler_params=_cparams(("parallel",), 48),
        name="retention_step",
    )(qk, rvdq, gates, state_ret, state_new)


def _t5_bucket(rel):
    n = jnp.maximum(rel, 0)
    max_exact = REL_BUCKETS // 2
    nf = jnp.maximum(n.astype(F32), 1.0)
    large = max_exact + (jnp.log(nf / max_exact) / math.log(REL_MAX_DIST / max_exact)
                         * (REL_BUCKETS - max_exact)).astype(jnp.int32)
    large = jnp.minimum(large, REL_BUCKETS - 1)
    return jnp.where(n < max_exact, n, large)


def _bias_lookup(tab_ref, bucket, head):
    out = jnp.zeros(bucket.shape, F32)
    far = tab_ref[REL_BUCKETS - 1, head]
    for bkt in range(REL_BUCKETS - 1):
        out = jnp.where(bucket == bkt, tab_ref[bkt, head] - far, out)
    return out


def _bias_prompt_kernel(tab_ref, o_ref, *, blk):
    i = lax.broadcasted_iota(jnp.int32, (blk, blk), 1)
    j = lax.broadcasted_iota(jnp.int32, (blk, blk), 0)
    for which in (BIAS_DIAG, BIAS_PREV):
        rel = i - j + which * blk
        bucket = _t5_bucket(rel)
        for h in range(DIFF_HEADS):
            o_ref[which, h] = jnp.where(rel >= 0, _bias_lookup(tab_ref, bucket, h) * LOG2E, MASK_VALUE)
    o_ref[BIAS_ZERO] = jnp.zeros(o_ref.shape[1:], F32)


def _bias_prompt(rel_bias, blk):
    return pl.pallas_call(
        functools.partial(_bias_prompt_kernel, blk=blk),
        in_specs=[pl.BlockSpec(memory_space=pltpu.SMEM)],
        out_specs=pl.BlockSpec(memory_space=pltpu.VMEM),
        out_shape=jax.ShapeDtypeStruct((3, DIFF_HEADS, blk, blk), F32),
        compiler_params=pltpu.CompilerParams(vmem_limit_bytes=32 * MIB),
        name="bias_prompt",
    )(rel_bias)


def _bias_sample_kernel(tab_ref, o_ref, onew_ref, *, past, width):
    rows = 2 * DIFF_HEADS
    r = lax.broadcasted_iota(jnp.int32, (rows, width), 0)
    lane = lax.broadcasted_iota(jnp.int32, (rows, width), 1)
    head = r % DIFF_HEADS
    kvh = lane % DIFF_KV_HEADS
    pos = lane // DIFF_KV_HEADS
    bucket = _t5_bucket(past - pos)
    out = jnp.zeros((rows, width), F32)
    for h in range(DIFF_HEADS):
        out = jnp.where(head == h, _bias_lookup(tab_ref, bucket, h), out)
    o_ref[...] = jnp.where(kvh == head // DIFF_GROUP, out, MASK_VALUE)
    rn = lax.broadcasted_iota(jnp.int32, (rows, LANES), 0) % DIFF_HEADS
    new = jnp.zeros((rows, LANES), F32)
    zero_bucket = _t5_bucket(jnp.zeros((rows, LANES), jnp.int32))
    for h in range(DIFF_HEADS):
        new = jnp.where(rn == h, _bias_lookup(tab_ref, zero_bucket, h), new)
    onew_ref[...] = new


def _bias_sample(rel_bias, past):
    width = past * DIFF_KV_HEADS
    return pl.pallas_call(
        functools.partial(_bias_sample_kernel, past=past, width=width),
        in_specs=[pl.BlockSpec(memory_space=pltpu.SMEM)],
        out_specs=[pl.BlockSpec(memory_space=pltpu.VMEM), pl.BlockSpec(memory_space=pltpu.VMEM)],
        out_shape=[jax.ShapeDtypeStruct((2 * DIFF_HEADS, width), F32),
                   jax.ShapeDtypeStruct((2 * DIFF_HEADS, LANES), F32)],
        compiler_params=pltpu.CompilerParams(vmem_limit_bytes=32 * MIB),
        name="bias_sample",
    )(rel_bias)


def _lambda_full(lam_ref, lam0):
    a = jnp.sum(lam_ref[0:1, :] * lam_ref[1:2, :], axis=-1, keepdims=True)
    b = jnp.sum(lam_ref[2:3, :] * lam_ref[3:4, :], axis=-1, keepdims=True)
    return jnp.exp(a) - jnp.exp(b) + lam0


def _attn_prompt_body(qt_in_ref, k_ref, vt_ref, bias_ref, gate_ref, sg_ref, lam_ref, o_ref,
                      qt_ref, m_ref, acc_ref, *, blk, nq, lam0, side_work=()):
    pair = pl.program_id(2)
    tiles = (pair, nq - 1 - pair)
    n_chunks = 2 * DIFF_GROUP
    row = lax.broadcasted_iota(jnp.int32, (DIFF_DV, blk), 0)
    for slot in range(2):
        for g in range(DIFF_GROUP):
            qg = qt_in_ref[tiles[slot], g * DIFF_DV:(g + 1) * DIFF_DV, :]
            for m in range(2):
                keep = (row < DIFF_DH) if m == 0 else (row >= DIFF_DH)
                c = m * DIFF_GROUP + g
                qt_ref[c, slot] = jnp.where(keep, qg, jnp.zeros_like(qg))
    m_ref[...] = jnp.full(m_ref.shape, MASK_VALUE, F32)
    acc_ref[...] = jnp.zeros(acc_ref.shape, F32)

    def run_blocks(blocks):
        s_lag = 3
        kbs = [k_ref[pl.ds(pl.multiple_of(ki * blk, blk), blk), :] for _, ki, _ in blocks]
        vts = [vt_ref[ki] for _, ki, _ in blocks]
        units = [(bi, c) for bi in range(len(blocks)) for c in range(n_chunks)]
        n_units = len(units)

        def scores(u):
            bi, c = units[u]
            slot, _, bias = blocks[bi]
            s = jnp.dot(kbs[bi], qt_ref[c, slot], preferred_element_type=F32)
            if bias is not None:
                s = s + bias_ref[bias, c % DIFF_GROUP]
            return s

        def softmax(u, s):
            bi, c = units[u]
            slot = blocks[bi][0]
            m_prev = m_ref[c, slot]
            m_new = jnp.maximum(m_prev, jnp.max(s, axis=0, keepdims=True))
            p = jnp.exp2(s - m_new)
            alpha = jnp.exp2(m_prev - m_new)
            m_ref[c, slot] = m_new
            return p.astype(BF16), alpha

        def accumulate(u, p, alpha):
            bi, c = units[u]
            slot = blocks[bi][0]
            acc_ref[c, slot] = acc_ref[c, slot] * alpha + jnp.dot(
                vts[bi], p, preferred_element_type=F32)

        s_vals, p_vals = {}, {}
        n_steps = n_units + s_lag + 1
        for t in range(n_steps):
            if t < n_units:
                s_vals[t] = scores(t)
            if 0 <= t - s_lag < n_units:
                p_vals[t - s_lag] = softmax(t - s_lag, s_vals.pop(t - s_lag))
            if 0 <= t - s_lag - 1 < n_units:
                accumulate(t - s_lag - 1, *p_vals.pop(t - s_lag - 1))
            lo, hi = (t * len(side_work)) // n_steps, ((t + 1) * len(side_work)) // n_steps
            for piece in side_work[lo:hi]:
                piece()

    n_b = jnp.minimum(nq - 2 - pair, nq - 3)
    blocks = []
    for j in range(nq - 3):
        if j < nq // 2 - 1:
            blocks.append((1, j, None))
        else:
            in_b = j < n_b
            blocks.append((jnp.where(in_b, 1, 0), jnp.where(in_b, j, j - n_b), None))
    first = pair == 0
    blocks.append((jnp.where(first, 1, 0), jnp.where(first, nq - 3, pair - 1),
                   jnp.where(first, BIAS_ZERO, BIAS_PREV)))
    blocks.append((0, pair, BIAS_DIAG))
    blocks.append((1, nq - 2 - pair, BIAS_PREV))
    blocks.append((1, nq - 1 - pair, BIAS_DIAG))
    run_blocks(blocks)

    lam = _lambda_full(lam_ref, lam0)
    for slot in range(2):
        rows = pl.ds(pl.multiple_of(tiles[slot] * blk, blk), blk)
        a = [acc_ref[c, slot, 0:DIFF_DV, :] / acc_ref[c, slot, DIFF_DV:DIFF_DV + 1, :]
             for c in range(n_chunks)]
        for g in range(DIFF_GROUP):
            ot = a[g] - lam * a[DIFF_GROUP + g]
            ot = ot * lax.rsqrt(jnp.mean(ot * ot, axis=0, keepdims=True) + NORM_EPS)
            ot = ot * sg_ref[...] * (1.0 - lam0)
            gate = gate_ref[rows, g * DIFF_DV:(g + 1) * DIFF_DV].astype(F32)
            o_ref[rows, g * DIFF_DV:(g + 1) * DIFF_DV] = (ot.T * gate).astype(o_ref.dtype)


def _attn_decode_pieces(k_refs, v_refs, qt_ref, kn_ref, vn_ref, bias_ref, bnew_ref, gate_ref,
                        sg_ref, lam_ref, o_ref, s_ref, p_ref, *, rows_per_page, lam0):
    n_pages = len(k_refs)
    w = rows_per_page
    state = {}

    def score_page(p):
        def run():
            kp = k_refs[p][...].astype(BF16)
            sp = lax.dot_general(qt_ref[...], kp, (((1,), (1,)), ((), ())),
                                 preferred_element_type=F32)
            s_ref[:, p * w:(p + 1) * w] = sp + bias_ref[:, p * w:(p + 1) * w]
        return run

    def softmax():
        s = s_ref[...]
        s_new = jnp.sum(qt_ref[...].astype(F32) * kn_ref[...].astype(F32), axis=1, keepdims=True)
        s_new = s_new + bnew_ref[:, 0:1]
        m = jnp.maximum(jnp.max(s, axis=1, keepdims=True), s_new)
        e_new = jnp.exp(s_new - m)
        pr = jnp.exp(s - m)
        state["l"] = jnp.sum(pr, axis=1, keepdims=True) + e_new
        state["acc"] = e_new * vn_ref[...].astype(F32)
        p_ref[...] = pr.astype(BF16)

    def value_page(p):
        def run():
            vp = v_refs[p][...].astype(BF16)
            state["acc"] = state["acc"] + jnp.dot(p_ref[:, p * w:(p + 1) * w], vp,
                                                  preferred_element_type=F32)
        return run

    def finish():
        a = state["acc"] / state["l"]
        lam = _lambda_full(lam_ref, lam0)
        o = a[:DIFF_HEADS] - lam * a[DIFF_HEADS:]
        o = o * lax.rsqrt(jnp.mean(o * o, axis=-1, keepdims=True) + NORM_EPS)
        o = o * sg_ref[...] * (1.0 - lam0)
        o_ref[...] = o * gate_ref[...]

    return ([score_page(p) for p in range(n_pages)] + [softmax]
            + [value_page(p) for p in range(n_pages)] + [finish])


N_PROMPT_IN = 7
N_DECODE_IN = 7


def _attn_kernel(pt_ref, *refs, n_pages, rows_per_page, blk, nq, lam0):
    del pt_ref
    prompt_in = refs[:N_PROMPT_IN]
    k_refs = refs[N_PROMPT_IN:N_PROMPT_IN + n_pages]
    v_refs = refs[N_PROMPT_IN + n_pages:N_PROMPT_IN + 2 * n_pages]
    decode_in = refs[N_PROMPT_IN + 2 * n_pages:N_PROMPT_IN + 2 * n_pages + N_DECODE_IN]
    o_ref, od_ref, qt_s, m_s, acc_s, sd_s, pd_s = refs[N_PROMPT_IN + 2 * n_pages + N_DECODE_IN:]
    lam_ref = prompt_in[-1]
    pieces = _attn_decode_pieces(k_refs, v_refs, *decode_in, lam_ref, od_ref, sd_s, pd_s,
                                 rows_per_page=rows_per_page, lam0=lam0)
    _attn_prompt_body(*prompt_in, o_ref, qt_s, m_s, acc_s, blk=blk, nq=nq, lam0=lam0,
                      side_work=pieces)


def _attention(qt_p, kb_p, vt_p, bias_p, gates_p, b, s, blk,
               dq, kb_new, vb_new, gate, cache_k, cache_v, page_table, bias, bias_new,
               subln_g, lam_rows, lam0, layer):
    nq = s // blk
    assert nq % 2 == 0 and nq >= 4
    bs = dq.shape[0]
    assert bs == b * DIFF_KV_HEADS * (nq // 2), "one sample sequence per prompt grid step"
    t = b * s
    gcol0 = RET_V_W // (DIFF_GROUP * DIFF_DV)
    n_pool, depth, page = cache_k.shape[0], cache_k.shape[1], cache_k.shape[2]
    n_pages = page_table.shape[1]
    w = page * DIFF_KV_HEADS
    ck = cache_k.reshape(n_pool, depth, w, 2 * DIFF_DH)
    cv = cache_v.reshape(n_pool, depth, w, DIFF_DV)
    q = (dq * (DIFF_DH ** -0.5)).reshape(bs, DIFF_KV_HEADS, DIFF_GROUP, 2, DIFF_DH)
    q = jnp.transpose(q, (0, 3, 1, 2, 4)).reshape(bs, 2, DIFF_HEADS, DIFF_DH)
    z = jnp.zeros_like(q[:, 0])
    qt = jnp.concatenate([jnp.concatenate([q[:, 0], z], axis=-1),
                          jnp.concatenate([z, q[:, 1]], axis=-1)], axis=1)
    kn = jnp.tile(jnp.repeat(kb_new.reshape(bs, DIFF_KV_HEADS, 2 * DIFF_DH), DIFF_GROUP, axis=1), (1, 2, 1))
    vn = jnp.tile(jnp.repeat(vb_new.reshape(bs, DIFF_KV_HEADS, DIFF_DV), DIFF_GROUP, axis=1), (1, 2, 1))
    g3 = gate.astype(F32).reshape(bs, DIFF_HEADS, DIFF_DV)
    rows = 2 * DIFF_HEADS

    npair = nq // 2

    def seq(i, h, p):
        return (i * DIFF_KV_HEADS + h) * npair + p

    def page_spec(pg):
        return pl.BlockSpec((None, None, w, 2 * DIFF_DH),
                            lambda i, h, p, pt: (pt[seq(i, h, p), pg], layer, 0, 0))

    per_seq = lambda i, h, p, pt: (i, h)
    const2 = lambda i, h, p, pt: (0, 0)
    row_spec = pl.BlockSpec((None, rows, LANES), lambda i, h, p, pt: (seq(i, h, p), 0, 0))
    head_spec = pl.BlockSpec((None, DIFF_HEADS, DIFF_DV), lambda i, h, p, pt: (seq(i, h, p), 0, 0))
    in_specs = ([pl.BlockSpec((nq, DIFF_GROUP * DIFF_DV, blk), lambda i, h, p, pt: (i, h, 0)),
                 pl.BlockSpec((s, 2 * DIFF_DH), per_seq),
                 pl.BlockSpec((nq, VT_ROWS, blk), lambda i, h, p, pt: (i, h, 0)),
                 pl.BlockSpec((3, DIFF_GROUP, blk, blk), lambda i, h, p, pt: (0, h, 0, 0)),
                 pl.BlockSpec((s, DIFF_GROUP * DIFF_DV), lambda i, h, p, pt: (i, gcol0 + h)),
                 pl.BlockSpec((DIFF_DV, 1), const2),
                 pl.BlockSpec((4, DIFF_DH), const2)]
                + [page_spec(pg) for pg in range(n_pages)] + [page_spec(pg) for pg in range(n_pages)]
                + [row_spec, row_spec, row_spec,
                   pl.BlockSpec((rows, n_pages * w), const2),
                   pl.BlockSpec((rows, LANES), const2),
                   head_spec,
                   pl.BlockSpec((1, DIFF_DV), const2)])
    assert len(in_specs) == N_PROMPT_IN + 2 * n_pages + N_DECODE_IN
    do_p, do_s = pl.pallas_call(
        functools.partial(_attn_kernel, n_pages=n_pages, rows_per_page=w, blk=blk, nq=nq, lam0=lam0),
        grid_spec=pltpu.PrefetchScalarGridSpec(
            num_scalar_prefetch=1, grid=(b, DIFF_KV_HEADS, npair), in_specs=in_specs,
            out_specs=[pl.BlockSpec((s, DIFF_GROUP * DIFF_DV), per_seq), head_spec],
            scratch_shapes=[pltpu.VMEM((4, 2, DIFF_DV, blk), BF16),
                            pltpu.VMEM((4, 2, 1, blk), F32),
                            pltpu.VMEM((4, 2, VT_ROWS, blk), F32),
                            pltpu.VMEM((rows, n_pages * w), F32),
                            pltpu.VMEM((rows, n_pages * w), BF16)]),
        out_shape=[jax.ShapeDtypeStruct((t, DIFF_OUT_W), BF16),
                   jax.ShapeDtypeStruct((bs, DIFF_HEADS, DIFF_DV), F32)],
        compiler_params=_cparams(("arbitrary", "arbitrary", "arbitrary"), 56),
        name="attention",
    )(page_table, qt_p, kb_p, vt_p, bias_p, gates_p, subln_g.reshape(DIFF_DV, 1), lam_rows,
      *([ck] * n_pages), *([cv] * n_pages), qt, kn, vn, bias, bias_new, g3,
      subln_g.reshape(1, DIFF_DV))
    return do_p, do_s.reshape(bs, DIFF_OUT_W).astype(BF16)


def _softplus(x):
    return jnp.maximum(x, 0.0) + jnp.log1p(jnp.exp(-jnp.abs(x)))


def _lru_coeffs(xc, wg_ref, ba_ref, bx_ref, lam_ref, n):
    sl = slice(n * RNN_BLOCK_W, (n + 1) * RNN_BLOCK_W)
    xn = xc[:, sl]
    z = jnp.dot(xn.astype(BF16), wg_ref[n], preferred_element_type=F32)
    rg = _sigmoid(z[:, :RNN_BLOCK_W] + ba_ref[:, sl])
    ig = _sigmoid(z[:, RNN_BLOCK_W:] + bx_ref[:, sl])
    log_a = -LRU_C * rg * _softplus(-lam_ref[:, sl])
    a = jnp.exp(log_a)
    y = -jnp.tanh(log_a) * (1.0 + a * a)
    root = jnp.where(y > 0.0, y * lax.rsqrt(y), 0.0)
    bt = root * (ig * xn)
    return a, bt


def _lru_tile(ti, tm, x_ref, gate_ref, cw_ref, cb_ref, wg_ref, ba_ref, bx_ref, lam_ref,
              o_ref, conv_ref, xbuf_ref, a_ref, b_ref, h_ref):
    pad = SUBLANES
    rows = slice(ti * tm, (ti + 1) * tm)
    xbuf_ref[pad:pad + tm, :] = x_ref[rows, :]
    xc = cb_ref[...]
    for j in range(CONV_WIDTH):
        off = pad - (CONV_WIDTH - 1) + j
        xc = xc + xbuf_ref[off:off + tm, :] * cw_ref[j:j + 1, :]
    conv_ref[...] = xbuf_ref[pad + tm - (CONV_WIDTH - 1):pad + tm, :]
    xbuf_ref[0:pad, :] = xbuf_ref[tm:tm + pad, :]

    for n in range(RNN_BLOCKS):
        a, bt = _lru_coeffs(xc, wg_ref, ba_ref, bx_ref, lam_ref, n)
        sl = slice(n * RNN_BLOCK_W, (n + 1) * RNN_BLOCK_W)
        a = a.reshape(tm // SUBLANES, SUBLANES, RNN_BLOCK_W)
        bt = bt.reshape(tm // SUBLANES, SUBLANES, RNN_BLOCK_W)
        r8 = lax.broadcasted_iota(jnp.int32, a.shape, 1)
        for d in (1, 2, 4):
            a_sh = pltpu.roll(a, d, 1)
            b_sh = pltpu.roll(bt, d, 1)
            ok = r8 >= d
            bt = jnp.where(ok, a * b_sh + bt, bt)
            a = jnp.where(ok, a * a_sh, a)
        a_ref[:, sl] = a.reshape(tm, RNN_BLOCK_W)
        b_ref[:, sl] = bt.reshape(tm, RNN_BLOCK_W)

    h = h_ref[...]
    for gi in range(tm // SUBLANES):
        rs = slice(gi * SUBLANES, (gi + 1) * SUBLANES)
        hg = a_ref[rs, :] * h + b_ref[rs, :]
        b_ref[rs, :] = hg
        h = hg[SUBLANES - 1:SUBLANES, :]
    h_ref[...] = h
    o_ref[rows, :] = (b_ref[...] * gate_ref[rows, :].astype(F32)).astype(o_ref.dtype)


def _ret_lru_kernel(qk_ref, v_ref, gr_ref, dmask_ref, qdec_ref, kdec_ref,
                    x_ref, gl_ref, cw_ref, cb_ref, wg_ref, ba_ref, bx_ref, lam_ref,
                    o_ret_ref, sfin_ref, o_lru_ref, hlast_ref, conv_ref,
                    state_ref, xbuf_ref, a_ref, b_ref, h_ref, *, chunk_dec, tm):
    j = pl.program_id(1)

    @pl.when(j == 0)
    def _():
        state_ref[...] = jnp.zeros_like(state_ref)
        xbuf_ref[0:SUBLANES, :] = jnp.zeros((SUBLANES, RNN_WIDTH), F32)
        h_ref[...] = jnp.zeros(h_ref.shape, F32)

    rows = qk_ref.shape[0]
    n_chunks = rows // dmask_ref.shape[-1]
    n_tiles = rows // tm
    ret = [functools.partial(_retention_chunk, ci, qk_ref, v_ref, gr_ref, dmask_ref, qdec_ref,
                             kdec_ref, o_ret_ref, state_ref, chunk_dec) for ci in range(n_chunks)]
    lru = [functools.partial(_lru_tile, ti, tm, x_ref, gl_ref, cw_ref, cb_ref, wg_ref, ba_ref, bx_ref,
                             lam_ref, o_lru_ref, conv_ref, xbuf_ref, a_ref, b_ref, h_ref)
           for ti in range(n_tiles)]
    per_tile = n_chunks // n_tiles
    for ti in range(n_tiles):
        ret[ti * per_tile]()
        lru[ti]()
        for piece in ret[ti * per_tile + 1:(ti + 1) * per_tile]:
            piece()
    hlast_ref[...] = h_ref[...]

    @pl.when(j == pl.num_programs(1) - 1)
    def _():
        sfin_ref[...] = state_ref[...]


def _ret_lru_prompt(qk, rv, gates, lx, conv_w, conv_b, wg, ba, bx, lam, b, s, tm):
    c = math.gcd(s, RET_CHUNK)
    dmask, qdec, kdec, chunk_dec = _retention_decays(c)
    rows = c * math.gcd(s // c, RET_CHUNKS_PER_STEP)
    assert rows % tm == 0 and (rows // c) % (rows // tm) == 0
    nt = s // rows
    t = b * s
    lcol = 2
    row_blk = lambda w, col=0: pl.BlockSpec((rows, w), lambda i, j: (i * nt + j, col))
    const2 = lambda shape: pl.BlockSpec(shape, lambda i, j: (0, 0))
    vec = const2((1, RNN_WIDTH))
    return pl.pallas_call(
        functools.partial(_ret_lru_kernel, chunk_dec=chunk_dec, tm=tm),
        grid=(b, nt),
        in_specs=[row_blk(2 * RET_QK_W), row_blk(RET_V_W), row_blk(RET_V_W),
                  pl.BlockSpec((RET_HEADS, c, c), lambda i, j: (0, 0, 0)),
                  const2((c, RET_HEADS)), const2((c, RET_HEADS)),
                  row_blk(RNN_WIDTH), row_blk(RNN_WIDTH, lcol),
                  const2((CONV_WIDTH, RNN_WIDTH)), vec,
                  pl.BlockSpec((RNN_BLOCKS, RNN_BLOCK_W, 2 * RNN_BLOCK_W), lambda i, j: (0, 0, 0)),
                  vec, vec, vec],
        out_specs=[row_blk(RET_V_W),
                   pl.BlockSpec((None, RET_HEADS, RET_DK, RET_DV), lambda i, j: (i, 0, 0, 0)),
                   row_blk(RNN_WIDTH),
                   pl.BlockSpec((None, 1, RNN_WIDTH), lambda i, j: (i, 0, 0)),
                   pl.BlockSpec((None, CONV_WIDTH - 1, RNN_WIDTH), lambda i, j: (i, 0, 0))],
        out_shape=[jax.ShapeDtypeStruct((t, RET_V_W), BF16),
                   jax.ShapeDtypeStruct((b, RET_HEADS, RET_DK, RET_DV), F32),
                   jax.ShapeDtypeStruct((t, RNN_WIDTH), BF16),
                   jax.ShapeDtypeStruct((b, 1, RNN_WIDTH), F32),
                   jax.ShapeDtypeStruct((b, CONV_WIDTH - 1, RNN_WIDTH), F32)],
        scratch_shapes=[pltpu.VMEM((RET_HEADS, RET_DK, RET_DV), F32),
                        pltpu.VMEM((tm + SUBLANES, RNN_WIDTH), F32),
                        pltpu.VMEM((tm, RNN_WIDTH), F32),
                        pltpu.VMEM((tm, RNN_WIDTH), F32),
                        pltpu.VMEM((1, RNN_WIDTH), F32)],
        compiler_params=_cparams(("parallel", "arbitrary"), 40),
        name="ret_lru_prompt",
    )(qk, rv, gates, dmask, qdec, kdec, lx, gates, conv_w, conv_b.reshape(1, -1), wg,
      ba.reshape(1, -1), bx.reshape(1, -1), lam.reshape(1, -1))


def _lru_step_kernel(x_ref, c0_ref, c1_ref, c2_ref, h0_ref, gate_ref, cw_ref, cb_ref, wg_ref,
                     ba_ref, bx_ref, lam_ref, o_ref, h_ref):
    xc = (cb_ref[...] + c0_ref[...] * cw_ref[0:1, :] + c1_ref[...] * cw_ref[1:2, :]
          + c2_ref[...] * cw_ref[2:3, :] + x_ref[...] * cw_ref[3:4, :])
    for n in range(RNN_BLOCKS):
        a, bt = _lru_coeffs(xc, wg_ref, ba_ref, bx_ref, lam_ref, n)
        sl = slice(n * RNN_BLOCK_W, (n + 1) * RNN_BLOCK_W)
        h = a * h0_ref[:, sl] + bt
        h_ref[:, sl] = h
        o_ref[:, sl] = (h * gate_ref[:, sl].astype(F32)).astype(o_ref.dtype)


def _lru_step(lx, conv_prev, h0, gate, conv_w, conv_b, wg, ba, bx, lam):
    bs = lx.shape[0]
    vm = pl.BlockSpec(memory_space=pltpu.VMEM)
    return pl.pallas_call(
        _lru_step_kernel,
        in_specs=[vm] * 12,
        out_specs=[vm, vm],
        out_shape=[jax.ShapeDtypeStruct((bs, RNN_WIDTH), BF16),
                   jax.ShapeDtypeStruct((bs, RNN_WIDTH), F32)],
        compiler_params=pltpu.CompilerParams(vmem_limit_bytes=32 * MIB),
        name="lru_step",
    )(lx, conv_prev[:, 0], conv_prev[:, 1], conv_prev[:, 2], h0, gate, conv_w,
      conv_b.reshape(1, -1), wg, ba.reshape(1, -1), bx.reshape(1, -1), lam.reshape(1, -1))


def _merge_kernel(ro_ref, do_ref, lo_ref, mg_ref, x_ref, p_ref, wr_ref, wd_ref, wl_ref, wo_ref,
                  wpg_ref, wp_ref, gn_ref, x_out_ref, hn_ref):
    d = D_MODEL
    tm = x_ref.shape[0]
    chunk = min(MERGE_CHUNK, tm)
    chunks = [slice(c * chunk, (c + 1) * chunk) for c in range(tm // chunk)]
    dot = functools.partial(jnp.dot, preferred_element_type=F32)
    merged = [(mg_ref[rs, 0:d].astype(F32) * dot(ro_ref[rs, :], wr_ref[...])
               + mg_ref[rs, d:2 * d].astype(F32) * dot(do_ref[rs, :], wd_ref[...])
               + mg_ref[rs, 2 * d:3 * d].astype(F32) * dot(lo_ref[rs, :], wl_ref[...])).astype(BF16)
              for rs in chunks]
    ple = [dot(p_ref[rs, :].astype(BF16), wp_ref[...]) for rs in chunks]
    x1 = [x_ref[rs, :] + dot(m, wo_ref[...]) for rs, m in zip(chunks, merged)]
    gate = [dot(v.astype(BF16), wpg_ref[...]) for v in x1]
    for rs, v, g, e in zip(chunks, x1, gate, ple):
        x2 = v + _sigmoid(g) * e
        x_out_ref[rs, :] = x2
        y = x2 * lax.rsqrt(jnp.mean(x2 * x2, axis=-1, keepdims=True) + NORM_EPS)
        hn_ref[rs, :] = (y * gn_ref[...]).astype(hn_ref.dtype)


def _merge(ro, do, lo, mg, x, p_all, layer, wr, wd, wl, wo, wpg, wp, g_next, hn_dtype, tm):
    t = x.shape[0]
    d = D_MODEL
    row = lambda wdt: pl.BlockSpec((tm, wdt), lambda i: (i, 0))
    full = lambda a: pl.BlockSpec(a.shape, lambda i: (0, 0), pipeline_mode=pl.Buffered(1))
    return pl.pallas_call(
        _merge_kernel,
        grid=(t // tm,),
        in_specs=[row(RET_V_W), row(DIFF_OUT_W), row(RNN_WIDTH), row(3 * d), row(d),
                  pl.BlockSpec((None, tm, D_PLE), lambda i: (layer, i, 0)),
                  full(wr), full(wd), full(wl), full(wo), full(wpg), full(wp),
                  pl.BlockSpec((1, d), lambda i: (0, 0))],
        out_specs=[row(d), row(d)],
        out_shape=[jax.ShapeDtypeStruct((t, d), F32), jax.ShapeDtypeStruct((t, d), hn_dtype)],
        compiler_params=_cparams(("parallel",), 56),
        name="merge",
    )(ro, do, lo, mg, x, p_all, wr, wd, wl, wo, wpg, wp, g_next.reshape(1, d))


def _lambda_init(layer):
    return 0.8 - 0.6 * math.exp(-0.3 * layer)


def _rope_tables(pos):
    half = RET_DK // 2
    inv = ROPE_BASE ** (-jnp.arange(half, dtype=F32) / half)
    ang = pos.astype(F32)[:, None] * inv[None, :]
    cos = jnp.cos(ang)
    sin = jnp.sin(ang)
    return jnp.concatenate([cos, cos], axis=-1), jnp.concatenate([-sin, sin], axis=-1)


def kernel(x_prompt, x_sample, cache_k, cache_v, state_ret, state_lru, state_conv, page_table, p_prompt, p_sample, rel_bias, norm_g, w_in, lambda_q1, lambda_k1, lambda_q2, lambda_k2, subln_g, conv_w, conv_b, gate_a_w, gate_a_b, gate_x_w, gate_x_b, lru_lambda, w_ret_out, w_diff_out, w_lru_out, w_o, w_ple, w_ple_gate, final_norm_g):
    bp, sp, d = x_prompt.shape
    bs, ss, _ = x_sample.shape
    assert d == D_MODEL and ss == 1
    depth = w_in.shape[0]
    n_pages, page = page_table.shape[1], cache_k.shape[2]
    past = n_pages * page
    tp = bp * sp
    blk = min(ATTN_BLOCK, sp)
    tm_p = min(PROJ_ROWS, sp)
    tm_qkv = min(QKV_ROWS, sp)
    tm_lru = min(LRU_ROWS, sp)
    tm_merge = min(MERGE_ROWS, sp)
    p_prompt_all = p_prompt.reshape(depth, tp, D_PLE)
    p_sample_all = p_sample.reshape(depth, bs, D_PLE)

    cos_p, sin_p = _rope_tables(jnp.arange(sp, dtype=jnp.int32))
    cos_s, sin_s = _rope_tables(jnp.full((bs,), past, dtype=jnp.int32))
    bias_p = _bias_prompt(rel_bias, blk)
    bias_s, bias_s_new = _bias_sample(rel_bias, past)

    w_in_b = w_in.astype(BF16)
    wg_all = jnp.concatenate([gate_a_w, gate_x_w], axis=-1).astype(BF16)
    xp = x_prompt.reshape(tp, d)
    xs = x_sample.reshape(bs, d)
    hn_p = _rmsnorm(xp, norm_g[0], BF16, tm_p)
    hn_s = _rmsnorm(xs, norm_g[0], BF16, bs)

    outs = {k: [] for k in ("ks", "vs", "rp", "lp", "ls", "cp", "cs")}
    kp_all = jnp.zeros((bp, depth, sp * DIFF_KV_HEADS, 2 * DIFF_DH), F32)
    vp_all = jnp.zeros((bp, depth, sp * DIFF_KV_HEADS, DIFF_DV), F32)
    rs_all = jnp.zeros(state_ret.shape, F32)
    for l in range(depth):
        wg = wg_all[l]
        wr, wd, wl = w_ret_out[l].astype(BF16), w_diff_out[l].astype(BF16), w_lru_out[l].astype(BF16)
        wo, wpg, wp = w_o[l].astype(BF16), w_ple_gate[l].astype(BF16), w_ple[l].astype(BF16)
        lam0 = _lambda_init(l)
        lam_rows = jnp.stack([lambda_q1[l], lambda_k1[l], lambda_q2[l], lambda_k2[l]], axis=0)
        last = l == depth - 1
        g_next = final_norm_g if last else norm_g[l + 1]
        hn_dtype = F32 if last else BF16

        qk = _proj(hn_p, w_in_b, l, COL_ROT, "rotary", tm_p, tables=(cos_p, sin_p))
        rv = _proj(hn_p, w_in_b, l, COL_RV, "plain", tm_p)
        gates = _proj(hn_p, w_in_b, l, COL_SILU, "silu", tm_p)
        kp_all, vp_all, kb, qt, vt = _proj_qkv_t(hn_p, w_in_b, l, tm_qkv, blk, kp_all, vp_all)
        lx = _proj(hn_p, w_in_b, l, COL_LX, "plain", tm_p, out_dtype=F32)
        mg = _proj(hn_p, w_in_b, l, COL_MG, "sigmoid", tm_p)
        qk_s = _proj(hn_s, w_in_b, l, COL_ROT, "rotary", bs, tables=(cos_s, sin_s))
        rvdq_s = _proj(hn_s, w_in_b, l, COL_RV_DQ, "plain", bs)
        gates_s = _proj(hn_s, w_in_b, l, COL_SILU, "silu", bs)
        kf_s, vf_s, kb_s, vb_s = _proj(hn_s, w_in_b, l, (COL_KV,), "kv", bs)
        lx_s = _proj(hn_s, w_in_b, l, COL_LX, "plain", bs, out_dtype=F32)
        mg_s = _proj(hn_s, w_in_b, l, COL_MG, "sigmoid", bs)

        do, do_s = _attention(qt, kb, vt, bias_p, gates, bp, sp, blk,
                              rvdq_s[:, RET_V_W:], kb_s, vb_s, gates_s[:, RET_V_W:RET_V_W + DIFF_OUT_W],
                              cache_k, cache_v, page_table, bias_s, bias_s_new, subln_g[l],
                              lam_rows, lam0, l)

        ro, s_fin, lo, h_last, conv_new = _ret_lru_prompt(
            qk, rv, gates, lx, conv_w[l], conv_b[l], wg, gate_a_b[l], gate_x_b[l], lru_lambda[l],
            bp, sp, tm_lru)
        xp, hn_p = _merge(ro, do, lo, mg, xp, p_prompt_all, l, wr, wd, wl, wo, wpg, wp,
                          g_next, hn_dtype, tm_merge)
        outs["rp"].append(s_fin)
        outs["lp"].append(h_last.reshape(bp, RNN_WIDTH))
        outs["cp"].append(conv_new)

        lx = lx_s
        ro, rs_all = _retention_step(qk_s, rvdq_s, gates_s, state_ret, rs_all, l)
        conv_prev = state_conv[:, l]
        lo, h_new = _lru_step(lx, conv_prev, state_lru[:, l], gates_s[:, 2 * RET_V_W:], conv_w[l],
                              conv_b[l], wg, gate_a_b[l], gate_x_b[l], lru_lambda[l])
        xs, hn_s = _merge(ro, do_s, lo, mg_s, xs, p_sample_all, l, wr, wd, wl, wo, wpg, wp,
                          g_next, hn_dtype, bs)
        outs["ks"].append(kf_s.reshape(bs, 1, DIFF_KV_HEADS, 2 * DIFF_DH))
        outs["vs"].append(vf_s.reshape(bs, 1, DIFF_KV_HEADS, DIFF_DV))
        outs["ls"].append(h_new)
        outs["cs"].append(jnp.concatenate([conv_prev[:, 1:], lx[:, None, :]], axis=1))

    st = lambda key: jnp.stack(outs[key], axis=1)
    return (hn_p.reshape(bp, sp, d), hn_s.reshape(bs, 1, d),
            kp_all.reshape(bp, depth, sp, DIFF_KV_HEADS, 2 * DIFF_DH),
            vp_all.reshape(bp, depth, sp, DIFF_KV_HEADS, DIFF_DV), st("ks"), st("vs"),
            st("rp"), rs_all, st("lp"), st("ls"), st("cp"), st("cs"))
```

```python
import functools
import math

import numpy as np
import jax
import jax.numpy as jnp
from jax import lax
from jax.experimental import pallas as pl
from jax.experimental.pallas import tpu as pltpu

F32 = jnp.float32
BF16 = jnp.bfloat16

D_MODEL = 1024
D_PLE = 256
NORM_EPS = 1e-6
RET_HEADS = 4
RET_DK = 128
RET_DV = 256
RET_CHUNK = 128
RET_CHUNKS_PER_STEP = 4
ROPE_BASE = 10000.0
DIFF_HEADS = 8
DIFF_KV_HEADS = 4
DIFF_GROUP = DIFF_HEADS // DIFF_KV_HEADS
DIFF_DH = 64
DIFF_DV = 2 * DIFF_DH
REL_BUCKETS = 32
REL_MAX_DIST = 128
RNN_WIDTH = 1024
RNN_BLOCKS = 8
RNN_BLOCK_W = RNN_WIDTH // RNN_BLOCKS
CONV_WIDTH = 4
LRU_C = 8.0

RET_QK_W = RET_HEADS * RET_DK
RET_V_W = RET_HEADS * RET_DV
DIFF_Q_W = DIFF_HEADS * 2 * DIFF_DH
DIFF_K_W = DIFF_KV_HEADS * 2 * DIFF_DH
DIFF_V_W = DIFF_KV_HEADS * DIFF_DV
DIFF_OUT_W = DIFF_HEADS * DIFF_DV
IN_SPLITS = (RET_QK_W, RET_QK_W, RET_V_W, RET_V_W,
             DIFF_Q_W, DIFF_K_W, DIFF_V_W, DIFF_OUT_W,
             RNN_WIDTH, RNN_WIDTH, 3 * D_MODEL)
IN_OFFS = tuple(int(v) for v in np.cumsum((0,) + IN_SPLITS))

LANES = 128
SUBLANES = 8
MASK_VALUE = -1e30
LOG2E = math.log2(math.e)
VT_ONES = 2 * SUBLANES
VT_ROWS = DIFF_DV + VT_ONES
BIAS_DIAG, BIAS_PREV, BIAS_ZERO = 0, 1, 2
MERGE_CHUNK = 256
ATTN_BLOCK = 256
PROJ_ROWS = 1024
QKV_ROWS = 512
LRU_ROWS = 256
MERGE_ROWS = 512
MIB = 1024 * 1024


def _cparams(sem, vmem_mib):
    return pltpu.CompilerParams(dimension_semantics=sem, vmem_limit_bytes=int(vmem_mib * MIB))


def _rmsnorm_kernel(x_ref, g_ref, o_ref):
    x = x_ref[...]
    y = x * lax.rsqrt(jnp.mean(x * x, axis=-1, keepdims=True) + NORM_EPS)
    o_ref[...] = (y * g_ref[...]).astype(o_ref.dtype)


def _rmsnorm(x, g, out_dtype, tm):
    t, d = x.shape
    return pl.pallas_call(
        _rmsnorm_kernel,
        grid=(t // tm,),
        in_specs=[pl.BlockSpec((tm, d), lambda i: (i, 0)),
                  pl.BlockSpec((1, d), lambda i: (0, 0))],
        out_specs=pl.BlockSpec((tm, d), lambda i: (i, 0)),
        out_shape=jax.ShapeDtypeStruct((t, d), out_dtype),
        compiler_params=_cparams(("parallel",), 32),
        name="rmsnorm",
    )(x, g.reshape(1, d))


def _proj_qkv_t_kernel(x_ref, wq_ref, wkv_ref, k_all_ref, v_all_ref, kf_ref, vf_ref, kb_ref,
                       qt_ref, vt_ref):
    del k_all_ref, v_all_ref
    blk = qt_ref.shape[-1]

    def matmuls(j):
        x = x_ref[j * blk:(j + 1) * blk, :]
        return (jnp.dot(x, wq_ref[...], preferred_element_type=F32),
                jnp.dot(x, wkv_ref[...], preferred_element_type=F32))

    def epilogue(j, q, kv):
        rows = slice(j * blk, (j + 1) * blk)
        k = kv[:, :DIFF_K_W]
        v = kv[:, DIFF_K_W:]
        for h in range(DIFF_KV_HEADS):
            dst = pl.ds(j * blk * DIFF_KV_HEADS + h, blk, stride=DIFF_KV_HEADS)
            kf_ref[dst, :] = k[:, h * 2 * DIFF_DH:(h + 1) * 2 * DIFF_DH]
            vf_ref[dst, :] = v[:, h * DIFF_DV:(h + 1) * DIFF_DV]
        kb_ref[rows, :] = k.astype(BF16)
        qt_ref[j] = (q * (DIFF_DH ** -0.5 * LOG2E)).T.astype(BF16)
        vt = v.T.astype(BF16)
        for h in range(DIFF_KV_HEADS):
            vt_ref[j, h * VT_ROWS:h * VT_ROWS + DIFF_DV, :] = vt[h * DIFF_DV:(h + 1) * DIFF_DV]
            vt_ref[j, h * VT_ROWS + DIFF_DV:(h + 1) * VT_ROWS, :] = jnp.ones((VT_ONES, blk), BF16)

    _pipeline_rows(qt_ref.shape[0], matmuls, epilogue)


def _pipeline_rows(n_chunks, matmuls, epilogue):
    pending = {}
    for c in range(n_chunks + 1):
        if c < n_chunks:
            pending[c] = matmuls(c)
        if c >= 1:
            epilogue(c - 1, *pending.pop(c - 1))


def _sigmoid(x):
    return 0.5 * jnp.tanh(0.5 * x) + 0.5


def _proj_epilogue(kind, acc, rows, refs):
    if kind == "plain":
        (o_ref,) = refs
        o_ref[rows, :] = acc.astype(o_ref.dtype)
    elif kind == "silu":
        (o_ref,) = refs
        o_ref[rows, :] = (acc * _sigmoid(acc)).astype(o_ref.dtype)
    elif kind == "sigmoid":
        (o_ref,) = refs
        o_ref[rows, :] = _sigmoid(acc).astype(o_ref.dtype)
    elif kind == "rotary":
        cos_ref, sin_ref, o_ref = refs
        cos = cos_ref[rows, :]
        sin = sin_ref[rows, :]
        for h in range(2 * RET_HEADS):
            z = acc[:, h * RET_DK:(h + 1) * RET_DK]
            r = z * cos + pltpu.roll(z, RET_DK // 2, 1) * sin
            if h >= RET_HEADS:
                r = r * (RET_DK ** -0.5)
            o_ref[rows, h * RET_DK:(h + 1) * RET_DK] = r.astype(o_ref.dtype)
    elif kind == "kv":
        kf_ref, vf_ref, kb_ref, vb_ref = refs
        k = acc[:, :DIFF_K_W]
        v = acc[:, DIFF_K_W:]
        kf_ref[rows, :] = k
        vf_ref[rows, :] = v
        kb_ref[rows, :] = k.astype(BF16)
        vb_ref[rows, :] = v.astype(BF16)
    else:
        raise ValueError(kind)


def _proj_kernel(x_ref, w_ref, *refs, kind, chunk):
    tm = x_ref.shape[0]

    def matmuls(c):
        return (jnp.dot(x_ref[c * chunk:(c + 1) * chunk, :], w_ref[...], preferred_element_type=F32),)

    def epilogue(c, acc):
        _proj_epilogue(kind, acc, slice(c * chunk, (c + 1) * chunk), refs)

    _pipeline_rows(tm // chunk, matmuls, epilogue)


SAMPLE_BLOCKS = (("rotary", "qk"), ("plain", "rvdq"), ("silu", "gates"), ("plain", "rvdq"),
                 ("kv", "kv"), ("silu", "gates"), ("plain", "lx"), ("silu", "gates"),
                 ("sigmoid", "mg"), ("sigmoid", "mg"), ("sigmoid", "mg"))


def _proj_sample_kernel(x_ref, w_ref, cos_ref, sin_ref, qk_ref, rvdq_ref, gates_ref, kf_ref, vf_ref,
                        kb_ref, vb_ref, lx_ref, mg_ref):
    j = pl.program_id(0)
    acc = jnp.dot(x_ref[...], w_ref[...], preferred_element_type=F32)
    outs = {"qk": (cos_ref, sin_ref, qk_ref), "rvdq": (rvdq_ref,), "gates": (gates_ref,),
            "kv": (kf_ref, vf_ref, kb_ref, vb_ref), "lx": (lx_ref,), "mg": (mg_ref,)}
    for col, (kind, name) in enumerate(SAMPLE_BLOCKS):
        pl.when(j == col)(functools.partial(_proj_epilogue, kind, acc, slice(None), outs[name]))


def _proj_sample(hn, w_all, layer, cos, sin):
    bs, d = hn.shape
    tn = PROJ_TN
    assert w_all.shape[2] == tn * len(SAMPLE_BLOCKS)

    def block_of(name):
        cols = [c for c, (_, n) in enumerate(SAMPLE_BLOCKS) if n == name]
        return lambda j: (0, sum(jnp.where(j >= c, 1, 0) for c in cols[1:]))

    def out(name, width, dtype):
        n_blocks = sum(1 for _, n in SAMPLE_BLOCKS if n == name)
        return (pl.BlockSpec((bs, width), block_of(name)),
                jax.ShapeDtypeStruct((bs, width * n_blocks), dtype))

    specs = [out("qk", tn, BF16), out("rvdq", tn, BF16), out("gates", tn, BF16),
             out("kv", DIFF_K_W, F32), out("kv", DIFF_V_W, F32), out("kv", DIFF_K_W, BF16),
             out("kv", DIFF_V_W, BF16), out("lx", tn, F32), out("mg", tn, BF16)]
    fixed = lambda j: (0, 0)
    return pl.pallas_call(
        _proj_sample_kernel,
        grid=(len(SAMPLE_BLOCKS),),
        in_specs=[pl.BlockSpec((bs, d), fixed),
                  pl.BlockSpec((None, d, tn), lambda j: (layer, 0, j)),
                  pl.BlockSpec((bs, RET_DK), fixed), pl.BlockSpec((bs, RET_DK), fixed)],
        out_specs=[s for s, _ in specs],
        out_shape=[o for _, o in specs],
        compiler_params=_cparams(("arbitrary",), 32),
        name="proj_sample",
    )(hn, w_all, cos, sin)


PROJ_TN = 1024
PROJ_CHUNK = 256
COL_ROT = (0,)
COL_RV = (1,)
COL_SILU = (2, 5, 7)
COL_DQ = 3
COL_KV = 4
COL_LX = (6,)
COL_MG = (8, 9, 10)
assert IN_OFFS[2] == PROJ_TN and IN_OFFS[5] == COL_KV * PROJ_TN and IN_OFFS[10] == COL_MG[0] * PROJ_TN


def _col_index(cols, j):
    if len(cols) == 1:
        return cols[0]
    step = cols[1] - cols[0]
    idx = cols[0] + step * j
    if cols[-1] != cols[0] + step * (len(cols) - 1):
        idx = idx - j // (len(cols) - 1)
    return idx


def _proj_qkv_t(hn, w_all, layer, tm, blk, k_all, v_all):
    t, d = hn.shape
    nj = tm // blk
    per_seq = k_all.shape[2] // (tm * DIFF_KV_HEADS)
    half = pl.BlockSpec((tm, DIFF_K_W), lambda i: (i, 0))
    per_head = pl.BlockSpec((None, None, tm * DIFF_KV_HEADS, DIFF_DV),
                            lambda i: (i // per_seq, layer, i % per_seq, 0))
    untouched = pl.BlockSpec(memory_space=pl.ANY)
    return pl.pallas_call(
        _proj_qkv_t_kernel,
        grid=(t // tm,),
        in_specs=[pl.BlockSpec((tm, d), lambda i: (i, 0)),
                  pl.BlockSpec((None, d, PROJ_TN), lambda i: (layer, 0, COL_DQ)),
                  pl.BlockSpec((None, d, PROJ_TN), lambda i: (layer, 0, COL_KV)),
                  untouched, untouched],
        out_specs=[per_head, per_head, half,
                   pl.BlockSpec((nj, DIFF_Q_W, blk), lambda i: (i, 0, 0)),
                   pl.BlockSpec((nj, DIFF_KV_HEADS * VT_ROWS, blk), lambda i: (i, 0, 0))],
        out_shape=[jax.ShapeDtypeStruct(k_all.shape, F32),
                   jax.ShapeDtypeStruct(v_all.shape, F32),
                   jax.ShapeDtypeStruct((t, DIFF_K_W), BF16),
                   jax.ShapeDtypeStruct((t // blk, DIFF_Q_W, blk), BF16),
                   jax.ShapeDtypeStruct((t // blk, DIFF_KV_HEADS * VT_ROWS, blk), BF16)],
        input_output_aliases={3: 0, 4: 1},
        compiler_params=_cparams(("parallel",), 48),
        name="proj_qkv_t",
    )(hn, w_all, w_all, k_all, v_all)


def _proj(hn, w_all, layer, cols, kind, tm, out_dtype=BF16, tables=None):
    t, d = hn.shape
    tn = PROJ_TN
    n = tn * len(cols)
    grid = (len(cols), t // tm)
    in_specs = [pl.BlockSpec((tm, d), lambda j, i: (i, 0)),
                pl.BlockSpec((None, d, tn), lambda j, i: (layer, 0, _col_index(cols, j)))]
    args = [hn, w_all]
    if kind == "rotary":
        cos, sin = tables
        nb = cos.shape[0] // tm
        spec = pl.BlockSpec((tm, RET_DK), lambda j, i: (i % nb, 0))
        in_specs += [spec, spec]
        args += [cos, sin]
    out_specs = pl.BlockSpec((tm, tn), lambda j, i: (i, j))
    out_shape = jax.ShapeDtypeStruct((t, n), out_dtype)
    return pl.pallas_call(
        functools.partial(_proj_kernel, kind=kind, chunk=min(PROJ_CHUNK, tm)),
        grid=grid, in_specs=in_specs, out_specs=out_specs, out_shape=out_shape,
        compiler_params=_cparams(("parallel", "parallel"), 40),
        name="proj_" + kind,
    )(*args)


def _retention_chunk(ci, qk_ref, v_ref, gate_ref, dmask_ref, qdec_ref, kdec_ref, o_ref, state_ref,
                     chunk_dec):
    heads = range(RET_HEADS)
    dot = functools.partial(jnp.dot, preferred_element_type=F32)
    cw = dmask_ref.shape[-1]
    rows = slice(ci * cw, (ci + 1) * cw)
    q = [qk_ref[rows, h * RET_DK:(h + 1) * RET_DK] for h in heads]
    k = [qk_ref[rows, RET_QK_W + h * RET_DK:RET_QK_W + (h + 1) * RET_DK] for h in heads]
    v = [v_ref[rows, h * RET_DV:(h + 1) * RET_DV] for h in heads]
    s = [lax.dot_general(q[h], k[h], (((1,), (1,)), ((), ())), preferred_element_type=F32)
         for h in heads]
    cross = [dot(q[h], state_ref[h].astype(BF16)) for h in heads]
    kd = [(k[h].astype(F32) * kdec_ref[:, h:h + 1]).astype(BF16) for h in heads]
    kv = [lax.dot_general(kd[h], v[h], (((0,), (0,)), ((), ())), preferred_element_type=F32)
          for h in heads]
    inner = [dot((s[h] * dmask_ref[h]).astype(BF16), v[h]) for h in heads]
    for h in heads:
        state_ref[h] = state_ref[h] * chunk_dec[h] + kv[h]
        o = inner[h] + cross[h] * qdec_ref[:, h:h + 1]
        o = o * lax.rsqrt(jnp.mean(o * o, axis=-1, keepdims=True) + NORM_EPS)
        g = gate_ref[rows, h * RET_DV:(h + 1) * RET_DV].astype(F32)
        o_ref[rows, h * RET_DV:(h + 1) * RET_DV] = (o * g).astype(o_ref.dtype)


def _retention_decays(c):
    log_g = np.log1p(-(2.0 ** (-5.0 - np.arange(RET_HEADS, dtype=np.float64))))
    idx = np.arange(c, dtype=np.float64)
    dist = idx[:, None] - idx[None, :]
    dmask = np.where(dist[None] >= 0, np.exp(np.maximum(dist, 0.0)[None] * log_g[:, None, None]), 0.0)
    qdec = np.exp((idx + 1.0)[:, None] * log_g[None, :])
    kdec = np.exp((c - 1.0 - idx)[:, None] * log_g[None, :])
    chunk_dec = tuple(float(x) for x in np.exp(c * log_g).astype(np.float32))
    return jnp.asarray(dmask, F32), jnp.asarray(qdec, F32), jnp.asarray(kdec, F32), chunk_dec


def _retention_step_kernel(qk_ref, v_ref, gate_ref, s_ref, s_all_ref, o_ref, snew_ref, *, decay):
    del s_all_ref
    sb = qk_ref.shape[0]
    rowid = lax.broadcasted_iota(jnp.int32, (sb, 1), 0)

    def per_sequence(b, outs):
        new_outs = []
        for h in range(RET_HEADS):
            q = qk_ref[:, h * RET_DK:(h + 1) * RET_DK]
            k = qk_ref[:, RET_QK_W + h * RET_DK:RET_QK_W + (h + 1) * RET_DK]
            v = v_ref[:, h * RET_DV:(h + 1) * RET_DV]
            st = s_ref[b, h]
            qs = jnp.dot(q, st.astype(BF16), preferred_element_type=F32)
            new_outs.append(outs[h] + jnp.where(rowid == b, decay[h] * qs, 0.0))
            kb = jnp.where(rowid == b, k, jnp.zeros_like(k))
            snew_ref[b, h] = st * decay[h] + lax.dot_general(
                kb, v, (((0,), (0,)), ((), ())), preferred_element_type=F32)
        return tuple(new_outs)

    outs = []
    for h in range(RET_HEADS):
        q = qk_ref[:, h * RET_DK:(h + 1) * RET_DK].astype(F32)
        k = qk_ref[:, RET_QK_W + h * RET_DK:RET_QK_W + (h + 1) * RET_DK].astype(F32)
        v = v_ref[:, h * RET_DV:(h + 1) * RET_DV].astype(F32)
        outs.append(jnp.sum(q * k, axis=1, keepdims=True) * v)
    outs = lax.fori_loop(0, sb, per_sequence, tuple(outs))
    for h in range(RET_HEADS):
        o = outs[h]
        o = o * lax.rsqrt(jnp.mean(o * o, axis=-1, keepdims=True) + NORM_EPS)
        g = gate_ref[:, h * RET_DV:(h + 1) * RET_DV].astype(F32)
        o_ref[:, h * RET_DV:(h + 1) * RET_DV] = (o * g).astype(o_ref.dtype)


def _retention_step(qk, rvdq, gates, state_ret, state_new, layer):
    bs = qk.shape[0]
    sb = min(2 * SUBLANES, bs)
    log_g = np.log1p(-(2.0 ** (-5.0 - np.arange(RET_HEADS, dtype=np.float64))))
    decay = tuple(float(x) for x in np.exp(log_g).astype(np.float32))
    layer_state = pl.BlockSpec((sb, None, RET_HEADS, RET_DK, RET_DV), lambda i: (i, layer, 0, 0, 0))
    return pl.pallas_call(
        functools.partial(_retention_step_kernel, decay=decay),
        grid=(bs // sb,),
        in_specs=[pl.BlockSpec((sb, 2 * RET_QK_W), lambda i: (i, 0)),
                  pl.BlockSpec((sb, RET_V_W), lambda i: (i, 0)),
                  pl.BlockSpec((sb, RET_V_W), lambda i: (i, 0)),
                  layer_state,
                  pl.BlockSpec(memory_space=pl.ANY)],
        out_specs=[pl.BlockSpec((sb, RET_V_W), lambda i: (i, 0)), layer_state],
        out_shape=[jax.ShapeDtypeStruct((bs, RET_V_W), BF16),
                   jax.ShapeDtypeStruct(state_new.shape, F32)],
        input_output_aliases={4: 1},
        compiler_params=_cparams(("parallel",), 48),
        name="retention_step",
    )(qk, rvdq, gates, state_ret, state_new)


def _t5_bucket(rel):
    n = jnp.maximum(rel, 0)
    max_exact = REL_BUCKETS // 2
    nf = jnp.maximum(n.astype(F32), 1.0)
    large = max_exact + (jnp.log(nf / max_exact) / math.log(REL_MAX_DIST / max_exact)
                         * (REL_BUCKETS - max_exact)).astype(jnp.int32)
    large = jnp.minimum(large, REL_BUCKETS - 1)
    return jnp.where(n < max_exact, n, large)


def _bias_lookup(tab_ref, bucket, head):
    out = jnp.zeros(bucket.shape, F32)
    far = tab_ref[REL_BUCKETS - 1, head]
    for bkt in range(REL_BUCKETS - 1):
        out = jnp.where(bucket == bkt, tab_ref[bkt, head] - far, out)
    return out


def _bias_prompt_kernel(tab_ref, o_ref, *, blk):
    i = lax.broadcasted_iota(jnp.int32, (blk, blk), 1)
    j = lax.broadcasted_iota(jnp.int32, (blk, blk), 0)
    for which in (BIAS_DIAG, BIAS_PREV):
        rel = i - j + which * blk
        bucket = _t5_bucket(rel)
        for h in range(DIFF_HEADS):
            o_ref[which, h] = jnp.where(rel >= 0, _bias_lookup(tab_ref, bucket, h) * LOG2E, MASK_VALUE)
    o_ref[BIAS_ZERO] = jnp.zeros(o_ref.shape[1:], F32)


def _bias_prompt(rel_bias, blk):
    return pl.pallas_call(
        functools.partial(_bias_prompt_kernel, blk=blk),
        in_specs=[pl.BlockSpec(memory_space=pltpu.SMEM)],
        out_specs=pl.BlockSpec(memory_space=pltpu.VMEM),
        out_shape=jax.ShapeDtypeStruct((3, DIFF_HEADS, blk, blk), F32),
        compiler_params=pltpu.CompilerParams(vmem_limit_bytes=32 * MIB),
        name="bias_prompt",
    )(rel_bias)


def _bias_sample_kernel(tab_ref, o_ref, onew_ref, *, past, width):
    rows = 2 * DIFF_HEADS
    r = lax.broadcasted_iota(jnp.int32, (rows, width), 0)
    lane = lax.broadcasted_iota(jnp.int32, (rows, width), 1)
    head = r % DIFF_HEADS
    kvh = lane % DIFF_KV_HEADS
    pos = lane // DIFF_KV_HEADS
    bucket = _t5_bucket(past - pos)
    out = jnp.zeros((rows, width), F32)
    for h in range(DIFF_HEADS):
        out = jnp.where(head == h, _bias_lookup(tab_ref, bucket, h), out)
    o_ref[...] = jnp.where(kvh == head // DIFF_GROUP, out, MASK_VALUE)
    rn = lax.broadcasted_iota(jnp.int32, (rows, LANES), 0) % DIFF_HEADS
    new = jnp.zeros((rows, LANES), F32)
    zero_bucket = _t5_bucket(jnp.zeros((rows, LANES), jnp.int32))
    for h in range(DIFF_HEADS):
        new = jnp.where(rn == h, _bias_lookup(tab_ref, zero_bucket, h), new)
    onew_ref[...] = new


def _bias_sample(rel_bias, past):
    width = past * DIFF_KV_HEADS
    return pl.pallas_call(
        functools.partial(_bias_sample_kernel, past=past, width=width),
        in_specs=[pl.BlockSpec(memory_space=pltpu.SMEM)],
        out_specs=[pl.BlockSpec(memory_space=pltpu.VMEM), pl.BlockSpec(memory_space=pltpu.VMEM)],
        out_shape=[jax.ShapeDtypeStruct((2 * DIFF_HEADS, width), F32),
                   jax.ShapeDtypeStruct((2 * DIFF_HEADS, LANES), F32)],
        compiler_params=pltpu.CompilerParams(vmem_limit_bytes=32 * MIB),
        name="bias_sample",
    )(rel_bias)


def _lambda_full(lam_ref, lam0):
    a = jnp.sum(lam_ref[0:1, :] * lam_ref[1:2, :], axis=-1, keepdims=True)
    b = jnp.sum(lam_ref[2:3, :] * lam_ref[3:4, :], axis=-1, keepdims=True)
    return jnp.exp(a) - jnp.exp(b) + lam0


def _attn_prompt_body(qt_in_ref, k_ref, vt_ref, bias_ref, gate_ref, sg_ref, lam_ref, o_ref,
                      qt_ref, m_ref, acc_ref, *, blk, nq, lam0, side_work=()):
    pair = pl.program_id(2)
    tiles = (pair, nq - 1 - pair)
    n_chunks = 2 * DIFF_GROUP
    row = lax.broadcasted_iota(jnp.int32, (DIFF_DV, blk), 0)
    for slot in range(2):
        for g in range(DIFF_GROUP):
            qg = qt_in_ref[tiles[slot], g * DIFF_DV:(g + 1) * DIFF_DV, :]
            for m in range(2):
                keep = (row < DIFF_DH) if m == 0 else (row >= DIFF_DH)
                c = m * DIFF_GROUP + g
                qt_ref[c, slot] = jnp.where(keep, qg, jnp.zeros_like(qg))
    m_ref[...] = jnp.full(m_ref.shape, MASK_VALUE, F32)
    acc_ref[...] = jnp.zeros(acc_ref.shape, F32)

    def run_blocks(blocks):
        s_lag = 3
        kbs = [k_ref[pl.ds(pl.multiple_of(ki * blk, blk), blk), :] for _, ki, _ in blocks]
        vts = [vt_ref[ki] for _, ki, _ in blocks]
        units = [(bi, c) for bi in range(len(blocks)) for c in range(n_chunks)]
        n_units = len(units)

        def scores(u):
            bi, c = units[u]
            slot, _, bias = blocks[bi]
            s = jnp.dot(kbs[bi], qt_ref[c, slot], preferred_element_type=F32)
            if bias is not None:
                s = s + bias_ref[bias, c % DIFF_GROUP]
            return s

        def softmax(u, s):
            bi, c = units[u]
            slot = blocks[bi][0]
            m_prev = m_ref[c, slot]
            m_new = jnp.maximum(m_prev, jnp.max(s, axis=0, keepdims=True))
            p = jnp.exp2(s - m_new)
            alpha = jnp.exp2(m_prev - m_new)
            m_ref[c, slot] = m_new
            return p.astype(BF16), alpha

        def accumulate(u, p, alpha):
            bi, c = units[u]
            slot = blocks[bi][0]
            acc_ref[c, slot] = acc_ref[c, slot] * alpha + jnp.dot(
                vts[bi], p, preferred_element_type=F32)

        s_vals, p_vals = {}, {}
        n_steps = n_units + s_lag + 1
        for t in range(n_steps):
            if t < n_units:
                s_vals[t] = scores(t)
            if 0 <= t - s_lag < n_units:
                p_vals[t - s_lag] = softmax(t - s_lag, s_vals.pop(t - s_lag))
            if 0 <= t - s_lag - 1 < n_units:
                accumulate(t - s_lag - 1, *p_vals.pop(t - s_lag - 1))
            lo, hi = (t * len(side_work)) // n_steps, ((t + 1) * len(side_work)) // n_steps
            for piece in side_work[lo:hi]:
                piece()

    n_b = jnp.minimum(nq - 2 - pair, nq - 3)
    blocks = []
    for j in range(nq - 3):
        if j < nq // 2 - 1:
            blocks.append((1, j, None))
        else:
            in_b = j < n_b
            blocks.append((jnp.where(in_b, 1, 0), jnp.where(in_b, j, j - n_b), None))
    first = pair == 0
    blocks.append((jnp.where(first, 1, 0), jnp.where(first, nq - 3, pair - 1),
                   jnp.where(first, BIAS_ZERO, BIAS_PREV)))
    blocks.append((0, pair, BIAS_DIAG))
    blocks.append((1, nq - 2 - pair, BIAS_PREV))
    blocks.append((1, nq - 1 - pair, BIAS_DIAG))
    run_blocks(blocks)

    lam = _lambda_full(lam_ref, lam0)
    for slot in range(2):
        rows = pl.ds(pl.multiple_of(tiles[slot] * blk, blk), blk)
        a = [acc_ref[c, slot, 0:DIFF_DV, :] / acc_ref[c, slot, DIFF_DV:DIFF_DV + 1, :]
             for c in range(n_chunks)]
        for g in range(DIFF_GROUP):
            ot = a[g] - lam * a[DIFF_GROUP + g]
            ot = ot * lax.rsqrt(jnp.mean(ot * ot, axis=0, keepdims=True) + NORM_EPS)
            ot = ot * sg_ref[...] * (1.0 - lam0)
            gate = gate_ref[rows, g * DIFF_DV:(g + 1) * DIFF_DV].astype(F32)
            o_ref[rows, g * DIFF_DV:(g + 1) * DIFF_DV] = (ot.T * gate).astype(o_ref.dtype)


def _attn_decode_pieces(k_refs, v_refs, qt_ref, kn_ref, vn_ref, bias_ref, bnew_ref, gate_ref,
                        sg_ref, lam_ref, o_ref, s_ref, p_ref, *, rows_per_page, lam0):
    n_pages = len(k_refs)
    w = rows_per_page
    state = {}

    def score_page(p):
        def run():
            kp = k_refs[p][...].astype(BF16)
            sp = lax.dot_general(qt_ref[...], kp, (((1,), (1,)), ((), ())),
                                 preferred_element_type=F32)
            s_ref[:, p * w:(p + 1) * w] = sp + bias_ref[:, p * w:(p + 1) * w]
        return run

    def softmax():
        s = s_ref[...]
        s_new = jnp.sum(qt_ref[...].astype(F32) * kn_ref[...].astype(F32), axis=1, keepdims=True)
        s_new = s_new + bnew_ref[:, 0:1]
        m = jnp.maximum(jnp.max(s, axis=1, keepdims=True), s_new)
        e_new = jnp.exp(s_new - m)
        pr = jnp.exp(s - m)
        state["l"] = jnp.sum(pr, axis=1, keepdims=True) + e_new
        state["acc"] = e_new * vn_ref[...].astype(F32)
        p_ref[...] = pr.astype(BF16)

    def value_page(p):
        def run():
            vp = v_refs[p][...].astype(BF16)
            state["acc"] = state["acc"] + jnp.dot(p_ref[:, p * w:(p + 1) * w], vp,
                                                  preferred_element_type=F32)
        return run

    def finish():
        a = state["acc"] / state["l"]
        lam = _lambda_full(lam_ref, lam0)
        o = a[:DIFF_HEADS] - lam * a[DIFF_HEADS:]
        o = o * lax.rsqrt(jnp.mean(o * o, axis=-1, keepdims=True) + NORM_EPS)
        o = o * sg_ref[...] * (1.0 - lam0)
        o_ref[...] = o * gate_ref[...]

    return ([score_page(p) for p in range(n_pages)] + [softmax]
            + [value_page(p) for p in range(n_pages)] + [finish])


N_PROMPT_IN = 7
N_DECODE_IN = 7


def _attn_kernel(pt_ref, *refs, n_pages, rows_per_page, blk, nq, lam0):
    del pt_ref
    prompt_in = refs[:N_PROMPT_IN]
    k_refs = refs[N_PROMPT_IN:N_PROMPT_IN + n_pages]
    v_refs = refs[N_PROMPT_IN + n_pages:N_PROMPT_IN + 2 * n_pages]
    decode_in = refs[N_PROMPT_IN + 2 * n_pages:N_PROMPT_IN + 2 * n_pages + N_DECODE_IN]
    o_ref, od_ref, qt_s, m_s, acc_s, sd_s, pd_s = refs[N_PROMPT_IN + 2 * n_pages + N_DECODE_IN:]
    lam_ref = prompt_in[-1]
    pieces = _attn_decode_pieces(k_refs, v_refs, *decode_in, lam_ref, od_ref, sd_s, pd_s,
                                 rows_per_page=rows_per_page, lam0=lam0)
    _attn_prompt_body(*prompt_in, o_ref, qt_s, m_s, acc_s, blk=blk, nq=nq, lam0=lam0,
                      side_work=pieces)


def _attention(qt_p, kb_p, vt_p, bias_p, gates_p, b, s, blk,
               dq, kb_new, vb_new, gate, cache_k, cache_v, page_table, bias, bias_new,
               subln_g, lam_rows, lam0, layer):
    nq = s // blk
    assert nq % 2 == 0 and nq >= 4
    bs = dq.shape[0]
    assert bs == b * DIFF_KV_HEADS * (nq // 2), "one sample sequence per prompt grid step"
    t = b * s
    gcol0 = RET_V_W // (DIFF_GROUP * DIFF_DV)
    n_pool, depth, page = cache_k.shape[0], cache_k.shape[1], cache_k.shape[2]
    n_pages = page_table.shape[1]
    w = page * DIFF_KV_HEADS
    ck = cache_k.reshape(n_pool, depth, w, 2 * DIFF_DH)
    cv = cache_v.reshape(n_pool, depth, w, DIFF_DV)
    q = (dq * (DIFF_DH ** -0.5)).reshape(bs, DIFF_KV_HEADS, DIFF_GROUP, 2, DIFF_DH)
    q = jnp.transpose(q, (0, 3, 1, 2, 4)).reshape(bs, 2, DIFF_HEADS, DIFF_DH)
    z = jnp.zeros_like(q[:, 0])
    qt = jnp.concatenate([jnp.concatenate([q[:, 0], z], axis=-1),
                          jnp.concatenate([z, q[:, 1]], axis=-1)], axis=1)
    kn = jnp.tile(jnp.repeat(kb_new.reshape(bs, DIFF_KV_HEADS, 2 * DIFF_DH), DIFF_GROUP, axis=1), (1, 2, 1))
    vn = jnp.tile(jnp.repeat(vb_new.reshape(bs, DIFF_KV_HEADS, DIFF_DV), DIFF_GROUP, axis=1), (1, 2, 1))
    g3 = gate.astype(F32).reshape(bs, DIFF_HEADS, DIFF_DV)
    rows = 2 * DIFF_HEADS

    npair = nq // 2

    def seq(i, h, p):
        return (i * DIFF_KV_HEADS + h) * npair + p

    def page_spec(pg):
        return pl.BlockSpec((None, None, w, 2 * DIFF_DH),
                            lambda i, h, p, pt: (pt[seq(i, h, p), pg], layer, 0, 0))

    per_seq = lambda i, h, p, pt: (i, h)
    const2 = lambda i, h, p, pt: (0, 0)
    row_spec = pl.BlockSpec((None, rows, LANES), lambda i, h, p, pt: (seq(i, h, p), 0, 0))
    head_spec = pl.BlockSpec((None, DIFF_HEADS, DIFF_DV), lambda i, h, p, pt: (seq(i, h, p), 0, 0))
    in_specs = ([pl.BlockSpec((nq, DIFF_GROUP * DIFF_DV, blk), lambda i, h, p, pt: (i, h, 0)),
                 pl.BlockSpec((s, 2 * DIFF_DH), per_seq),
                 pl.BlockSpec((nq, VT_ROWS, blk), lambda i, h, p, pt: (i, h, 0)),
                 pl.BlockSpec((3, DIFF_GROUP, blk, blk), lambda i, h, p, pt: (0, h, 0, 0)),
                 pl.BlockSpec((s, DIFF_GROUP * DIFF_DV), lambda i, h, p, pt: (i, gcol0 + h)),
                 pl.BlockSpec((DIFF_DV, 1), const2),
                 pl.BlockSpec((4, DIFF_DH), const2)]
                + [page_spec(pg) for pg in range(n_pages)] + [page_spec(pg) for pg in range(n_pages)]
                + [row_spec, row_spec, row_spec,
                   pl.BlockSpec((rows, n_pages * w), const2),
                   pl.BlockSpec((rows, LANES), const2),
                   head_spec,
                   pl.BlockSpec((1, DIFF_DV), const2)])
    assert len(in_specs) == N_PROMPT_IN + 2 * n_pages + N_DECODE_IN
    do_p, do_s = pl.pallas_call(
        functools.partial(_attn_kernel, n_pages=n_pages, rows_per_page=w, blk=blk, nq=nq, lam0=lam0),
        grid_spec=pltpu.PrefetchScalarGridSpec(
            num_scalar_prefetch=1, grid=(b, DIFF_KV_HEADS, npair), in_specs=in_specs,
            out_specs=[pl.BlockSpec((s, DIFF_GROUP * DIFF_DV), per_seq), head_spec],
            scratch_shapes=[pltpu.VMEM((4, 2, DIFF_DV, blk), BF16),
                            pltpu.VMEM((4, 2, 1, blk), F32),
                            pltpu.VMEM((4, 2, VT_ROWS, blk), F32),
                            pltpu.VMEM((rows, n_pages * w), F32),
                            pltpu.VMEM((rows, n_pages * w), BF16)]),
        out_shape=[jax.ShapeDtypeStruct((t, DIFF_OUT_W), BF16),
                   jax.ShapeDtypeStruct((bs, DIFF_HEADS, DIFF_DV), F32)],
        compiler_params=_cparams(("arbitrary", "arbitrary", "arbitrary"), 56),
        name="attention",
    )(page_table, qt_p, kb_p, vt_p, bias_p, gates_p, subln_g.reshape(DIFF_DV, 1), lam_rows,
      *([ck] * n_pages), *([cv] * n_pages), qt, kn, vn, bias, bias_new, g3,
      subln_g.reshape(1, DIFF_DV))
    return do_p, do_s.reshape(bs, DIFF_OUT_W).astype(BF16)


def _softplus(x):
    return jnp.maximum(x, 0.0) + jnp.log1p(jnp.exp(-jnp.abs(x)))


def _lru_coeffs(xc, wg_ref, ba_ref, bx_ref, lam_ref, n):
    sl = slice(n * RNN_BLOCK_W, (n + 1) * RNN_BLOCK_W)
    xn = xc[:, sl]
    z = jnp.dot(xn.astype(BF16), wg_ref[n], preferred_element_type=F32)
    rg = _sigmoid(z[:, :RNN_BLOCK_W] + ba_ref[:, sl])
    ig = _sigmoid(z[:, RNN_BLOCK_W:] + bx_ref[:, sl])
    log_a = -LRU_C * rg * _softplus(-lam_ref[:, sl])
    a = jnp.exp(log_a)
    y = -jnp.tanh(log_a) * (1.0 + a * a)
    root = jnp.where(y > 0.0, y * lax.rsqrt(y), 0.0)
    bt = root * (ig * xn)
    return a, bt


def _lru_tile(ti, tm, x_ref, gate_ref, cw_ref, cb_ref, wg_ref, ba_ref, bx_ref, lam_ref,
              o_ref, conv_ref, xbuf_ref, a_ref, b_ref, h_ref):
    pad = SUBLANES
    rows = slice(ti * tm, (ti + 1) * tm)
    xbuf_ref[pad:pad + tm, :] = x_ref[rows, :]
    xc = cb_ref[...]
    for j in range(CONV_WIDTH):
        off = pad - (CONV_WIDTH - 1) + j
        xc = xc + xbuf_ref[off:off + tm, :] * cw_ref[j:j + 1, :]
    conv_ref[...] = xbuf_ref[pad + tm - (CONV_WIDTH - 1):pad + tm, :]
    xbuf_ref[0:pad, :] = xbuf_ref[tm:tm + pad, :]

    for n in range(RNN_BLOCKS):
        a, bt = _lru_coeffs(xc, wg_ref, ba_ref, bx_ref, lam_ref, n)
        sl = slice(n * RNN_BLOCK_W, (n + 1) * RNN_BLOCK_W)
        a = a.reshape(tm // SUBLANES, SUBLANES, RNN_BLOCK_W)
        bt = bt.reshape(tm // SUBLANES, SUBLANES, RNN_BLOCK_W)
        r8 = lax.broadcasted_iota(jnp.int32, a.shape, 1)
        for d in (1, 2, 4):
            a_sh = pltpu.roll(a, d, 1)
            b_sh = pltpu.roll(bt, d, 1)
            ok = r8 >= d
            bt = jnp.where(ok, a * b_sh + bt, bt)
            a = jnp.where(ok, a * a_sh, a)
        a_ref[:, sl] = a.reshape(tm, RNN_BLOCK_W)
        b_ref[:, sl] = bt.reshape(tm, RNN_BLOCK_W)

    h = h_ref[...]
    for gi in range(tm // SUBLANES):
        rs = slice(gi * SUBLANES, (gi + 1) * SUBLANES)
        hg = a_ref[rs, :] * h + b_ref[rs, :]
        b_ref[rs, :] = hg
        h = hg[SUBLANES - 1:SUBLANES, :]
    h_ref[...] = h
    o_ref[rows, :] = (b_ref[...] * gate_ref[rows, :].astype(F32)).astype(o_ref.dtype)


def _ret_lru_kernel(qk_ref, v_ref, gr_ref, dmask_ref, qdec_ref, kdec_ref,
                    x_ref, gl_ref, cw_ref, cb_ref, wg_ref, ba_ref, bx_ref, lam_ref,
                    o_ret_ref, sfin_ref, o_lru_ref, hlast_ref, conv_ref,
                    state_ref, xbuf_ref, a_ref, b_ref, h_ref, *, chunk_dec, tm):
    j = pl.program_id(1)

    @pl.when(j == 0)
    def _():
        state_ref[...] = jnp.zeros_like(state_ref)
        xbuf_ref[0:SUBLANES, :] = jnp.zeros((SUBLANES, RNN_WIDTH), F32)
        h_ref[...] = jnp.zeros(h_ref.shape, F32)

    rows = qk_ref.shape[0]
    n_chunks = rows // dmask_ref.shape[-1]
    n_tiles = rows // tm
    ret = [functools.partial(_retention_chunk, ci, qk_ref, v_ref, gr_ref, dmask_ref, qdec_ref,
                             kdec_ref, o_ret_ref, state_ref, chunk_dec) for ci in range(n_chunks)]
    lru = [functools.partial(_lru_tile, ti, tm, x_ref, gl_ref, cw_ref, cb_ref, wg_ref, ba_ref, bx_ref,
                             lam_ref, o_lru_ref, conv_ref, xbuf_ref, a_ref, b_ref, h_ref)
           for ti in range(n_tiles)]
    per_tile = n_chunks // n_tiles
    for ti in range(n_tiles):
        ret[ti * per_tile]()
        lru[ti]()
        for piece in ret[ti * per_tile + 1:(ti + 1) * per_tile]:
            piece()
    hlast_ref[...] = h_ref[...]

    @pl.when(j == pl.num_programs(1) - 1)
    def _():
        sfin_ref[...] = state_ref[...]


def _ret_lru_prompt(qk, rv, gates, lx, conv_w, conv_b, wg, ba, bx, lam, b, s, tm):
    c = math.gcd(s, RET_CHUNK)
    dmask, qdec, kdec, chunk_dec = _retention_decays(c)
    rows = c * math.gcd(s // c, RET_CHUNKS_PER_STEP)
    assert rows % tm == 0 and (rows // c) % (rows // tm) == 0
    nt = s // rows
    t = b * s
    lcol = 2
    row_blk = lambda w, col=0: pl.BlockSpec((rows, w), lambda i, j: (i * nt + j, col))
    const2 = lambda shape: pl.BlockSpec(shape, lambda i, j: (0, 0))
    vec = const2((1, RNN_WIDTH))
    return pl.pallas_call(
        functools.partial(_ret_lru_kernel, chunk_dec=chunk_dec, tm=tm),
        grid=(b, nt),
        in_specs=[row_blk(2 * RET_QK_W), row_blk(RET_V_W), row_blk(RET_V_W),
                  pl.BlockSpec((RET_HEADS, c, c), lambda i, j: (0, 0, 0)),
                  const2((c, RET_HEADS)), const2((c, RET_HEADS)),
                  row_blk(RNN_WIDTH), row_blk(RNN_WIDTH, lcol),
                  const2((CONV_WIDTH, RNN_WIDTH)), vec,
                  pl.BlockSpec((RNN_BLOCKS, RNN_BLOCK_W, 2 * RNN_BLOCK_W), lambda i, j: (0, 0, 0)),
                  vec, vec, vec],
        out_specs=[row_blk(RET_V_W),
                   pl.BlockSpec((None, RET_HEADS, RET_DK, RET_DV), lambda i, j: (i, 0, 0, 0)),
                   row_blk(RNN_WIDTH),
                   pl.BlockSpec((None, 1, RNN_WIDTH), lambda i, j: (i, 0, 0)),
                   pl.BlockSpec((None, CONV_WIDTH - 1, RNN_WIDTH), lambda i, j: (i, 0, 0))],
        out_shape=[jax.ShapeDtypeStruct((t, RET_V_W), BF16),
                   jax.ShapeDtypeStruct((b, RET_HEADS, RET_DK, RET_DV), F32),
                   jax.ShapeDtypeStruct((t, RNN_WIDTH), BF16),
                   jax.ShapeDtypeStruct((b, 1, RNN_WIDTH), F32),
                   jax.ShapeDtypeStruct((b, CONV_WIDTH - 1, RNN_WIDTH), F32)],
        scratch_shapes=[pltpu.VMEM((RET_HEADS, RET_DK, RET_DV), F32),
                        pltpu.VMEM((tm + SUBLANES, RNN_WIDTH), F32),
                        pltpu.VMEM((tm, RNN_WIDTH), F32),
                        pltpu.VMEM((tm, RNN_WIDTH), F32),
                        pltpu.VMEM((1, RNN_WIDTH), F32)],
        compiler_params=_cparams(("parallel", "arbitrary"), 40),
        name="ret_lru_prompt",
    )(qk, rv, gates, dmask, qdec, kdec, lx, gates, conv_w, conv_b.reshape(1, -1), wg,
      ba.reshape(1, -1), bx.reshape(1, -1), lam.reshape(1, -1))


def _lru_step_kernel(x_ref, c0_ref, c1_ref, c2_ref, h0_ref, gate_ref, cw_ref, cb_ref, wg_ref,
                     ba_ref, bx_ref, lam_ref, o_ref, h_ref):
    xc = (cb_ref[...] + c0_ref[...] * cw_ref[0:1, :] + c1_ref[...] * cw_ref[1:2, :]
          + c2_ref[...] * cw_ref[2:3, :] + x_ref[...] * cw_ref[3:4, :])
    for n in range(RNN_BLOCKS):
        a, bt = _lru_coeffs(xc, wg_ref, ba_ref, bx_ref, lam_ref, n)
        sl = slice(n * RNN_BLOCK_W, (n + 1) * RNN_BLOCK_W)
        h = a * h0_ref[:, sl] + bt
        h_ref[:, sl] = h
        o_ref[:, sl] = (h * gate_ref[:, sl].astype(F32)).astype(o_ref.dtype)


def _lru_step(lx, conv_prev, h0, gate, conv_w, conv_b, wg, ba, bx, lam):
    bs = lx.shape[0]
    vm = pl.BlockSpec(memory_space=pltpu.VMEM)
    return pl.pallas_call(
        _lru_step_kernel,
        in_specs=[vm] * 12,
        out_specs=[vm, vm],
        out_shape=[jax.ShapeDtypeStruct((bs, RNN_WIDTH), BF16),
                   jax.ShapeDtypeStruct((bs, RNN_WIDTH), F32)],
        compiler_params=pltpu.CompilerParams(vmem_limit_bytes=32 * MIB),
        name="lru_step",
    )(lx, conv_prev[:, 0], conv_prev[:, 1], conv_prev[:, 2], h0, gate, conv_w,
      conv_b.reshape(1, -1), wg, ba.reshape(1, -1), bx.reshape(1, -1), lam.reshape(1, -1))


def _merge_kernel(ro_ref, do_ref, lo_ref, mg_ref, x_ref, p_ref, wr_ref, wd_ref, wl_ref, wo_ref,
                  wpg_ref, wp_ref, gn_ref, x_out_ref, hn_ref):
    d = D_MODEL
    tm = x_ref.shape[0]
    chunk = min(MERGE_CHUNK, tm)
    chunks = [slice(c * chunk, (c + 1) * chunk) for c in range(tm // chunk)]
    dot = functools.partial(jnp.dot, preferred_element_type=F32)
    merged = [(mg_ref[rs, 0:d].astype(F32) * dot(ro_ref[rs, :], wr_ref[...])
               + mg_ref[rs, d:2 * d].astype(F32) * dot(do_ref[rs, :], wd_ref[...])
               + mg_ref[rs, 2 * d:3 * d].astype(F32) * dot(lo_ref[rs, :], wl_ref[...])).astype(BF16)
              for rs in chunks]
    ple = [dot(p_ref[rs, :].astype(BF16), wp_ref[...]) for rs in chunks]
    x1 = [x_ref[rs, :] + dot(m, wo_ref[...]) for rs, m in zip(chunks, merged)]
    gate = [dot(v.astype(BF16), wpg_ref[...]) for v in x1]
    for rs, v, g, e in zip(chunks, x1, gate, ple):
        x2 = v + _sigmoid(g) * e
        x_out_ref[rs, :] = x2
        y = x2 * lax.rsqrt(jnp.mean(x2 * x2, axis=-1, keepdims=True) + NORM_EPS)
        hn_ref[rs, :] = (y * gn_ref[...]).astype(hn_ref.dtype)


def _merge(ro, do, lo, mg, x, p_all, layer, wr, wd, wl, wo, wpg, wp, g_next, hn_dtype, tm):
    t = x.shape[0]
    d = D_MODEL
    row = lambda wdt: pl.BlockSpec((tm, wdt), lambda i: (i, 0))
    full = lambda a: pl.BlockSpec(a.shape, lambda i: (0, 0), pipeline_mode=pl.Buffered(1))
    return pl.pallas_call(
        _merge_kernel,
        grid=(t // tm,),
        in_specs=[row(RET_V_W), row(DIFF_OUT_W), row(RNN_WIDTH), row(3 * d), row(d),
                  pl.BlockSpec((None, tm, D_PLE), lambda i: (layer, i, 0)),
                  full(wr), full(wd), full(wl), full(wo), full(wpg), full(wp),
                  pl.BlockSpec((1, d), lambda i: (0, 0))],
        out_specs=[row(d), row(d)],
        out_shape=[jax.ShapeDtypeStruct((t, d), F32), jax.ShapeDtypeStruct((t, d), hn_dtype)],
        compiler_params=_cparams(("parallel",), 56),
        name="merge",
    )(ro, do, lo, mg, x, p_all, wr, wd, wl, wo, wpg, wp, g_next.reshape(1, d))


def _lambda_init(layer):
    return 0.8 - 0.6 * math.exp(-0.3 * layer)


def _rope_tables(pos):
    half = RET_DK // 2
    inv = ROPE_BASE ** (-jnp.arange(half, dtype=F32) / half)
    ang = pos.astype(F32)[:, None] * inv[None, :]
    cos = jnp.cos(ang)
    sin = jnp.sin(ang)
    return jnp.concatenate([cos, cos], axis=-1), jnp.concatenate([-sin, sin], axis=-1)


def kernel(x_prompt, x_sample, cache_k, cache_v, state_ret, state_lru, state_conv, page_table, p_prompt, p_sample, rel_bias, norm_g, w_in, lambda_q1, lambda_k1, lambda_q2, lambda_k2, subln_g, conv_w, conv_b, gate_a_w, gate_a_b, gate_x_w, gate_x_b, lru_lambda, w_ret_out, w_diff_out, w_lru_out, w_o, w_ple, w_ple_gate, final_norm_g):
    bp, sp, d = x_prompt.shape
    bs, ss, _ = x_sample.shape
    assert d == D_MODEL and ss == 1
    depth = w_in.shape[0]
    n_pages, page = page_table.shape[1], cache_k.shape[2]
    past = n_pages * page
    tp = bp * sp
    blk = min(ATTN_BLOCK, sp)
    tm_p = min(PROJ_ROWS, sp)
    tm_qkv = min(QKV_ROWS, sp)
    tm_lru = min(LRU_ROWS, sp)
    tm_merge = min(MERGE_ROWS, sp)
    p_prompt_all = p_prompt.reshape(depth, tp, D_PLE)
    p_sample_all = p_sample.reshape(depth, bs, D_PLE)

    cos_p, sin_p = _rope_tables(jnp.arange(sp, dtype=jnp.int32))
    cos_s, sin_s = _rope_tables(jnp.full((bs,), past, dtype=jnp.int32))
    bias_p = _bias_prompt(rel_bias, blk)
    bias_s, bias_s_new = _bias_sample(rel_bias, past)

    w_in_b = w_in.astype(BF16)
    wg_all = jnp.concatenate([gate_a_w, gate_x_w], axis=-1).astype(BF16)
    xp = x_prompt.reshape(tp, d)
    xs = x_sample.reshape(bs, d)
    hn_p = _rmsnorm(xp, norm_g[0], BF16, tm_p)
    hn_s = _rmsnorm(xs, norm_g[0], BF16, bs)

    outs = {k: [] for k in ("ks", "vs", "rp", "lp", "ls", "cp", "cs")}
    kp_all = jnp.zeros((bp, depth, sp * DIFF_KV_HEADS, 2 * DIFF_DH), F32)
    vp_all = jnp.zeros((bp, depth, sp * DIFF_KV_HEADS, DIFF_DV), F32)
    rs_all = jnp.zeros(state_ret.shape, F32)
    for l in range(depth):
        wg = wg_all[l]
        wr, wd, wl = w_ret_out[l].astype(BF16), w_diff_out[l].astype(BF16), w_lru_out[l].astype(BF16)
        wo, wpg, wp = w_o[l].astype(BF16), w_ple_gate[l].astype(BF16), w_ple[l].astype(BF16)
        lam0 = _lambda_init(l)
        lam_rows = jnp.stack([lambda_q1[l], lambda_k1[l], lambda_q2[l], lambda_k2[l]], axis=0)
        last = l == depth - 1
        g_next = final_norm_g if last else norm_g[l + 1]
        hn_dtype = F32 if last else BF16

        qk = _proj(hn_p, w_in_b, l, COL_ROT, "rotary", tm_p, tables=(cos_p, sin_p))
        rv = _proj(hn_p, w_in_b, l, COL_RV, "plain", tm_p)
        gates = _proj(hn_p, w_in_b, l, COL_SILU, "silu", tm_p)
        kp_all, vp_all, kb, qt, vt = _proj_qkv_t(hn_p, w_in_b, l, tm_qkv, blk, kp_all, vp_all)
        lx = _proj(hn_p, w_in_b, l, COL_LX, "plain", tm_p, out_dtype=F32)
        mg = _proj(hn_p, w_in_b, l, COL_MG, "sigmoid", tm_p)
        qk_s, rvdq_s, gates_s, kf_s, vf_s, kb_s, vb_s, lx_s, mg_s = _proj_sample(
            hn_s, w_in_b, l, cos_s, sin_s)

        do, do_s = _attention(qt, kb, vt, bias_p, gates, bp, sp, blk,
                              rvdq_s[:, RET_V_W:], kb_s, vb_s, gates_s[:, RET_V_W:RET_V_W + DIFF_OUT_W],
                              cache_k, cache_v, page_table, bias_s, bias_s_new, subln_g[l],
                              lam_rows, lam0, l)

        ro, s_fin, lo, h_last, conv_new = _ret_lru_prompt(
            qk, rv, gates, lx, conv_w[l], conv_b[l], wg, gate_a_b[l], gate_x_b[l], lru_lambda[l],
            bp, sp, tm_lru)
        xp, hn_p = _merge(ro, do, lo, mg, xp, p_prompt_all, l, wr, wd, wl, wo, wpg, wp,
                          g_next, hn_dtype, tm_merge)
        outs["rp"].append(s_fin)
        outs["lp"].append(h_last.reshape(bp, RNN_WIDTH))
        outs["cp"].append(conv_new)

        lx = lx_s
        ro, rs_all = _retention_step(qk_s, rvdq_s, gates_s, state_ret, rs_all, l)
        conv_prev = state_conv[:, l]
        lo, h_new = _lru_step(lx, conv_prev, state_lru[:, l], gates_s[:, 2 * RET_V_W:], conv_w[l],
                              conv_b[l], wg, gate_a_b[l], gate_x_b[l], lru_lambda[l])
        xs, hn_s = _merge(ro, do_s, lo, mg_s, xs, p_sample_all, l, wr, wd, wl, wo, wpg, wp,
                          g_next, hn_dtype, bs)
        outs["ks"].append(kf_s.reshape(bs, 1, DIFF_KV_HEADS, 2 * DIFF_DH))
        outs["vs"].append(vf_s.reshape(bs, 1, DIFF_KV_HEADS, DIFF_DV))
        outs["ls"].append(h_new)
        outs["cs"].append(jnp.concatenate([conv_prev[:, 1:], lx[:, None, :]], axis=1))

    st = lambda key: jnp.stack(outs[key], axis=1)
    return (hn_p.reshape(bp, sp, d), hn_s.reshape(bs, 1, d),
            kp_all.reshape(bp, depth, sp, DIFF_KV_HEADS, 2 * DIFF_DH),
            vp_all.reshape(bp, depth, sp, DIFF_KV_HEADS, DIFF_DV), st("ks"), st("vs"),
            st("rp"), rs_all, st("lp"), st("ls"), st("cp"), st("cs"))
```

```python
import functools
import math

import numpy as np
import jax
import jax.numpy as jnp
from jax import lax
from jax.experimental import pallas as pl
from jax.experimental.pallas import tpu as pltpu

F32 = jnp.float32
BF16 = jnp.bfloat16

D_MODEL = 1024
D_PLE = 256
NORM_EPS = 1e-6
RET_HEADS = 4
RET_DK = 128
RET_DV = 256
RET_CHUNK = 128
RET_CHUNKS_PER_STEP = 4
ROPE_BASE = 10000.0
DIFF_HEADS = 8
DIFF_KV_HEADS = 4
DIFF_GROUP = DIFF_HEADS // DIFF_KV_HEADS
DIFF_DH = 64
DIFF_DV = 2 * DIFF_DH
REL_BUCKETS = 32
REL_MAX_DIST = 128
RNN_WIDTH = 1024
RNN_BLOCKS = 8
RNN_BLOCK_W = RNN_WIDTH // RNN_BLOCKS
CONV_WIDTH = 4
LRU_C = 8.0

RET_QK_W = RET_HEADS * RET_DK
RET_V_W = RET_HEADS * RET_DV
DIFF_Q_W = DIFF_HEADS * 2 * DIFF_DH
DIFF_K_W = DIFF_KV_HEADS * 2 * DIFF_DH
DIFF_V_W = DIFF_KV_HEADS * DIFF_DV
DIFF_OUT_W = DIFF_HEADS * DIFF_DV
IN_SPLITS = (RET_QK_W, RET_QK_W, RET_V_W, RET_V_W,
             DIFF_Q_W, DIFF_K_W, DIFF_V_W, DIFF_OUT_W,
             RNN_WIDTH, RNN_WIDTH, 3 * D_MODEL)
IN_OFFS = tuple(int(v) for v in np.cumsum((0,) + IN_SPLITS))

LANES = 128
SUBLANES = 8
MASK_VALUE = -1e30
LOG2E = math.log2(math.e)
VT_ONES = 2 * SUBLANES
VT_ROWS = DIFF_DV + VT_ONES
BIAS_DIAG, BIAS_PREV, BIAS_ZERO = 0, 1, 2
MERGE_CHUNK = 256
ATTN_BLOCK = 256
PROJ_ROWS = 1024
QKV_ROWS = 512
LRU_ROWS = 256
MERGE_ROWS = 512
MIB = 1024 * 1024


def _cparams(sem, vmem_mib):
    return pltpu.CompilerParams(dimension_semantics=sem, vmem_limit_bytes=int(vmem_mib * MIB))


def _rmsnorm_kernel(x_ref, g_ref, o_ref):
    x = x_ref[...]
    y = x * lax.rsqrt(jnp.mean(x * x, axis=-1, keepdims=True) + NORM_EPS)
    o_ref[...] = (y * g_ref[...]).astype(o_ref.dtype)


def _rmsnorm(x, g, out_dtype, tm):
    t, d = x.shape
    return pl.pallas_call(
        _rmsnorm_kernel,
        grid=(t // tm,),
        in_specs=[pl.BlockSpec((tm, d), lambda i: (i, 0)),
                  pl.BlockSpec((1, d), lambda i: (0, 0))],
        out_specs=pl.BlockSpec((tm, d), lambda i: (i, 0)),
        out_shape=jax.ShapeDtypeStruct((t, d), out_dtype),
        compiler_params=_cparams(("parallel",), 32),
        name="rmsnorm",
    )(x, g.reshape(1, d))


def _proj_qkv_t_kernel(x_ref, wq_ref, wkv_ref, k_all_ref, v_all_ref, kf_ref, vf_ref, kb_ref,
                       qt_ref, vt_ref):
    del k_all_ref, v_all_ref
    blk = qt_ref.shape[-1]

    def matmuls(j):
        x = x_ref[j * blk:(j + 1) * blk, :]
        return (jnp.dot(x, wq_ref[...], preferred_element_type=F32),
                jnp.dot(x, wkv_ref[...], preferred_element_type=F32))

    def epilogue(j, q, kv):
        rows = slice(j * blk, (j + 1) * blk)
        k = kv[:, :DIFF_K_W]
        v = kv[:, DIFF_K_W:]
        for h in range(DIFF_KV_HEADS):
            dst = pl.ds(j * blk * DIFF_KV_HEADS + h, blk, stride=DIFF_KV_HEADS)
            kf_ref[dst, :] = k[:, h * 2 * DIFF_DH:(h + 1) * 2 * DIFF_DH]
            vf_ref[dst, :] = v[:, h * DIFF_DV:(h + 1) * DIFF_DV]
        kb_ref[rows, :] = k.astype(BF16)
        qt_ref[j] = (q * (DIFF_DH ** -0.5 * LOG2E)).T.astype(BF16)
        vt = v.T.astype(BF16)
        for h in range(DIFF_KV_HEADS):
            vt_ref[j, h * VT_ROWS:h * VT_ROWS + DIFF_DV, :] = vt[h * DIFF_DV:(h + 1) * DIFF_DV]
            vt_ref[j, h * VT_ROWS + DIFF_DV:(h + 1) * VT_ROWS, :] = jnp.ones((VT_ONES, blk), BF16)

    _pipeline_rows(qt_ref.shape[0], matmuls, epilogue)


def _pipeline_rows(n_chunks, matmuls, epilogue):
    pending = {}
    for c in range(n_chunks + 1):
        if c < n_chunks:
            pending[c] = matmuls(c)
        if c >= 1:
            epilogue(c - 1, *pending.pop(c - 1))


def _sigmoid(x):
    return 0.5 * jnp.tanh(0.5 * x) + 0.5


def _proj_epilogue(kind, acc, rows, refs):
    if kind == "plain":
        (o_ref,) = refs
        o_ref[rows, :] = acc.astype(o_ref.dtype)
    elif kind == "silu":
        (o_ref,) = refs
        o_ref[rows, :] = (acc * _sigmoid(acc)).astype(o_ref.dtype)
    elif kind == "sigmoid":
        (o_ref,) = refs
        o_ref[rows, :] = _sigmoid(acc).astype(o_ref.dtype)
    elif kind == "rotary":
        cos_ref, sin_ref, o_ref = refs
        cos = cos_ref[rows, :]
        sin = sin_ref[rows, :]
        for h in range(2 * RET_HEADS):
            z = acc[:, h * RET_DK:(h + 1) * RET_DK]
            r = z * cos + pltpu.roll(z, RET_DK // 2, 1) * sin
            if h >= RET_HEADS:
                r = r * (RET_DK ** -0.5)
            o_ref[rows, h * RET_DK:(h + 1) * RET_DK] = r.astype(o_ref.dtype)
    elif kind == "kv":
        kf_ref, vf_ref, kb_ref, vb_ref = refs
        k = acc[:, :DIFF_K_W]
        v = acc[:, DIFF_K_W:]
        kf_ref[rows, :] = k
        vf_ref[rows, :] = v
        kb_ref[rows, :] = k.astype(BF16)
        vb_ref[rows, :] = v.astype(BF16)
    else:
        raise ValueError(kind)


def _proj_kernel(x_ref, w_ref, *refs, kind, chunk):
    tm = x_ref.shape[0]

    def matmuls(c):
        return (jnp.dot(x_ref[c * chunk:(c + 1) * chunk, :], w_ref[...], preferred_element_type=F32),)

    def epilogue(c, acc):
        _proj_epilogue(kind, acc, slice(c * chunk, (c + 1) * chunk), refs)

    _pipeline_rows(tm // chunk, matmuls, epilogue)


SAMPLE_BLOCKS = (("rotary", "qk"), ("plain", "rvdq"), ("silu", "gates"), ("plain", "rvdq"),
                 ("kv", "kv"), ("silu", "gates"), ("plain", "lx"), ("silu", "gates"),
                 ("sigmoid", "mg"), ("sigmoid", "mg"), ("sigmoid", "mg"))


def _proj_sample_kernel(x_ref, w_ref, cos_ref, sin_ref, qk_ref, rvdq_ref, gates_ref, kf_ref, vf_ref,
                        kb_ref, vb_ref, lx_ref, mg_ref):
    j = pl.program_id(0)
    acc = jnp.dot(x_ref[...], w_ref[...], preferred_element_type=F32)
    outs = {"qk": (cos_ref, sin_ref, qk_ref), "rvdq": (rvdq_ref,), "gates": (gates_ref,),
            "kv": (kf_ref, vf_ref, kb_ref, vb_ref), "lx": (lx_ref,), "mg": (mg_ref,)}
    for col, (kind, name) in enumerate(SAMPLE_BLOCKS):
        pl.when(j == col)(functools.partial(_proj_epilogue, kind, acc, slice(None), outs[name]))


def _proj_sample(hn, w_all, layer, cos, sin):
    bs, d = hn.shape
    tn = PROJ_TN
    assert w_all.shape[2] == tn * len(SAMPLE_BLOCKS)

    def block_of(name):
        cols = [c for c, (_, n) in enumerate(SAMPLE_BLOCKS) if n == name]
        return lambda j: (0, sum(jnp.where(j >= c, 1, 0) for c in cols[1:]))

    def out(name, width, dtype):
        n_blocks = sum(1 for _, n in SAMPLE_BLOCKS if n == name)
        return (pl.BlockSpec((bs, width), block_of(name)),
                jax.ShapeDtypeStruct((bs, width * n_blocks), dtype))

    specs = [out("qk", tn, BF16), out("rvdq", tn, BF16), out("gates", tn, BF16),
             out("kv", DIFF_K_W, F32), out("kv", DIFF_V_W, F32), out("kv", DIFF_K_W, BF16),
             out("kv", DIFF_V_W, BF16), out("lx", tn, F32), out("mg", tn, BF16)]
    fixed = lambda j: (0, 0)
    return pl.pallas_call(
        _proj_sample_kernel,
        grid=(len(SAMPLE_BLOCKS),),
        in_specs=[pl.BlockSpec((bs, d), fixed),
                  pl.BlockSpec((None, d, tn), lambda j: (layer, 0, j)),
                  pl.BlockSpec((bs, RET_DK), fixed), pl.BlockSpec((bs, RET_DK), fixed)],
        out_specs=[s for s, _ in specs],
        out_shape=[o for _, o in specs],
        compiler_params=_cparams(("arbitrary",), 32),
        name="proj_sample",
    )(hn, w_all, cos, sin)


PROJ_TN = 1024
PROJ_CHUNK = 256
COL_ROT = (0,)
COL_RV = (1,)
COL_SILU = (2, 5, 7)
COL_DQ = 3
COL_KV = 4
COL_LX = (6,)
COL_MG = (8, 9, 10)
assert IN_OFFS[2] == PROJ_TN and IN_OFFS[5] == COL_KV * PROJ_TN and IN_OFFS[10] == COL_MG[0] * PROJ_TN


def _col_index(cols, j):
    if len(cols) == 1:
        return cols[0]
    step = cols[1] - cols[0]
    idx = cols[0] + step * j
    if cols[-1] != cols[0] + step * (len(cols) - 1):
        idx = idx - j // (len(cols) - 1)
    return idx


def _proj_qkv_t(hn, w_all, layer, tm, blk, k_all, v_all):
    t, d = hn.shape
    nj = tm // blk
    per_seq = k_all.shape[2] // (tm * DIFF_KV_HEADS)
    half = pl.BlockSpec((tm, DIFF_K_W), lambda i: (i, 0))
    per_head = pl.BlockSpec((None, None, tm * DIFF_KV_HEADS, DIFF_DV),
                            lambda i: (i // per_seq, layer, i % per_seq, 0))
    untouched = pl.BlockSpec(memory_space=pl.ANY)
    return pl.pallas_call(
        _proj_qkv_t_kernel,
        grid=(t // tm,),
        in_specs=[pl.BlockSpec((tm, d), lambda i: (i, 0)),
                  pl.BlockSpec((None, d, PROJ_TN), lambda i: (layer, 0, COL_DQ)),
                  pl.BlockSpec((None, d, PROJ_TN), lambda i: (layer, 0, COL_KV)),
                  untouched, untouched],
        out_specs=[per_head, per_head, half,
                   pl.BlockSpec((nj, DIFF_Q_W, blk), lambda i: (i, 0, 0)),
                   pl.BlockSpec((nj, DIFF_KV_HEADS * VT_ROWS, blk), lambda i: (i, 0, 0))],
        out_shape=[jax.ShapeDtypeStruct(k_all.shape, F32),
                   jax.ShapeDtypeStruct(v_all.shape, F32),
                   jax.ShapeDtypeStruct((t, DIFF_K_W), BF16),
                   jax.ShapeDtypeStruct((t // blk, DIFF_Q_W, blk), BF16),
                   jax.ShapeDtypeStruct((t // blk, DIFF_KV_HEADS * VT_ROWS, blk), BF16)],
        input_output_aliases={3: 0, 4: 1},
        compiler_params=_cparams(("parallel",), 48),
        name="proj_qkv_t",
    )(hn, w_all, w_all, k_all, v_all)


def _proj(hn, w_all, layer, cols, kind, tm, out_dtype=BF16, tables=None):
    t, d = hn.shape
    tn = PROJ_TN
    n = tn * len(cols)
    grid = (len(cols), t // tm)
    in_specs = [pl.BlockSpec((tm, d), lambda j, i: (i, 0)),
                pl.BlockSpec((None, d, tn), lambda j, i: (layer, 0, _col_index(cols, j)))]
    args = [hn, w_all]
    if kind == "rotary":
        cos, sin = tables
        nb = cos.shape[0] // tm
        spec = pl.BlockSpec((tm, RET_DK), lambda j, i: (i % nb, 0))
        in_specs += [spec, spec]
        args += [cos, sin]
    out_specs = pl.BlockSpec((tm, tn), lambda j, i: (i, j))
    out_shape = jax.ShapeDtypeStruct((t, n), out_dtype)
    return pl.pallas_call(
        functools.partial(_proj_kernel, kind=kind, chunk=min(PROJ_CHUNK, tm)),
        grid=grid, in_specs=in_specs, out_specs=out_specs, out_shape=out_shape,
        compiler_params=_cparams(("parallel", "parallel"), 40),
        name="proj_" + kind,
    )(*args)


def _retention_chunk(ci, qk_ref, v_ref, gate_ref, dmask_ref, qdec_ref, kdec_ref, o_ref, state_ref,
                     chunk_dec):
    heads = range(RET_HEADS)
    dot = functools.partial(jnp.dot, preferred_element_type=F32)
    cw = dmask_ref.shape[-1]
    rows = slice(ci * cw, (ci + 1) * cw)
    q = [qk_ref[rows, h * RET_DK:(h + 1) * RET_DK] for h in heads]
    k = [qk_ref[rows, RET_QK_W + h * RET_DK:RET_QK_W + (h + 1) * RET_DK] for h in heads]
    v = [v_ref[rows, h * RET_DV:(h + 1) * RET_DV] for h in heads]
    s = [lax.dot_general(q[h], k[h], (((1,), (1,)), ((), ())), preferred_element_type=F32)
         for h in heads]
    cross = [dot(q[h], state_ref[h].astype(BF16)) for h in heads]
    kd = [(k[h].astype(F32) * kdec_ref[:, h:h + 1]).astype(BF16) for h in heads]
    kv = [lax.dot_general(kd[h], v[h], (((0,), (0,)), ((), ())), preferred_element_type=F32)
          for h in heads]
    inner = [dot((s[h] * dmask_ref[h]).astype(BF16), v[h]) for h in heads]
    for h in heads:
        state_ref[h] = state_ref[h] * chunk_dec[h] + kv[h]
        o = inner[h] + cross[h] * qdec_ref[:, h:h + 1]
        o = o * lax.rsqrt(jnp.mean(o * o, axis=-1, keepdims=True) + NORM_EPS)
        g = gate_ref[rows, h * RET_DV:(h + 1) * RET_DV].astype(F32)
        o_ref[rows, h * RET_DV:(h + 1) * RET_DV] = (o * g).astype(o_ref.dtype)


def _retention_decays(c):
    log_g = np.log1p(-(2.0 ** (-5.0 - np.arange(RET_HEADS, dtype=np.float64))))
    idx = np.arange(c, dtype=np.float64)
    dist = idx[:, None] - idx[None, :]
    dmask = np.where(dist[None] >= 0, np.exp(np.maximum(dist, 0.0)[None] * log_g[:, None, None]), 0.0)
    qdec = np.exp((idx + 1.0)[:, None] * log_g[None, :])
    kdec = np.exp((c - 1.0 - idx)[:, None] * log_g[None, :])
    chunk_dec = tuple(float(x) for x in np.exp(c * log_g).astype(np.float32))
    return jnp.asarray(dmask, F32), jnp.asarray(qdec, F32), jnp.asarray(kdec, F32), chunk_dec


def _retention_step_kernel(qk_ref, v_ref, gate_ref, s_ref, s_all_ref, o_ref, snew_ref, *, decay):
    del s_all_ref
    sb = qk_ref.shape[0]
    rowid = lax.broadcasted_iota(jnp.int32, (sb, 1), 0)

    def per_sequence(b, outs):
        new_outs = []
        for h in range(RET_HEADS):
            q = qk_ref[:, h * RET_DK:(h + 1) * RET_DK]
            k = qk_ref[:, RET_QK_W + h * RET_DK:RET_QK_W + (h + 1) * RET_DK]
            v = v_ref[:, h * RET_DV:(h + 1) * RET_DV]
            st = s_ref[b, h]
            qs = jnp.dot(q, st.astype(BF16), preferred_element_type=F32)
            new_outs.append(outs[h] + jnp.where(rowid == b, decay[h] * qs, 0.0))
            kb = jnp.where(rowid == b, k, jnp.zeros_like(k))
            snew_ref[b, h] = st * decay[h] + lax.dot_general(
                kb, v, (((0,), (0,)), ((), ())), preferred_element_type=F32)
        return tuple(new_outs)

    outs = []
    for h in range(RET_HEADS):
        q = qk_ref[:, h * RET_DK:(h + 1) * RET_DK].astype(F32)
        k = qk_ref[:, RET_QK_W + h * RET_DK:RET_QK_W + (h + 1) * RET_DK].astype(F32)
        v = v_ref[:, h * RET_DV:(h + 1) * RET_DV].astype(F32)
        outs.append(jnp.sum(q * k, axis=1, keepdims=True) * v)
    outs = lax.fori_loop(0, sb, per_sequence, tuple(outs))
    for h in range(RET_HEADS):
        o = outs[h]
        o = o * lax.rsqrt(jnp.mean(o * o, axis=-1, keepdims=True) + NORM_EPS)
        g = gate_ref[:, h * RET_DV:(h + 1) * RET_DV].astype(F32)
        o_ref[:, h * RET_DV:(h + 1) * RET_DV] = (o * g).astype(o_ref.dtype)


def _retention_step(qk, rvdq, gates, state_ret, state_new, layer):
    bs = qk.shape[0]
    sb = min(2 * SUBLANES, bs)
    log_g = np.log1p(-(2.0 ** (-5.0 - np.arange(RET_HEADS, dtype=np.float64))))
    decay = tuple(float(x) for x in np.exp(log_g).astype(np.float32))
    layer_state = pl.BlockSpec((sb, None, RET_HEADS, RET_DK, RET_DV), lambda i: (i, layer, 0, 0, 0))
    return pl.pallas_call(
        functools.partial(_retention_step_kernel, decay=decay),
        grid=(bs // sb,),
        in_specs=[pl.BlockSpec((sb, 2 * RET_QK_W), lambda i: (i, 0)),
                  pl.BlockSpec((sb, RET_V_W), lambda i: (i, 0)),
                  pl.BlockSpec((sb, RET_V_W), lambda i: (i, 0)),
                  layer_state,
                  pl.BlockSpec(memory_space=pl.ANY)],
        out_specs=[pl.BlockSpec((sb, RET_V_W), lambda i: (i, 0)), layer_state],
        out_shape=[jax.ShapeDtypeStruct((bs, RET_V_W), BF16),
                   jax.ShapeDtypeStruct(state_new.shape, F32)],
        input_output_aliases={4: 1},
        compiler_params=_cparams(("parallel",), 48),
        name="retention_step",
    )(qk, rvdq, gates, state_ret, state_new)


def _t5_bucket(rel):
    n = jnp.maximum(rel, 0)
    max_exact = REL_BUCKETS // 2
    nf = jnp.maximum(n.astype(F32), 1.0)
    large = max_exact + (jnp.log(nf / max_exact) / math.log(REL_MAX_DIST / max_exact)
                         * (REL_BUCKETS - max_exact)).astype(jnp.int32)
    large = jnp.minimum(large, REL_BUCKETS - 1)
    return jnp.where(n < max_exact, n, large)


def _bias_lookup(tab_ref, bucket, head):
    out = jnp.zeros(bucket.shape, F32)
    far = tab_ref[REL_BUCKETS - 1, head]
    for bkt in range(REL_BUCKETS - 1):
        out = jnp.where(bucket == bkt, tab_ref[bkt, head] - far, out)
    return out


def _bias_prompt_kernel(tab_ref, o_ref, *, blk):
    i = lax.broadcasted_iota(jnp.int32, (blk, blk), 1)
    j = lax.broadcasted_iota(jnp.int32, (blk, blk), 0)
    for which in (BIAS_DIAG, BIAS_PREV):
        rel = i - j + which * blk
        bucket = _t5_bucket(rel)
        for h in range(DIFF_HEADS):
            o_ref[which, h] = jnp.where(rel >= 0, _bias_lookup(tab_ref, bucket, h) * LOG2E, MASK_VALUE)
    o_ref[BIAS_ZERO] = jnp.zeros(o_ref.shape[1:], F32)


def _bias_prompt(rel_bias, blk):
    return pl.pallas_call(
        functools.partial(_bias_prompt_kernel, blk=blk),
        in_specs=[pl.BlockSpec(memory_space=pltpu.SMEM)],
        out_specs=pl.BlockSpec(memory_space=pltpu.VMEM),
        out_shape=jax.ShapeDtypeStruct((3, DIFF_HEADS, blk, blk), F32),
        compiler_params=pltpu.CompilerParams(vmem_limit_bytes=32 * MIB),
        name="bias_prompt",
    )(rel_bias)


def _bias_sample_kernel(tab_ref, o_ref, onew_ref, *, past, width):
    rows = 2 * DIFF_HEADS
    r = lax.broadcasted_iota(jnp.int32, (rows, width), 0)
    lane = lax.broadcasted_iota(jnp.int32, (rows, width), 1)
    head = r % DIFF_HEADS
    kvh = lane % DIFF_KV_HEADS
    pos = lane // DIFF_KV_HEADS
    bucket = _t5_bucket(past - pos)
    out = jnp.zeros((rows, width), F32)
    for h in range(DIFF_HEADS):
        out = jnp.where(head == h, _bias_lookup(tab_ref, bucket, h), out)
    o_ref[...] = jnp.where(kvh == head // DIFF_GROUP, out, MASK_VALUE)
    rn = lax.broadcasted_iota(jnp.int32, (rows, LANES), 0) % DIFF_HEADS
    new = jnp.zeros((rows, LANES), F32)
    zero_bucket = _t5_bucket(jnp.zeros((rows, LANES), jnp.int32))
    for h in range(DIFF_HEADS):
        new = jnp.where(rn == h, _bias_lookup(tab_ref, zero_bucket, h), new)
    onew_ref[...] = new


def _bias_sample(rel_bias, past):
    width = past * DIFF_KV_HEADS
    return pl.pallas_call(
        functools.partial(_bias_sample_kernel, past=past, width=width),
        in_specs=[pl.BlockSpec(memory_space=pltpu.SMEM)],
        out_specs=[pl.BlockSpec(memory_space=pltpu.VMEM), pl.BlockSpec(memory_space=pltpu.VMEM)],
        out_shape=[jax.ShapeDtypeStruct((2 * DIFF_HEADS, width), F32),
                   jax.ShapeDtypeStruct((2 * DIFF_HEADS, LANES), F32)],
        compiler_params=pltpu.CompilerParams(vmem_limit_bytes=32 * MIB),
        name="bias_sample",
    )(rel_bias)


def _lambda_full(lam_ref, lam0):
    a = jnp.sum(lam_ref[0:1, :] * lam_ref[1:2, :], axis=-1, keepdims=True)
    b = jnp.sum(lam_ref[2:3, :] * lam_ref[3:4, :], axis=-1, keepdims=True)
    return jnp.exp(a) - jnp.exp(b) + lam0


def _attn_prompt_body(qt_in_ref, k_ref, vt_ref, bias_ref, gate_ref, sg_ref, lam_ref, o_ref,
                      qt_ref, m_ref, acc_ref, *, blk, nq, lam0, side_work=()):
    pair = pl.program_id(2)
    tiles = (pair, nq - 1 - pair)
    n_chunks = 2 * DIFF_GROUP
    row = lax.broadcasted_iota(jnp.int32, (DIFF_DV, blk), 0)
    for slot in range(2):
        for g in range(DIFF_GROUP):
            qg = qt_in_ref[tiles[slot], g * DIFF_DV:(g + 1) * DIFF_DV, :]
            for m in range(2):
                keep = (row < DIFF_DH) if m == 0 else (row >= DIFF_DH)
                c = m * DIFF_GROUP + g
                qt_ref[c, slot] = jnp.where(keep, qg, jnp.zeros_like(qg))
    m_ref[...] = jnp.full(m_ref.shape, MASK_VALUE, F32)
    acc_ref[...] = jnp.zeros(acc_ref.shape, F32)

    def run_blocks(blocks):
        s_lag = 3
        kbs = [k_ref[pl.ds(pl.multiple_of(ki * blk, blk), blk), :] for _, ki, _ in blocks]
        vts = [vt_ref[ki] for _, ki, _ in blocks]
        units = [(bi, c) for bi in range(len(blocks)) for c in range(n_chunks)]
        n_units = len(units)

        def scores(u):
            bi, c = units[u]
            slot, _, bias = blocks[bi]
            s = jnp.dot(kbs[bi], qt_ref[c, slot], preferred_element_type=F32)
            if bias is not None:
                s = s + bias_ref[bias, c % DIFF_GROUP]
            return s

        def softmax(u, s):
            bi, c = units[u]
            slot = blocks[bi][0]
            m_prev = m_ref[c, slot]
            m_new = jnp.maximum(m_prev, jnp.max(s, axis=0, keepdims=True))
            p = jnp.exp2(s - m_new)
            alpha = jnp.exp2(m_prev - m_new)
            m_ref[c, slot] = m_new
            return p.astype(BF16), alpha

        def accumulate(u, p, alpha):
            bi, c = units[u]
            slot = blocks[bi][0]
            acc_ref[c, slot] = acc_ref[c, slot] * alpha + jnp.dot(
                vts[bi], p, preferred_element_type=F32)

        s_vals, p_vals = {}, {}
        n_steps = n_units + s_lag + 1
        for t in range(n_steps):
            if t < n_units:
                s_vals[t] = scores(t)
            if 0 <= t - s_lag < n_units:
                p_vals[t - s_lag] = softmax(t - s_lag, s_vals.pop(t - s_lag))
            if 0 <= t - s_lag - 1 < n_units:
                accumulate(t - s_lag - 1, *p_vals.pop(t - s_lag - 1))
            lo, hi = (t * len(side_work)) // n_steps, ((t + 1) * len(side_work)) // n_steps
            for piece in side_work[lo:hi]:
                piece()

    n_b = jnp.minimum(nq - 2 - pair, nq - 3)
    blocks = []
    for j in range(nq - 3):
        if j < nq // 2 - 1:
            blocks.append((1, j, None))
        else:
            in_b = j < n_b
            blocks.append((jnp.where(in_b, 1, 0), jnp.where(in_b, j, j - n_b), None))
    first = pair == 0
    blocks.append((jnp.where(first, 1, 0), jnp.where(first, nq - 3, pair - 1),
                   jnp.where(first, BIAS_ZERO, BIAS_PREV)))
    blocks.append((0, pair, BIAS_DIAG))
    blocks.append((1, nq - 2 - pair, BIAS_PREV))
    blocks.append((1, nq - 1 - pair, BIAS_DIAG))
    run_blocks(blocks)

    lam = _lambda_full(lam_ref, lam0)
    for slot in range(2):
        rows = pl.ds(pl.multiple_of(tiles[slot] * blk, blk), blk)
        a = [acc_ref[c, slot, 0:DIFF_DV, :] / acc_ref[c, slot, DIFF_DV:DIFF_DV + 1, :]
             for c in range(n_chunks)]
        for g in range(DIFF_GROUP):
            ot = a[g] - lam * a[DIFF_GROUP + g]
            ot = ot * lax.rsqrt(jnp.mean(ot * ot, axis=0, keepdims=True) + NORM_EPS)
            ot = ot * sg_ref[...] * (1.0 - lam0)
            gate = gate_ref[rows, g * DIFF_DV:(g + 1) * DIFF_DV].astype(F32)
            o_ref[rows, g * DIFF_DV:(g + 1) * DIFF_DV] = (ot.T * gate).astype(o_ref.dtype)


def _attn_decode_pieces(k_refs, v_refs, qt_ref, kn_ref, vn_ref, bias_ref, bnew_ref, gate_ref,
                        sg_ref, lam_ref, o_ref, s_ref, p_ref, *, rows_per_page, lam0):
    n_pages = len(k_refs)
    w = rows_per_page
    state = {}

    def score_page(p):
        def run():
            kp = k_refs[p][...].astype(BF16)
            sp = lax.dot_general(qt_ref[...], kp, (((1,), (1,)), ((), ())),
                                 preferred_element_type=F32)
            s_ref[:, p * w:(p + 1) * w] = sp + bias_ref[:, p * w:(p + 1) * w]
        return run

    def softmax():
        s = s_ref[...]
        s_new = jnp.sum(qt_ref[...].astype(F32) * kn_ref[...].astype(F32), axis=1, keepdims=True)
        s_new = s_new + bnew_ref[:, 0:1]
        m = jnp.maximum(jnp.max(s, axis=1, keepdims=True), s_new)
        e_new = jnp.exp(s_new - m)
        pr = jnp.exp(s - m)
        state["l"] = jnp.sum(pr, axis=1, keepdims=True) + e_new
        state["acc"] = e_new * vn_ref[...].astype(F32)
        p_ref[...] = pr.astype(BF16)

    def value_page(p):
        def run():
            vp = v_refs[p][...].astype(BF16)
            state["acc"] = state["acc"] + jnp.dot(p_ref[:, p * w:(p + 1) * w], vp,
                                                  preferred_element_type=F32)
        return run

    def finish():
        a = state["acc"] / state["l"]
        lam = _lambda_full(lam_ref, lam0)
        o = a[:DIFF_HEADS] - lam * a[DIFF_HEADS:]
        o = o * lax.rsqrt(jnp.mean(o * o, axis=-1, keepdims=True) + NORM_EPS)
        o = o * sg_ref[...] * (1.0 - lam0)
        o_ref[...] = o * gate_ref[...]

    return ([score_page(p) for p in range(n_pages)] + [softmax]
            + [value_page(p) for p in range(n_pages)] + [finish])


N_PROMPT_IN = 7
N_DECODE_IN = 7


def _attn_kernel(pt_ref, *refs, n_pages, rows_per_page, blk, nq, lam0, layer):
    prompt_in = refs[:N_PROMPT_IN]
    ck_ref, cv_ref = refs[N_PROMPT_IN:N_PROMPT_IN + 2]
    decode_in = refs[N_PROMPT_IN + 2:N_PROMPT_IN + 2 + N_DECODE_IN]
    o_ref, od_ref, qt_s, m_s, acc_s, sd_s, pd_s, kbuf, vbuf, sem = refs[N_PROMPT_IN + 2 + N_DECODE_IN:]
    lam_ref = prompt_in[-1]
    n_steps = pl.num_programs(0) * pl.num_programs(1) * pl.num_programs(2)
    sid = (pl.program_id(0) * pl.num_programs(1) + pl.program_id(1)) * pl.num_programs(2) + pl.program_id(2)
    slot = sid % 2

    def page_copies(seq_id, to_slot):
        copies = []
        for pg in range(n_pages):
            page = pt_ref[seq_id, pg]
            copies.append(pltpu.make_async_copy(ck_ref.at[page, layer], kbuf.at[to_slot, pg],
                                                sem.at[to_slot, 0, pg]))
            copies.append(pltpu.make_async_copy(cv_ref.at[page, layer], vbuf.at[to_slot, pg],
                                                sem.at[to_slot, 1, pg]))
        return copies

    @pl.when(sid == 0)
    def _():
        for cp in page_copies(sid, slot):
            cp.start()

    @pl.when(sid + 1 < n_steps)
    def _():
        for cp in page_copies(sid + 1, 1 - slot):
            cp.start()

    for cp in page_copies(sid, slot):
        cp.wait()
    k_refs = [kbuf.at[slot, pg] for pg in range(n_pages)]
    v_refs = [vbuf.at[slot, pg] for pg in range(n_pages)]
    pieces = _attn_decode_pieces(k_refs, v_refs, *decode_in, lam_ref, od_ref, sd_s, pd_s,
                                 rows_per_page=rows_per_page, lam0=lam0)
    _attn_prompt_body(*prompt_in, o_ref, qt_s, m_s, acc_s, blk=blk, nq=nq, lam0=lam0,
                      side_work=pieces)


def _attention(qt_p, kb_p, vt_p, bias_p, gates_p, b, s, blk,
               dq, kb_new, vb_new, gate, cache_k, cache_v, page_table, bias, bias_new,
               subln_g, lam_rows, lam0, layer):
    nq = s // blk
    assert nq % 2 == 0 and nq >= 4
    bs = dq.shape[0]
    assert bs == b * DIFF_KV_HEADS * (nq // 2), "one sample sequence per prompt grid step"
    t = b * s
    gcol0 = RET_V_W // (DIFF_GROUP * DIFF_DV)
    n_pool, depth, page = cache_k.shape[0], cache_k.shape[1], cache_k.shape[2]
    n_pages = page_table.shape[1]
    w = page * DIFF_KV_HEADS
    ck = cache_k.reshape(n_pool, depth, w, 2 * DIFF_DH)
    cv = cache_v.reshape(n_pool, depth, w, DIFF_DV)
    q = (dq * (DIFF_DH ** -0.5)).reshape(bs, DIFF_KV_HEADS, DIFF_GROUP, 2, DIFF_DH)
    q = jnp.transpose(q, (0, 3, 1, 2, 4)).reshape(bs, 2, DIFF_HEADS, DIFF_DH)
    z = jnp.zeros_like(q[:, 0])
    qt = jnp.concatenate([jnp.concatenate([q[:, 0], z], axis=-1),
                          jnp.concatenate([z, q[:, 1]], axis=-1)], axis=1)
    kn = jnp.tile(jnp.repeat(kb_new.reshape(bs, DIFF_KV_HEADS, 2 * DIFF_DH), DIFF_GROUP, axis=1), (1, 2, 1))
    vn = jnp.tile(jnp.repeat(vb_new.reshape(bs, DIFF_KV_HEADS, DIFF_DV), DIFF_GROUP, axis=1), (1, 2, 1))
    g3 = gate.astype(F32).reshape(bs, DIFF_HEADS, DIFF_DV)
    rows = 2 * DIFF_HEADS

    npair = nq // 2

    def seq(i, h, p):
        return (i * DIFF_KV_HEADS + h) * npair + p

    whole_cache = pl.BlockSpec(memory_space=pl.ANY)
    per_seq = lambda i, h, p, pt: (i, h)
    const2 = lambda i, h, p, pt: (0, 0)
    row_spec = pl.BlockSpec((None, rows, LANES), lambda i, h, p, pt: (seq(i, h, p), 0, 0))
    head_spec = pl.BlockSpec((None, DIFF_HEADS, DIFF_DV), lambda i, h, p, pt: (seq(i, h, p), 0, 0))
    in_specs = ([pl.BlockSpec((nq, DIFF_GROUP * DIFF_DV, blk), lambda i, h, p, pt: (i, h, 0)),
                 pl.BlockSpec((s, 2 * DIFF_DH), per_seq),
                 pl.BlockSpec((nq, VT_ROWS, blk), lambda i, h, p, pt: (i, h, 0)),
                 pl.BlockSpec((3, DIFF_GROUP, blk, blk), lambda i, h, p, pt: (0, h, 0, 0)),
                 pl.BlockSpec((s, DIFF_GROUP * DIFF_DV), lambda i, h, p, pt: (i, gcol0 + h)),
                 pl.BlockSpec((DIFF_DV, 1), const2),
                 pl.BlockSpec((4, DIFF_DH), const2)]
                + [whole_cache, whole_cache]
                + [row_spec, row_spec, row_spec,
                   pl.BlockSpec((rows, n_pages * w), const2),
                   pl.BlockSpec((rows, LANES), const2),
                   head_spec,
                   pl.BlockSpec((1, DIFF_DV), const2)])
    assert len(in_specs) == N_PROMPT_IN + 2 + N_DECODE_IN
    do_p, do_s = pl.pallas_call(
        functools.partial(_attn_kernel, n_pages=n_pages, rows_per_page=w, blk=blk, nq=nq, lam0=lam0,
                          layer=layer),
        grid_spec=pltpu.PrefetchScalarGridSpec(
            num_scalar_prefetch=1, grid=(b, DIFF_KV_HEADS, npair), in_specs=in_specs,
            out_specs=[pl.BlockSpec((s, DIFF_GROUP * DIFF_DV), per_seq), head_spec],
            scratch_shapes=[pltpu.VMEM((4, 2, DIFF_DV, blk), BF16),
                            pltpu.VMEM((4, 2, 1, blk), F32),
                            pltpu.VMEM((4, 2, VT_ROWS, blk), F32),
                            pltpu.VMEM((rows, n_pages * w), F32),
                            pltpu.VMEM((rows, n_pages * w), BF16),
                            pltpu.VMEM((2, n_pages, w, 2 * DIFF_DH), F32),
                            pltpu.VMEM((2, n_pages, w, DIFF_DV), F32),
                            pltpu.SemaphoreType.DMA((2, 2, n_pages))]),
        out_shape=[jax.ShapeDtypeStruct((t, DIFF_OUT_W), BF16),
                   jax.ShapeDtypeStruct((bs, DIFF_HEADS, DIFF_DV), F32)],
        compiler_params=_cparams(("arbitrary", "arbitrary", "arbitrary"), 56),
        name="attention",
    )(page_table, qt_p, kb_p, vt_p, bias_p, gates_p, subln_g.reshape(DIFF_DV, 1), lam_rows,
      ck, cv, qt, kn, vn, bias, bias_new, g3,
      subln_g.reshape(1, DIFF_DV))
    return do_p, do_s.reshape(bs, DIFF_OUT_W).astype(BF16)


def _softplus(x):
    return jnp.maximum(x, 0.0) + jnp.log1p(jnp.exp(-jnp.abs(x)))


def _lru_coeffs(xc, wg_ref, ba_ref, bx_ref, lam_ref, n):
    sl = slice(n * RNN_BLOCK_W, (n + 1) * RNN_BLOCK_W)
    xn = xc[:, sl]
    z = jnp.dot(xn.astype(BF16), wg_ref[n], preferred_element_type=F32)
    rg = _sigmoid(z[:, :RNN_BLOCK_W] + ba_ref[:, sl])
    ig = _sigmoid(z[:, RNN_BLOCK_W:] + bx_ref[:, sl])
    log_a = -LRU_C * rg * _softplus(-lam_ref[:, sl])
    a = jnp.exp(log_a)
    y = -jnp.tanh(log_a) * (1.0 + a * a)
    root = jnp.where(y > 0.0, y * lax.rsqrt(y), 0.0)
    bt = root * (ig * xn)
    return a, bt


def _lru_tile(ti, tm, x_ref, gate_ref, cw_ref, cb_ref, wg_ref, ba_ref, bx_ref, lam_ref,
              o_ref, conv_ref, xbuf_ref, a_ref, b_ref, h_ref):
    pad = SUBLANES
    rows = slice(ti * tm, (ti + 1) * tm)
    xbuf_ref[pad:pad + tm, :] = x_ref[rows, :]
    xc = cb_ref[...]
    for j in range(CONV_WIDTH):
        off = pad - (CONV_WIDTH - 1) + j
        xc = xc + xbuf_ref[off:off + tm, :] * cw_ref[j:j + 1, :]
    conv_ref[...] = xbuf_ref[pad + tm - (CONV_WIDTH - 1):pad + tm, :]
    xbuf_ref[0:pad, :] = xbuf_ref[tm:tm + pad, :]

    for n in range(RNN_BLOCKS):
        a, bt = _lru_coeffs(xc, wg_ref, ba_ref, bx_ref, lam_ref, n)
        sl = slice(n * RNN_BLOCK_W, (n + 1) * RNN_BLOCK_W)
        a = a.reshape(tm // SUBLANES, SUBLANES, RNN_BLOCK_W)
        bt = bt.reshape(tm // SUBLANES, SUBLANES, RNN_BLOCK_W)
        r8 = lax.broadcasted_iota(jnp.int32, a.shape, 1)
        for d in (1, 2, 4):
            a_sh = pltpu.roll(a, d, 1)
            b_sh = pltpu.roll(bt, d, 1)
            ok = r8 >= d
            bt = jnp.where(ok, a * b_sh + bt, bt)
            a = jnp.where(ok, a * a_sh, a)
        a_ref[:, sl] = a.reshape(tm, RNN_BLOCK_W)
        b_ref[:, sl] = bt.reshape(tm, RNN_BLOCK_W)

    h = h_ref[...]
    for gi in range(tm // SUBLANES):
        rs = slice(gi * SUBLANES, (gi + 1) * SUBLANES)
        hg = a_ref[rs, :] * h + b_ref[rs, :]
        b_ref[rs, :] = hg
        h = hg[SUBLANES - 1:SUBLANES, :]
    h_ref[...] = h
    o_ref[rows, :] = (b_ref[...] * gate_ref[rows, :].astype(F32)).astype(o_ref.dtype)


def _ret_lru_kernel(qk_ref, v_ref, gr_ref, dmask_ref, qdec_ref, kdec_ref,
                    x_ref, gl_ref, cw_ref, cb_ref, wg_ref, ba_ref, bx_ref, lam_ref,
                    o_ret_ref, sfin_ref, o_lru_ref, hlast_ref, conv_ref,
                    state_ref, xbuf_ref, a_ref, b_ref, h_ref, *, chunk_dec, tm):
    j = pl.program_id(1)

    @pl.when(j == 0)
    def _():
        state_ref[...] = jnp.zeros_like(state_ref)
        xbuf_ref[0:SUBLANES, :] = jnp.zeros((SUBLANES, RNN_WIDTH), F32)
        h_ref[...] = jnp.zeros(h_ref.shape, F32)

    rows = qk_ref.shape[0]
    n_chunks = rows // dmask_ref.shape[-1]
    n_tiles = rows // tm
    ret = [functools.partial(_retention_chunk, ci, qk_ref, v_ref, gr_ref, dmask_ref, qdec_ref,
                             kdec_ref, o_ret_ref, state_ref, chunk_dec) for ci in range(n_chunks)]
    lru = [functools.partial(_lru_tile, ti, tm, x_ref, gl_ref, cw_ref, cb_ref, wg_ref, ba_ref, bx_ref,
                             lam_ref, o_lru_ref, conv_ref, xbuf_ref, a_ref, b_ref, h_ref)
           for ti in range(n_tiles)]
    per_tile = n_chunks // n_tiles
    for ti in range(n_tiles):
        ret[ti * per_tile]()
        lru[ti]()
        for piece in ret[ti * per_tile + 1:(ti + 1) * per_tile]:
            piece()
    hlast_ref[...] = h_ref[...]

    @pl.when(j == pl.num_programs(1) - 1)
    def _():
        sfin_ref[...] = state_ref[...]


def _ret_lru_prompt(qk, rv, gates, lx, conv_w, conv_b, wg, ba, bx, lam, b, s, tm):
    c = math.gcd(s, RET_CHUNK)
    dmask, qdec, kdec, chunk_dec = _retention_decays(c)
    rows = c * math.gcd(s // c, RET_CHUNKS_PER_STEP)
    assert rows % tm == 0 and (rows // c) % (rows // tm) == 0
    nt = s // rows
    t = b * s
    lcol = 2
    row_blk = lambda w, col=0: pl.BlockSpec((rows, w), lambda i, j: (i * nt + j, col))
    const2 = lambda shape: pl.BlockSpec(shape, lambda i, j: (0, 0))
    vec = const2((1, RNN_WIDTH))
    return pl.pallas_call(
        functools.partial(_ret_lru_kernel, chunk_dec=chunk_dec, tm=tm),
        grid=(b, nt),
        in_specs=[row_blk(2 * RET_QK_W), row_blk(RET_V_W), row_blk(RET_V_W),
                  pl.BlockSpec((RET_HEADS, c, c), lambda i, j: (0, 0, 0)),
                  const2((c, RET_HEADS)), const2((c, RET_HEADS)),
                  row_blk(RNN_WIDTH), row_blk(RNN_WIDTH, lcol),
                  const2((CONV_WIDTH, RNN_WIDTH)), vec,
                  pl.BlockSpec((RNN_BLOCKS, RNN_BLOCK_W, 2 * RNN_BLOCK_W), lambda i, j: (0, 0, 0)),
                  vec, vec, vec],
        out_specs=[row_blk(RET_V_W),
                   pl.BlockSpec((None, RET_HEADS, RET_DK, RET_DV), lambda i, j: (i, 0, 0, 0)),
                   row_blk(RNN_WIDTH),
                   pl.BlockSpec((None, 1, RNN_WIDTH), lambda i, j: (i, 0, 0)),
                   pl.BlockSpec((None, CONV_WIDTH - 1, RNN_WIDTH), lambda i, j: (i, 0, 0))],
        out_shape=[jax.ShapeDtypeStruct((t, RET_V_W), BF16),
                   jax.ShapeDtypeStruct((b, RET_HEADS, RET_DK, RET_DV), F32),
                   jax.ShapeDtypeStruct((t, RNN_WIDTH), BF16),
                   jax.ShapeDtypeStruct((b, 1, RNN_WIDTH), F32),
                   jax.ShapeDtypeStruct((b, CONV_WIDTH - 1, RNN_WIDTH), F32)],
        scratch_shapes=[pltpu.VMEM((RET_HEADS, RET_DK, RET_DV), F32),
                        pltpu.VMEM((tm + SUBLANES, RNN_WIDTH), F32),
                        pltpu.VMEM((tm, RNN_WIDTH), F32),
                        pltpu.VMEM((tm, RNN_WIDTH), F32),
                        pltpu.VMEM((1, RNN_WIDTH), F32)],
        compiler_params=_cparams(("parallel", "arbitrary"), 40),
        name="ret_lru_prompt",
    )(qk, rv, gates, dmask, qdec, kdec, lx, gates, conv_w, conv_b.reshape(1, -1), wg,
      ba.reshape(1, -1), bx.reshape(1, -1), lam.reshape(1, -1))


def _lru_step_kernel(x_ref, c0_ref, c1_ref, c2_ref, h0_ref, gate_ref, cw_ref, cb_ref, wg_ref,
                     ba_ref, bx_ref, lam_ref, o_ref, h_ref):
    xc = (cb_ref[...] + c0_ref[...] * cw_ref[0:1, :] + c1_ref[...] * cw_ref[1:2, :]
          + c2_ref[...] * cw_ref[2:3, :] + x_ref[...] * cw_ref[3:4, :])
    for n in range(RNN_BLOCKS):
        a, bt = _lru_coeffs(xc, wg_ref, ba_ref, bx_ref, lam_ref, n)
        sl = slice(n * RNN_BLOCK_W, (n + 1) * RNN_BLOCK_W)
        h = a * h0_ref[:, sl] + bt
        h_ref[:, sl] = h
        o_ref[:, sl] = (h * gate_ref[:, sl].astype(F32)).astype(o_ref.dtype)


def _lru_step(lx, conv_prev, h0, gate, conv_w, conv_b, wg, ba, bx, lam):
    bs = lx.shape[0]
    vm = pl.BlockSpec(memory_space=pltpu.VMEM)
    return pl.pallas_call(
        _lru_step_kernel,
        in_specs=[vm] * 12,
        out_specs=[vm, vm],
        out_shape=[jax.ShapeDtypeStruct((bs, RNN_WIDTH), BF16),
                   jax.ShapeDtypeStruct((bs, RNN_WIDTH), F32)],
        compiler_params=pltpu.CompilerParams(vmem_limit_bytes=32 * MIB),
        name="lru_step",
    )(lx, conv_prev[:, 0], conv_prev[:, 1], conv_prev[:, 2], h0, gate, conv_w,
      conv_b.reshape(1, -1), wg, ba.reshape(1, -1), bx.reshape(1, -1), lam.reshape(1, -1))


def _merge_kernel(ro_ref, do_ref, lo_ref, mg_ref, x_ref, p_ref, wr_ref, wd_ref, wl_ref, wo_ref,
                  wpg_ref, wp_ref, gn_ref, x_out_ref, hn_ref):
    d = D_MODEL
    tm = x_ref.shape[0]
    chunk = min(MERGE_CHUNK, tm)
    chunks = [slice(c * chunk, (c + 1) * chunk) for c in range(tm // chunk)]
    dot = functools.partial(jnp.dot, preferred_element_type=F32)
    merged = [(mg_ref[rs, 0:d].astype(F32) * dot(ro_ref[rs, :], wr_ref[...])
               + mg_ref[rs, d:2 * d].astype(F32) * dot(do_ref[rs, :], wd_ref[...])
               + mg_ref[rs, 2 * d:3 * d].astype(F32) * dot(lo_ref[rs, :], wl_ref[...])).astype(BF16)
              for rs in chunks]
    ple = [dot(p_ref[rs, :].astype(BF16), wp_ref[...]) for rs in chunks]
    x1 = [x_ref[rs, :] + dot(m, wo_ref[...]) for rs, m in zip(chunks, merged)]
    gate = [dot(v.astype(BF16), wpg_ref[...]) for v in x1]
    for rs, v, g, e in zip(chunks, x1, gate, ple):
        x2 = v + _sigmoid(g) * e
        x_out_ref[rs, :] = x2
        y = x2 * lax.rsqrt(jnp.mean(x2 * x2, axis=-1, keepdims=True) + NORM_EPS)
        hn_ref[rs, :] = (y * gn_ref[...]).astype(hn_ref.dtype)


def _merge(ro, do, lo, mg, x, p_all, layer, wr, wd, wl, wo, wpg, wp, g_next, hn_dtype, tm):
    t = x.shape[0]
    d = D_MODEL
    row = lambda wdt: pl.BlockSpec((tm, wdt), lambda i: (i, 0))
    full = lambda a: pl.BlockSpec(a.shape, lambda i: (0, 0), pipeline_mode=pl.Buffered(1))
    return pl.pallas_call(
        _merge_kernel,
        grid=(t // tm,),
        in_specs=[row(RET_V_W), row(DIFF_OUT_W), row(RNN_WIDTH), row(3 * d), row(d),
                  pl.BlockSpec((None, tm, D_PLE), lambda i: (layer, i, 0)),
                  full(wr), full(wd), full(wl), full(wo), full(wpg), full(wp),
                  pl.BlockSpec((1, d), lambda i: (0, 0))],
        out_specs=[row(d), row(d)],
        out_shape=[jax.ShapeDtypeStruct((t, d), F32), jax.ShapeDtypeStruct((t, d), hn_dtype)],
        compiler_params=_cparams(("parallel",), 56),
        name="merge",
    )(ro, do, lo, mg, x, p_all, wr, wd, wl, wo, wpg, wp, g_next.reshape(1, d))


def _lambda_init(layer):
    return 0.8 - 0.6 * math.exp(-0.3 * layer)


def _rope_tables(pos):
    half = RET_DK // 2
    inv = ROPE_BASE ** (-jnp.arange(half, dtype=F32) / half)
    ang = pos.astype(F32)[:, None] * inv[None, :]
    cos = jnp.cos(ang)
    sin = jnp.sin(ang)
    return jnp.concatenate([cos, cos], axis=-1), jnp.concatenate([-sin, sin], axis=-1)


def kernel(x_prompt, x_sample, cache_k, cache_v, state_ret, state_lru, state_conv, page_table, p_prompt, p_sample, rel_bias, norm_g, w_in, lambda_q1, lambda_k1, lambda_q2, lambda_k2, subln_g, conv_w, conv_b, gate_a_w, gate_a_b, gate_x_w, gate_x_b, lru_lambda, w_ret_out, w_diff_out, w_lru_out, w_o, w_ple, w_ple_gate, final_norm_g):
    bp, sp, d = x_prompt.shape
    bs, ss, _ = x_sample.shape
    assert d == D_MODEL and ss == 1
    depth = w_in.shape[0]
    n_pages, page = page_table.shape[1], cache_k.shape[2]
    past = n_pages * page
    tp = bp * sp
    blk = min(ATTN_BLOCK, sp)
    tm_p = min(PROJ_ROWS, sp)
    tm_qkv = min(QKV_ROWS, sp)
    tm_lru = min(LRU_ROWS, sp)
    tm_merge = min(MERGE_ROWS, sp)
    p_prompt_all = p_prompt.reshape(depth, tp, D_PLE)
    p_sample_all = p_sample.reshape(depth, bs, D_PLE)

    cos_p, sin_p = _rope_tables(jnp.arange(sp, dtype=jnp.int32))
    cos_s, sin_s = _rope_tables(jnp.full((bs,), past, dtype=jnp.int32))
    bias_p = _bias_prompt(rel_bias, blk)
    bias_s, bias_s_new = _bias_sample(rel_bias, past)

    w_in_b = w_in.astype(BF16)
    wg_all = jnp.concatenate([gate_a_w, gate_x_w], axis=-1).astype(BF16)
    xp = x_prompt.reshape(tp, d)
    xs = x_sample.reshape(bs, d)
    hn_p = _rmsnorm(xp, norm_g[0], BF16, tm_p)
    hn_s = _rmsnorm(xs, norm_g[0], BF16, bs)

    outs = {k: [] for k in ("ks", "vs", "rp", "lp", "ls", "cp", "cs")}
    kp_all = jnp.zeros((bp, depth, sp * DIFF_KV_HEADS, 2 * DIFF_DH), F32)
    vp_all = jnp.zeros((bp, depth, sp * DIFF_KV_HEADS, DIFF_DV), F32)
    rs_all = jnp.zeros(state_ret.shape, F32)
    for l in range(depth):
        wg = wg_all[l]
        wr, wd, wl = w_ret_out[l].astype(BF16), w_diff_out[l].astype(BF16), w_lru_out[l].astype(BF16)
        wo, wpg, wp = w_o[l].astype(BF16), w_ple_gate[l].astype(BF16), w_ple[l].astype(BF16)
        lam0 = _lambda_init(l)
        lam_rows = jnp.stack([lambda_q1[l], lambda_k1[l], lambda_q2[l], lambda_k2[l]], axis=0)
        last = l == depth - 1
        g_next = final_norm_g if last else norm_g[l + 1]
        hn_dtype = F32 if last else BF16

        qk = _proj(hn_p, w_in_b, l, COL_ROT, "rotary", tm_p, tables=(cos_p, sin_p))
        rv = _proj(hn_p, w_in_b, l, COL_RV, "plain", tm_p)
        gates = _proj(hn_p, w_in_b, l, COL_SILU, "silu", tm_p)
        kp_all, vp_all, kb, qt, vt = _proj_qkv_t(hn_p, w_in_b, l, tm_qkv, blk, kp_all, vp_all)
        lx = _proj(hn_p, w_in_b, l, COL_LX, "plain", tm_p, out_dtype=F32)
        mg = _proj(hn_p, w_in_b, l, COL_MG, "sigmoid", tm_p)
        qk_s, rvdq_s, gates_s, kf_s, vf_s, kb_s, vb_s, lx_s, mg_s = _proj_sample(
            hn_s, w_in_b, l, cos_s, sin_s)

        do, do_s = _attention(qt, kb, vt, bias_p, gates, bp, sp, blk,
                              rvdq_s[:, RET_V_W:], kb_s, vb_s, gates_s[:, RET_V_W:RET_V_W + DIFF_OUT_W],
                              cache_k, cache_v, page_table, bias_s, bias_s_new, subln_g[l],
                              lam_rows, lam0, l)

        ro, s_fin, lo, h_last, conv_new = _ret_lru_prompt(
            qk, rv, gates, lx, conv_w[l], conv_b[l], wg, gate_a_b[l], gate_x_b[l], lru_lambda[l],
            bp, sp, tm_lru)
        xp, hn_p = _merge(ro, do, lo, mg, xp, p_prompt_all, l, wr, wd, wl, wo, wpg, wp,
                          g_next, hn_dtype, tm_merge)
        outs["rp"].append(s_fin)
        outs["lp"].append(h_last.reshape(bp, RNN_WIDTH))
        outs["cp"].append(conv_new)

        lx = lx_s
        ro, rs_all = _retention_step(qk_s, rvdq_s, gates_s, state_ret, rs_all, l)
        conv_prev = state_conv[:, l]
        lo, h_new = _lru_step(lx, conv_prev, state_lru[:, l], gates_s[:, 2 * RET_V_W:], conv_w[l],
                              conv_b[l], wg, gate_a_b[l], gate_x_b[l], lru_lambda[l])
        xs, hn_s = _merge(ro, do_s, lo, mg_s, xs, p_sample_all, l, wr, wd, wl, wo, wpg, wp,
                          g_next, hn_dtype, bs)
        outs["ks"].append(kf_s.reshape(bs, 1, DIFF_KV_HEADS, 2 * DIFF_DH))
        outs["vs"].append(vf_s.reshape(bs, 1, DIFF_KV_HEADS, DIFF_DV))
        outs["ls"].append(h_new)
        outs["cs"].append(jnp.concatenate([conv_prev[:, 1:], lx[:, None, :]], axis=1))

    st = lambda key: jnp.stack(outs[key], axis=1)
    return (hn_p.reshape(bp, sp, d), hn_s.reshape(bs, 1, d),
            kp_all.reshape(bp, depth, sp, DIFF_KV_HEADS, 2 * DIFF_DH),
            vp_all.reshape(bp, depth, sp, DIFF_KV_HEADS, DIFF_DV), st("ks"), st("vs"),
            st("rp"), rs_all, st("lp"), st("ls"), st("cp"), st("cs"))
```
